```python
import math
import jax, jax.numpy as jnp
from jax import lax
import numpy as np

D_MODEL = 2048
BATCH = 4
SEQ = 2048
DEPTH = 2

ROPE_THETA = 10000.0
ROPE_DIM = 64
Q_BLOCK = 128
NORM_EPS = 1e-6
NEG_INF = -1e30

DA_HEADS = 4
DA_QK_DIM = 64
DA_V_DIM = 128
DA_WIDTH = DA_HEADS * DA_V_DIM
DA_SUBLN_EPS = 1e-5

SW_HEADS = 16
SW_KV_HEADS = 2
SW_HEAD_DIM = 64
SW_WINDOW = 128
SW_WIDTH = SW_HEADS * SW_HEAD_DIM
SW_KV_WIDTH = SW_KV_HEADS * SW_HEAD_DIM

MLA_HEADS = 4
MLA_NOPE_DIM = 128
MLA_ROPE_DIM = 64
MLA_V_DIM = 128
MLA_Q_RANK = 512
MLA_KV_RANK = 512
MLA_WIDTH = MLA_HEADS * MLA_V_DIM

D_MIX = DA_WIDTH + SW_WIDTH + MLA_WIDTH

IN_SPLITS = (
    2 * DA_HEADS * DA_QK_DIM, 2 * DA_HEADS * DA_QK_DIM, DA_WIDTH, DA_WIDTH,
    SW_WIDTH, SW_KV_WIDTH, SW_KV_WIDTH, SW_WIDTH,
    MLA_Q_RANK, MLA_KV_RANK, MLA_ROPE_DIM, MLA_WIDTH,
)
D_IN = 5952

kernel_name = "hybrid_diffattn_swa_sink_mla_parallel_heads"


def rms_norm(x, w, eps=NORM_EPS):
    xf = x.astype(jnp.float32)
    xf = xf * lax.rsqrt(jnp.mean(xf * xf, axis=-1, keepdims=True) + eps)
    return xf.astype(x.dtype) * w


def rope_tables(positions, dim):
    inv_freq = jnp.power(ROPE_THETA, -jnp.arange(0, dim, 2, dtype=jnp.float32) / dim)
    ang = positions.astype(jnp.float32)[..., None] * inv_freq
    return jnp.cos(ang), jnp.sin(ang)


def apply_rope(x, cos, sin):
    half = x.shape[-1] // 2
    x1, x2 = x[..., :half], x[..., half:]
    c, s = cos.astype(x.dtype), sin.astype(x.dtype)
    return jnp.concatenate([x1 * c - x2 * s, x2 * c + x1 * s], axis=-1)


def to_query_blocks(q):
    B, S = q.shape[:2]
    nb = S // Q_BLOCK
    qb = q.reshape((B, nb, Q_BLOCK) + q.shape[2:])
    return jnp.moveaxis(qb, 1, 0), nb


def from_query_blocks(o):
    o = jnp.moveaxis(o, 0, 1)
    return o.reshape((o.shape[0], o.shape[1] * o.shape[2]) + o.shape[3:])


def diff_attention(q, k, v, lam, subln_w, lam_init):
    S = q.shape[1]
    scale = q.shape[-1] ** -0.5
    qb, nb = to_query_blocks(q)
    k_pos = jnp.arange(S)

    def block(args):
        qblk, i = args
        s = jnp.einsum('bqhcd,bkhcd->bhcqk', qblk, k).astype(jnp.float32) * scale
        q_pos = i * Q_BLOCK + jnp.arange(Q_BLOCK)
        s = jnp.where(k_pos[None, :] <= q_pos[:, None], s, NEG_INF)
        p = jax.nn.softmax(s, axis=-1)
        a = (p[:, :, 0] - lam * p[:, :, 1]).astype(v.dtype)
        return jnp.einsum('bhqk,bkhd->bqhd', a, v)

    o = from_query_blocks(lax.map(block, (qb, jnp.arange(nb))))
    o = rms_norm(o, subln_w, DA_SUBLN_EPS) * (1.0 - lam_init)
    return o.reshape(o.shape[0], o.shape[1], -1)


def causal_attention(q, k, v, scale):
    S = q.shape[1]
    qb, nb = to_query_blocks(q)
    k_pos = jnp.arange(S)

    def block(args):
        qblk, i = args
        s = jnp.einsum('bqhd,bkhd->bhqk', qblk, k).astype(jnp.float32) * scale
        q_pos = i * Q_BLOCK + jnp.arange(Q_BLOCK)
        s = jnp.where(k_pos[None, :] <= q_pos[:, None], s, NEG_INF)
        p = jax.nn.softmax(s, axis=-1).astype(v.dtype)
        return jnp.einsum('bhqk,bkhd->bqhd', p, v)

    o = from_query_blocks(lax.map(block, (qb, jnp.arange(nb))))
    return o.reshape(o.shape[0], o.shape[1], -1)


def sliding_window_attention(q, k, v, sinks):
    B, S, Hq, d = q.shape
    Hkv = k.shape[2]
    rep = Hq // Hkv
    W = SW_WINDOW
    nb = S // W
    qb = q.reshape(B, nb, W, Hkv, rep, d)

    def band(t):
        tb = t.reshape(B, nb, W, Hkv, d)
        prev = jnp.concatenate([jnp.zeros_like(tb[:, :1]), tb[:, :-1]], axis=1)
        return jnp.concatenate([prev, tb], axis=2)

    kb, vb = band(k), band(v)
    s = jnp.einsum('bnqgrd,bnkgd->bngrqk', qb, kb).astype(jnp.float32) * (d ** -0.5)
    qi = jnp.arange(W)[:, None]
    kj = jnp.arange(2 * W)[None, :]
    rel = qi + W - kj
    in_band = (rel >= 0) & (rel < W)
    has_prev = (jnp.arange(nb) > 0)[:, None, None] | (kj >= W)[None]
    mask = in_band[None] & has_prev
    s = jnp.where(mask[None, :, None, None], s, NEG_INF)
    sink = jnp.broadcast_to(sinks.astype(jnp.float32).reshape(1, 1, Hkv, rep, 1, 1),
                            s.shape[:-1] + (1,))
    p = jax.nn.softmax(jnp.concatenate([s, sink], axis=-1), axis=-1)[..., :-1].astype(v.dtype)
    o = jnp.einsum('bngrqk,bnkgd->bnqgrd', p, vb)
    return o.reshape(B, S, Hq * d)


def hybrid_layer(x, cos, sin, layer, pre_w, post_w, w_in, lq1, lk1, lq2, lk2, subln_w,
                 sinks, q_norm_w, kv_norm_w, w_uq, w_ukv, w_out):
    B, S, _ = x.shape
    h = rms_norm(x, pre_w)
    proj = h @ w_in
    offsets = np.cumsum(IN_SPLITS)[:-1].tolist()
    a_q, a_k, a_v, a_g, b_q, b_k, b_v, b_g, c_q, c_kv, c_kr, c_g = jnp.split(proj, offsets, axis=-1)

    cos4, sin4 = cos[:, :, None, :], sin[:, :, None, :]
    cos5, sin5 = cos4[:, :, :, None], sin4[:, :, :, None]

    qa = apply_rope(a_q.reshape(B, S, DA_HEADS, 2, DA_QK_DIM), cos5, sin5)
    ka = apply_rope(a_k.reshape(B, S, DA_HEADS, 2, DA_QK_DIM), cos5, sin5)
    va = a_v.reshape(B, S, DA_HEADS, DA_V_DIM)
    lam_init = 0.8 - 0.6 * math.exp(-0.3 * layer)
    lam = (jnp.exp(jnp.sum(lq1.astype(jnp.float32) * lk1.astype(jnp.float32)))
           - jnp.exp(jnp.sum(lq2.astype(jnp.float32) * lk2.astype(jnp.float32))) + lam_init)
    y_a = diff_attention(qa, ka, va, lam, subln_w, lam_init)

    qb = apply_rope(b_q.reshape(B, S, SW_HEADS, SW_HEAD_DIM), cos4, sin4)
    kb = apply_rope(b_k.reshape(B, S, SW_KV_HEADS, SW_HEAD_DIM), cos4, sin4)
    vb = b_v.reshape(B, S, SW_KV_HEADS, SW_HEAD_DIM)
    y_b = sliding_window_attention(qb, kb, vb, sinks)

    qc = (rms_norm(c_q, q_norm_w) @ w_uq).reshape(B, S, MLA_HEADS, MLA_NOPE_DIM + MLA_ROPE_DIM)
    q_nope, q_rope = qc[..., :MLA_NOPE_DIM], apply_rope(qc[..., MLA_NOPE_DIM:], cos4, sin4)
    kvc = (rms_norm(c_kv, kv_norm_w) @ w_ukv).reshape(B, S, MLA_HEADS, MLA_NOPE_DIM + MLA_V_DIM)
    k_nope, vc = kvc[..., :MLA_NOPE_DIM], kvc[..., MLA_NOPE_DIM:]
    k_rope = apply_rope(c_kr[:, :, None, :], cos4, sin4)
    qc = jnp.concatenate([q_nope, q_rope], axis=-1)
    kc = jnp.concatenate([k_nope, jnp.broadcast_to(k_rope, (B, S, MLA_HEADS, MLA_ROPE_DIM))], axis=-1)
    y_c = causal_attention(qc, kc, vc, (MLA_NOPE_DIM + MLA_ROPE_DIM) ** -0.5)

    y = jnp.concatenate([y_a, y_b, y_c], axis=-1) * jax.nn.silu(jnp.concatenate([a_g, b_g, c_g], axis=-1))
    out = y @ w_out
    return x + rms_norm(out, post_w)


def setup_inputs(seed: int = 0) -> dict:
    key = jax.random.key(seed)
    ks = jax.random.split(key, 16)
    f32 = jnp.float32
    L = DEPTH

    def nrm(k, shape, scale):
        return jax.random.normal(k, shape, f32) * scale

    return {
        "x": nrm(ks[0], (BATCH, SEQ, D_MODEL), 1.0),
        "positions": jnp.broadcast_to(jnp.arange(SEQ, dtype=jnp.int32), (BATCH, SEQ)),
        "pre_norm_w": 1.0 + nrm(ks[1], (L, D_MODEL), 0.02),
        "post_norm_w": 1.0 + nrm(ks[2], (L, D_MODEL), 0.02),
        "w_in": nrm(ks[3], (L, D_MODEL, D_IN), D_MODEL ** -0.5),
        "diff_lambda_q1": nrm(ks[4], (L, DA_QK_DIM), 0.1),
        "diff_lambda_k1": nrm(ks[5], (L, DA_QK_DIM), 0.1),
        "diff_lambda_q2": nrm(ks[6], (L, DA_QK_DIM), 0.1),
        "diff_lambda_k2": nrm(ks[7], (L, DA_QK_DIM), 0.1),
        "diff_subln_w": 1.0 + nrm(ks[8], (L, DA_V_DIM), 0.02),
        "sink_logits": nrm(ks[9], (L, SW_HEADS), 0.5),
        "mla_q_norm_w": 1.0 + nrm(ks[10], (L, MLA_Q_RANK), 0.02),
        "mla_kv_norm_w": 1.0 + nrm(ks[11], (L, MLA_KV_RANK), 0.02),
        "w_uq": nrm(ks[12], (L, MLA_Q_RANK, MLA_HEADS * (MLA_NOPE_DIM + MLA_ROPE_DIM)), MLA_Q_RANK ** -0.5),
        "w_ukv": nrm(ks[13], (L, MLA_KV_RANK, MLA_HEADS * (MLA_NOPE_DIM + MLA_V_DIM)), MLA_KV_RANK ** -0.5),
        "w_out": nrm(ks[14], (L, D_MIX, D_MODEL), D_MIX ** -0.5),
    }


def reference(x, positions, pre_norm_w, post_norm_w, w_in, diff_lambda_q1, diff_lambda_k1,
              diff_lambda_q2, diff_lambda_k2, diff_subln_w, sink_logits, mla_q_norm_w,
              mla_kv_norm_w, w_uq, w_ukv, w_out):
    cos, sin = rope_tables(positions, ROPE_DIM)
    for layer in range(DEPTH):
        x = hybrid_layer(x, cos, sin, layer, pre_norm_w[layer], post_norm_w[layer], w_in[layer],
                         diff_lambda_q1[layer], diff_lambda_k1[layer], diff_lambda_q2[layer],
                         diff_lambda_k2[layer], diff_subln_w[layer], sink_logits[layer],
                         mla_q_norm_w[layer], mla_kv_norm_w[layer], w_uq[layer], w_ukv[layer],
                         w_out[layer])
    return x
```

```python
import functools
import math

import jax
import jax.numpy as jnp
import numpy as np
from jax import lax
from jax.experimental import pallas as pl
from jax.experimental.pallas import tpu as pltpu

D_MODEL = 2048
ROPE_THETA = 10000.0
NORM_EPS = 1e-6
NEG_INF = -1e30
DA_SUBLN_EPS = 1e-5
SW_WINDOW = 128
LANES = 128

IN_TILE = 1024
N_SLABS = 48
SLAB_AQ, SLAB_AK, SLAB_BQ = 0, 4, 8
SLAB_BK, SLAB_CKR, SLAB_BV, SLAB_AV = 16, 17, 18, 20
SLAB_CQ, SLAB_CKV = 24, 28
SLAB_AG, SLAB_BG, SLAB_CG = 32, 36, 44

VMEM_LIMIT = 48 * 1024 * 1024

_NT = (((1,), (1,)), ((), ()))


def _cparams(sem):
    return pltpu.CompilerParams(dimension_semantics=sem, vmem_limit_bytes=VMEM_LIMIT)


def _rope_table_kernel(pos_ref, freq_ref, cos_ref, sa_ref, sb_ref):
    ang = pos_ref[...].astype(jnp.float32) * freq_ref[...]
    c, s = jnp.cos(ang), jnp.sin(ang)
    lane = lax.broadcasted_iota(jnp.int32, ang.shape, 1)
    first_half = (lane % 64) < 32
    cos_ref[...] = c
    sa_ref[...] = jnp.where(first_half, -s, 0.0)
    sb_ref[...] = jnp.where(first_half, 0.0, s)


def _rope_tables(positions):
    m = positions.size
    pos_b = jnp.broadcast_to(positions.reshape(m, 1), (m, LANES))
    inv_freq = jnp.power(ROPE_THETA, -jnp.arange(0, 64, 2, dtype=jnp.float32) / 64)
    freq = jnp.tile(inv_freq, 4).reshape(1, LANES)
    tm = 1024
    spec = pl.BlockSpec((tm, LANES), lambda i: (i, 0))
    return pl.pallas_call(
        _rope_table_kernel,
        grid=(m // tm,),
        in_specs=[spec, pl.BlockSpec((1, LANES), lambda i: (0, 0))],
        out_specs=[spec, spec, spec],
        out_shape=[jax.ShapeDtypeStruct((m, LANES), jnp.float32)] * 3,
        compiler_params=_cparams(("parallel",)),
        name="rope_tables",
    )(pos_b, freq)


def _rope(a, cos, sa, sb):
    return a * cos + pltpu.roll(a, 96, 1) * sa + pltpu.roll(a, 32, 1) * sb


def _reorder_w_in(w):
    z = jnp.zeros((w.shape[0], 64), w.dtype)
    z2 = jnp.zeros((w.shape[0], 128), w.dtype)
    pieces = [
        w[:, 0:512], w[:, 512:1024],
        w[:, 2048:3072],
        w[:, 3072:3200], w[:, 5376:5440], z,
        w[:, 3200:3328], z2, w[:, 1024:1536],
        w[:, 4352:4864], w[:, 4864:5376],
        w[:, 1536:2048], w[:, 3328:4352], w[:, 5440:5952],
    ]
    return jnp.concatenate(pieces, axis=1).astype(jnp.bfloat16)


def _in_proj_kernel(x_ref, prew_ref, w_ref, cos_ref, sa_ref, sb_ref, o_ref, h_ref):
    j = pl.program_id(1)

    @pl.when(j == 0)
    def _():
        x = x_ref[...]
        ms = jnp.mean(x * x, axis=-1, keepdims=True)
        h_ref[...] = (x * lax.rsqrt(ms + NORM_EPS) * prew_ref[...]).astype(jnp.bfloat16)

    def epilogue(kinds):
        acc = jnp.dot(h_ref[...], w_ref[...], preferred_element_type=jnp.float32)
        for s, kind in enumerate(kinds):
            a = acc[:, s * LANES:(s + 1) * LANES]
            if kind in ("rope", "rope_q"):
                a = _rope(a, cos_ref[...], sa_ref[...], sb_ref[...])
                if kind == "rope_q":
                    a = a * 0.125
            elif kind == "silu":
                a = a * (1.0 / (1.0 + jnp.exp(-a)))
            o_ref[:, s * LANES:(s + 1) * LANES] = a.astype(o_ref.dtype)

    pl.when(j == 0)(lambda: epilogue(["rope_q"] * 4 + ["rope"] * 4))
    pl.when(j == 1)(lambda: epilogue(["rope_q"] * 8))
    pl.when(j == 2)(lambda: epilogue(["rope"] * 2 + ["plain"] * 6))
    pl.when(j == 3)(lambda: epilogue(["plain"] * 8))
    pl.when(j >= 4)(lambda: epilogue(["silu"] * 8))


def _in_proj(x2, pre_w, w_in_r, cos, sa, sb):
    m = x2.shape[0]
    tm, tn = 1024, IN_TILE
    tab = pl.BlockSpec((tm, LANES), lambda i, j: (i, 0))
    return pl.pallas_call(
        _in_proj_kernel,
        grid=(m // tm, N_SLABS * LANES // tn),
        in_specs=[
            pl.BlockSpec((tm, D_MODEL), lambda i, j: (i, 0)),
            pl.BlockSpec((1, D_MODEL), lambda i, j: (0, 0)),
            pl.BlockSpec((D_MODEL, tn), lambda i, j: (0, j)),
            tab, tab, tab,
        ],
        out_specs=pl.BlockSpec((tm, tn), lambda i, j: (i, j)),
        out_shape=jax.ShapeDtypeStruct((m, N_SLABS * LANES), jnp.bfloat16),
        scratch_shapes=[pltpu.VMEM((tm, D_MODEL), jnp.bfloat16)],
        compiler_params=_cparams(("parallel", "arbitrary")),
        name="in_proj",
    )(x2, pre_w.reshape(1, D_MODEL), w_in_r, cos, sa, sb)


def _flash(q_variants, k_ref, v_ref, qi, tq):
    dv = v_ref.shape[-1]

    def step(j, carry, masked):
        start = pl.multiple_of(j * tq, tq)
        kj = k_ref[pl.ds(start, tq), :]
        vj = v_ref[pl.ds(start, tq), :]
        out = []
        for q, (m, l, acc) in zip(q_variants, carry):
            s = lax.dot_general(q, kj, _NT, preferred_element_type=jnp.float32)
            if masked:
                row = lax.broadcasted_iota(jnp.int32, s.shape, 0)
                col = lax.broadcasted_iota(jnp.int32, s.shape, 1)
                s = jnp.where(col <= row, s, NEG_INF)
            m_new = jnp.maximum(m, jnp.max(s, axis=-1, keepdims=True))
            alpha = jnp.exp(m - m_new)
            p = jnp.exp(s - m_new)
            l = alpha * l + jnp.sum(p, axis=-1, keepdims=True)
            acc = alpha * acc + jnp.dot(p.astype(jnp.bfloat16), vj, preferred_element_type=jnp.float32)
            out.append((m_new, l, acc))
        return tuple(out)

    init = tuple((jnp.full((tq, 1), NEG_INF, jnp.float32), jnp.zeros((tq, 1), jnp.float32),
                  jnp.zeros((tq, dv), jnp.float32)) for _ in q_variants)
    carry = lax.fori_loop(0, qi, lambda j, c: step(j, c, False), init)
    carry = step(qi, carry, True)
    return [(acc, l) for (_, l, acc) in carry]


def _diff_attn_kernel(q_ref, k_ref, v_ref, g_ref, lam_ref, subw_ref, o_ref, *, tq, lam_init):
    qi = pl.program_id(2)
    q = q_ref[...]
    lane = lax.broadcasted_iota(jnp.int32, q.shape, 1)
    zero = jnp.zeros_like(q)
    q1 = jnp.where(lane < 64, q, zero)
    q2 = jnp.where(lane >= 64, q, zero)
    (acc1, l1), (acc2, l2) = _flash([q1, q2], k_ref, v_ref, qi, tq)
    t = lam_ref[...]
    lam = (jnp.exp(jnp.sum(t[0:1] * t[1:2], axis=-1, keepdims=True))
           - jnp.exp(jnp.sum(t[2:3] * t[3:4], axis=-1, keepdims=True)) + lam_init)
    o = acc1 * (1.0 / l1) - lam * (acc2 * (1.0 / l2))
    ms = jnp.mean(o * o, axis=-1, keepdims=True)
    o = o * lax.rsqrt(ms + DA_SUBLN_EPS) * subw_ref[...] * (1.0 - lam_init)
    o_ref[...] = (o * g_ref[...].astype(jnp.float32)).astype(o_ref.dtype)


def _diff_attn(proj, lam_rows, subln_w, batch, seq, lam_init):
    tq = 256
    nq = seq // tq
    heads = 4
    return pl.pallas_call(
        functools.partial(_diff_attn_kernel, tq=tq, lam_init=lam_init),
        grid=(batch, heads, nq),
        in_specs=[
            pl.BlockSpec((tq, LANES), lambda b, h, i: (b * nq + i, SLAB_AQ + h)),
            pl.BlockSpec((seq, LANES), lambda b, h, i: (b, SLAB_AK + h)),
            pl.BlockSpec((seq, LANES), lambda b, h, i: (b, SLAB_AV + h)),
            pl.BlockSpec((tq, LANES), lambda b, h, i: (b * nq + i, SLAB_AG + h)),
            pl.BlockSpec((4, 64), lambda b, h, i: (0, 0)),
            pl.BlockSpec((1, LANES), lambda b, h, i: (0, 0)),
        ],
        out_specs=pl.BlockSpec((tq, LANES), lambda b, h, i: (b * nq + i, h)),
        out_shape=jax.ShapeDtypeStruct((batch * seq, heads * LANES), jnp.bfloat16),
        compiler_params=_cparams(("parallel", "parallel", "arbitrary")),
        name="diff_attn",
    )(proj, proj, proj, proj, lam_rows, subln_w.reshape(1, LANES))


def _mla_attn_kernel(q_ref, k_ref, v_ref, g_ref, o_ref, *, tq):
    qi = pl.program_id(2)
    ((acc, l),) = _flash([q_ref[...]], k_ref, v_ref, qi, tq)
    o = acc * (1.0 / l)
    o_ref[...] = (o * g_ref[...].astype(jnp.float32)).astype(o_ref.dtype)


def _mla_attn(qf, kf, vc, proj, batch, seq):
    tq = 256
    nq = seq // tq
    heads = 4
    return pl.pallas_call(
        functools.partial(_mla_attn_kernel, tq=tq),
        grid=(batch, heads, nq),
        in_specs=[
            pl.BlockSpec((tq, 2 * LANES), lambda b, h, i: (b * nq + i, h)),
            pl.BlockSpec((seq, 2 * LANES), lambda b, h, i: (b, h)),
            pl.BlockSpec((seq, LANES), lambda b, h, i: (b, h)),
            pl.BlockSpec((tq, LANES), lambda b, h, i: (b * nq + i, SLAB_CG + h)),
        ],
        out_specs=pl.BlockSpec((tq, LANES), lambda b, h, i: (b * nq + i, h)),
        out_shape=jax.ShapeDtypeStruct((batch * seq, heads * LANES), jnp.bfloat16),
        compiler_params=_cparams(("parallel", "parallel", "arbitrary")),
        name="mla_attn",
    )(qf, kf, vc, proj)


def _pad_w_uq(w):
    z = jnp.zeros((w.shape[0], 64), w.dtype)
    pieces = []
    for h in range(4):
        pieces += [w[:, h * 192:(h + 1) * 192], z]
    return jnp.concatenate(pieces, axis=1).astype(jnp.bfloat16)


def _reorder_w_ukv(w):
    k = [w[:, h * 256:h * 256 + 128] for h in range(4)]
    v = [w[:, h * 256 + 128:(h + 1) * 256] for h in range(4)]
    return jnp.concatenate(k + v, axis=1).astype(jnp.bfloat16)


def _mla_up_kernel(cq_ref, ckv_ref, ckr_ref, qnw_ref, kvnw_ref, wuq_ref, wukv_ref,
                   cos_ref, sa_ref, sb_ref, qf_ref, kf_ref, vc_ref):
    def norm(ref, w_ref):
        c = ref[...].astype(jnp.float32)
        ms = jnp.mean(c * c, axis=-1, keepdims=True)
        return (c * lax.rsqrt(ms + NORM_EPS) * w_ref[...]).astype(jnp.bfloat16)

    scale = 192 ** -0.5
    q = jnp.dot(norm(cq_ref, qnw_ref), wuq_ref[...], preferred_element_type=jnp.float32)
    kv = jnp.dot(norm(ckv_ref, kvnw_ref), wukv_ref[...], preferred_element_type=jnp.float32)
    kr = ckr_ref[...]
    for h in range(4):
        nope = q[:, 2 * h * LANES:(2 * h + 1) * LANES]
        rope = _rope(q[:, (2 * h + 1) * LANES:(2 * h + 2) * LANES], cos_ref[...], sa_ref[...], sb_ref[...])
        qf_ref[:, 2 * h * LANES:(2 * h + 1) * LANES] = (nope * scale).astype(qf_ref.dtype)
        qf_ref[:, (2 * h + 1) * LANES:(2 * h + 2) * LANES] = (rope * scale).astype(qf_ref.dtype)
        kf_ref[:, 2 * h * LANES:(2 * h + 1) * LANES] = kv[:, h * LANES:(h + 1) * LANES].astype(kf_ref.dtype)
        kf_ref[:, (2 * h + 1) * LANES:(2 * h + 2) * LANES] = kr
    vc_ref[...] = kv[:, 4 * LANES:].astype(vc_ref.dtype)


def _mla_up(proj, qn_w, kvn_w, w_uq_p, w_ukv_r, cos, sa, sb):
    m = proj.shape[0]
    tm = 512
    tab = pl.BlockSpec((tm, LANES), lambda i: (i, 0))
    full = lambda shape: pl.BlockSpec(shape, lambda i: (0, 0))
    return pl.pallas_call(
        _mla_up_kernel,
        grid=(m // tm,),
        in_specs=[
            pl.BlockSpec((tm, 512), lambda i: (i, SLAB_CQ // 4)),
            pl.BlockSpec((tm, 512), lambda i: (i, SLAB_CKV // 4)),
            pl.BlockSpec((tm, LANES), lambda i: (i, SLAB_CKR)),
            full((1, 512)), full((1, 512)), full((512, 1024)), full((512, 1024)),
            tab, tab, tab,
        ],
        out_specs=[
            pl.BlockSpec((tm, 1024), lambda i: (i, 0)),
            pl.BlockSpec((tm, 1024), lambda i: (i, 0)),
            pl.BlockSpec((tm, 512), lambda i: (i, 0)),
        ],
        out_shape=[
            jax.ShapeDtypeStruct((m, 1024), jnp.bfloat16),
            jax.ShapeDtypeStruct((m, 1024), jnp.bfloat16),
            jax.ShapeDtypeStruct((m, 512), jnp.bfloat16),
        ],
        compiler_params=_cparams(("parallel",)),
        name="mla_up",
    )(proj, proj, proj, qn_w.reshape(1, 512), kvn_w.reshape(1, 512), w_uq_p, w_ukv_r, cos, sa, sb)


def _swa_kernel(sink_ref, q_ref, kc_ref, kp_ref, vc_ref, vp_ref, g0_ref, g1_ref, o_ref, *, blocks):
    w = SW_WINDOW
    first = pl.program_id(1) == 0
    lane = lax.broadcasted_iota(jnp.int32, ((blocks + 1) * w, LANES), 1)
    lo = lane < 64

    def halves(prev_ref, cur_ref):
        t = jnp.concatenate([prev_ref[...], cur_ref[...]], axis=0).astype(jnp.float32)
        r = pltpu.roll(t, 64, 1)
        out = []
        for g in range(2):
            a, b = (t, r) if g == 0 else (r, t)
            out.append((jnp.where(lo, a, 0.0).astype(jnp.bfloat16), jnp.where(lo, 0.0, b).astype(jnp.bfloat16)))
        return out

    k_halves = halves(kp_ref, kc_ref)
    v_halves = halves(vp_ref, vc_ref)

    qrow = lax.broadcasted_iota(jnp.int32, (w, 2 * w), 0)
    kcol = lax.broadcasted_iota(jnp.int32, (w, 2 * w), 1)
    rel = qrow + w - kcol
    in_band = (rel >= 0) & (rel < w)
    first_key = jnp.where(first, w, 0)

    for n in range(blocks):
        mask = (in_band & (kcol >= first_key)) if n == 0 else in_band
        for g, g_ref in ((0, g0_ref), (1, g1_ref)):
            k_lo, k_hi = k_halves[g]
            v_lo, v_hi = v_halves[g]
            rows = slice(n * w, (n + 2) * w)
            kk = jnp.concatenate([k_lo[rows], k_hi[rows]], axis=0)
            vv = jnp.concatenate([v_lo[rows], v_hi[rows]], axis=0)
            qs = jnp.concatenate(
                [q_ref[n * w:(n + 1) * w, (4 * g + s) * LANES:(4 * g + s + 1) * LANES] for s in range(4)], axis=0)
            sc = lax.dot_general(qs, kk, _NT, preferred_element_type=jnp.float32)
            p_rows = []
            for s in range(4):
                p_par = []
                for e in range(2):
                    blk = sc[s * w:(s + 1) * w, e * 2 * w:(e + 1) * 2 * w]
                    blk = jnp.where(mask, blk, NEG_INF)
                    sink = sink_ref[8 * g + 2 * s + e]
                    mx = jnp.maximum(jnp.max(blk, axis=-1, keepdims=True), sink)
                    p = jnp.exp(blk - mx)
                    den = jnp.sum(p, axis=-1, keepdims=True) + jnp.exp(sink - mx)
                    p_par.append((p * (1.0 / den)).astype(jnp.bfloat16))
                p_rows.append(jnp.concatenate(p_par, axis=1))
            pm = jnp.concatenate(p_rows, axis=0)
            o = jnp.dot(pm, vv, preferred_element_type=jnp.float32)
            for s in range(4):
                gate = g_ref[n * w:(n + 1) * w, s * LANES:(s + 1) * LANES].astype(jnp.float32)
                o_ref[n * w:(n + 1) * w, (4 * g + s) * LANES:(4 * g + s + 1) * LANES] = (
                    o[s * w:(s + 1) * w] * gate).astype(o_ref.dtype)


def _swa(proj, sinks, batch, seq):
    w = SW_WINDOW
    blocks = 4
    rows = blocks * w
    steps = seq // rows
    nb = seq // w
    cur = lambda slab: pl.BlockSpec((rows, LANES), lambda b, i: (b * steps + i, slab))
    prev = lambda slab: pl.BlockSpec(
        (w, LANES), lambda b, i: (jnp.maximum(b * nb + i * blocks - 1, 0), slab))
    return pl.pallas_call(
        functools.partial(_swa_kernel, blocks=blocks),
        grid=(batch, steps),
        in_specs=[
            pl.BlockSpec(memory_space=pltpu.SMEM),
            pl.BlockSpec((rows, 1024), lambda b, i: (b * steps + i, SLAB_BQ // 8)),
            cur(SLAB_BK), prev(SLAB_BK), cur(SLAB_BV), prev(SLAB_BV),
            pl.BlockSpec((rows, 512), lambda b, i: (b * steps + i, SLAB_BG // 4)),
            pl.BlockSpec((rows, 512), lambda b, i: (b * steps + i, SLAB_BG // 4 + 1)),
        ],
        out_specs=pl.BlockSpec((rows, 1024), lambda b, i: (b * steps + i, 0)),
        out_shape=jax.ShapeDtypeStruct((batch * seq, 1024), jnp.bfloat16),
        compiler_params=_cparams(("parallel", "arbitrary")),
        name="swa",
    )(sinks.astype(jnp.float32), proj, proj, proj, proj, proj, proj, proj)


def _out_proj_kernel(ya_ref, yb_ref, yc_ref, w_ref, x_ref, postw_ref, o_ref):
    out = jnp.dot(ya_ref[...], w_ref[0:512, :], preferred_element_type=jnp.float32)
    out += jnp.dot(yb_ref[...], w_ref[512:1536, :], preferred_element_type=jnp.float32)
    out += jnp.dot(yc_ref[...], w_ref[1536:2048, :], preferred_element_type=jnp.float32)
    ms = jnp.mean(out * out, axis=-1, keepdims=True)
    o_ref[...] = x_ref[...] + out * lax.rsqrt(ms + NORM_EPS) * postw_ref[...]


def _out_proj(ya, yb, yc, w_out, x2, post_w):
    m = x2.shape[0]
    tm = 512
    row = lambda width: pl.BlockSpec((tm, width), lambda i: (i, 0))
    return pl.pallas_call(
        _out_proj_kernel,
        grid=(m // tm,),
        in_specs=[
            row(512), row(1024), row(512),
            pl.BlockSpec((D_MODEL, D_MODEL), lambda i: (0, 0)),
            row(D_MODEL),
            pl.BlockSpec((1, D_MODEL), lambda i: (0, 0)),
        ],
        out_specs=row(D_MODEL),
        out_shape=jax.ShapeDtypeStruct((m, D_MODEL), jnp.float32),
        compiler_params=_cparams(("parallel",)),
        name="out_proj",
    )(ya, yb, yc, w_out, x2, post_w.reshape(1, D_MODEL))


def kernel(x, positions, pre_norm_w, post_norm_w, w_in, diff_lambda_q1, diff_lambda_k1, diff_lambda_q2,
           diff_lambda_k2, diff_subln_w, sink_logits, mla_q_norm_w, mla_kv_norm_w, w_uq, w_ukv, w_out):
    batch, seq, d = x.shape
    depth = w_in.shape[0]
    cos, sa, sb = _rope_tables(positions)
    x2 = x.reshape(batch * seq, d)
    for layer in range(depth):
        lam_init = 0.8 - 0.6 * math.exp(-0.3 * layer)
        lam_rows = jnp.stack([diff_lambda_q1[layer], diff_lambda_k1[layer],
                              diff_lambda_q2[layer], diff_lambda_k2[layer]]).astype(jnp.float32)
        proj = _in_proj(x2, pre_norm_w[layer], _reorder_w_in(w_in[layer]), cos, sa, sb)
        ya = _diff_attn(proj, lam_rows, diff_subln_w[layer], batch, seq, lam_init)
        yb = _swa(proj, sink_logits[layer], batch, seq)
        qf, kf, vc = _mla_up(proj, mla_q_norm_w[layer], mla_kv_norm_w[layer],
                             _pad_w_uq(w_uq[layer]), _reorder_w_ukv(w_ukv[layer]), cos, sa, sb)
        yc = _mla_attn(qf, kf, vc, proj, batch, seq)
        x2 = _out_proj(ya, yb, yc, w_out[layer].astype(jnp.bfloat16), x2, post_norm_w[layer])
    return x2.reshape(batch, seq, d)
```

```python
import functools
import math

import jax
import jax.numpy as jnp
import numpy as np
from jax import lax
from jax.experimental import pallas as pl
from jax.experimental.pallas import tpu as pltpu

D_MODEL = 2048
ROPE_THETA = 10000.0
NORM_EPS = 1e-6
NEG_INF = -1e30
DA_SUBLN_EPS = 1e-5
SW_WINDOW = 128
LANES = 128

IN_TILE = 1024
N_SLABS = 48
SLAB_AQ, SLAB_AK, SLAB_BQ = 0, 4, 8
SLAB_BK, SLAB_CKR, SLAB_BV, SLAB_AV = 16, 17, 18, 20
SLAB_CQ, SLAB_CKV = 24, 28
SLAB_AG, SLAB_BG, SLAB_CG = 32, 36, 44

VMEM_LIMIT = 48 * 1024 * 1024

_NT = (((1,), (1,)), ((), ()))


def _cparams(sem):
    return pltpu.CompilerParams(dimension_semantics=sem, vmem_limit_bytes=VMEM_LIMIT)


def _rope_table_kernel(pos_ref, freq_ref, cos_ref, sa_ref, sb_ref):
    ang = pos_ref[...].astype(jnp.float32) * freq_ref[...]
    c, s = jnp.cos(ang), jnp.sin(ang)
    lane = lax.broadcasted_iota(jnp.int32, ang.shape, 1)
    first_half = (lane % 64) < 32
    cos_ref[...] = c
    sa_ref[...] = jnp.where(first_half, -s, 0.0)
    sb_ref[...] = jnp.where(first_half, 0.0, s)


def _rope_tables(positions):
    m = positions.size
    pos_b = jnp.broadcast_to(positions.reshape(m, 1), (m, LANES))
    inv_freq = jnp.power(ROPE_THETA, -jnp.arange(0, 64, 2, dtype=jnp.float32) / 64)
    freq = jnp.tile(inv_freq, 4).reshape(1, LANES)
    tm = 1024
    spec = pl.BlockSpec((tm, LANES), lambda i: (i, 0))
    return pl.pallas_call(
        _rope_table_kernel,
        grid=(m // tm,),
        in_specs=[spec, pl.BlockSpec((1, LANES), lambda i: (0, 0))],
        out_specs=[spec, spec, spec],
        out_shape=[jax.ShapeDtypeStruct((m, LANES), jnp.float32)] * 3,
        compiler_params=_cparams(("parallel",)),
        name="rope_tables",
    )(pos_b, freq)


def _rope(a, cos, sa, sb):
    return a * cos + pltpu.roll(a, 96, 1) * sa + pltpu.roll(a, 32, 1) * sb


def _reorder_w_in(w):
    z = jnp.zeros((w.shape[0], 64), w.dtype)
    z2 = jnp.zeros((w.shape[0], 128), w.dtype)
    pieces = [
        w[:, 0:512], w[:, 512:1024],
        w[:, 2048:3072],
        w[:, 3072:3200], w[:, 5376:5440], z,
        w[:, 3200:3328], z2, w[:, 1024:1536],
        w[:, 4352:4864], w[:, 4864:5376],
        w[:, 1536:2048], w[:, 3328:4352], w[:, 5440:5952],
    ]
    return jnp.concatenate(pieces, axis=1).astype(jnp.bfloat16)


def _in_proj_kernel(x_ref, prew_ref, w_ref, cos_ref, sa_ref, sb_ref, o_ref, h_ref):
    j = pl.program_id(1)

    @pl.when(j == 0)
    def _():
        x = x_ref[...]
        ms = jnp.mean(x * x, axis=-1, keepdims=True)
        h_ref[...] = (x * lax.rsqrt(ms + NORM_EPS) * prew_ref[...]).astype(jnp.bfloat16)

    def epilogue(kinds):
        acc = jnp.dot(h_ref[...], w_ref[...], preferred_element_type=jnp.float32)
        for s, kind in enumerate(kinds):
            a = acc[:, s * LANES:(s + 1) * LANES]
            if kind in ("rope", "rope_q"):
                a = _rope(a, cos_ref[...], sa_ref[...], sb_ref[...])
                if kind == "rope_q":
                    a = a * 0.125
            elif kind == "silu":
                a = a * (1.0 / (1.0 + jnp.exp(-a)))
            o_ref[:, s * LANES:(s + 1) * LANES] = a.astype(o_ref.dtype)

    pl.when(j == 0)(lambda: epilogue(["rope_q"] * 4 + ["rope"] * 4))
    pl.when(j == 1)(lambda: epilogue(["rope_q"] * 8))
    pl.when(j == 2)(lambda: epilogue(["rope"] * 2 + ["plain"] * 6))
    pl.when(j == 3)(lambda: epilogue(["plain"] * 8))
    pl.when(j >= 4)(lambda: epilogue(["silu"] * 8))


def _in_proj(x2, pre_w, w_in_r, cos, sa, sb):
    m = x2.shape[0]
    tm, tn = 1024, IN_TILE
    tab = pl.BlockSpec((tm, LANES), lambda i, j: (i, 0))
    return pl.pallas_call(
        _in_proj_kernel,
        grid=(m // tm, N_SLABS * LANES // tn),
        in_specs=[
            pl.BlockSpec((tm, D_MODEL), lambda i, j: (i, 0)),
            pl.BlockSpec((1, D_MODEL), lambda i, j: (0, 0)),
            pl.BlockSpec((D_MODEL, tn), lambda i, j: (0, j)),
            tab, tab, tab,
        ],
        out_specs=pl.BlockSpec((tm, tn), lambda i, j: (i, j)),
        out_shape=jax.ShapeDtypeStruct((m, N_SLABS * LANES), jnp.bfloat16),
        scratch_shapes=[pltpu.VMEM((tm, D_MODEL), jnp.bfloat16)],
        compiler_params=_cparams(("parallel", "arbitrary")),
        name="in_proj",
    )(x2, pre_w.reshape(1, D_MODEL), w_in_r, cos, sa, sb)


ATTN_TQ = 256


def _transpose_v(v_ref, vt_ref):
    vt_ref[...] = v_ref[...].astype(jnp.float32).T.astype(vt_ref.dtype)


def _causal_rows(q, k_ref, vt_ref, i, tq):
    n_keys = (i + 1) * tq
    s = lax.dot_general(k_ref[0:n_keys, :], q, _NT, preferred_element_type=jnp.float32)
    key = lax.broadcasted_iota(jnp.int32, (tq, tq), 0)
    qry = lax.broadcasted_iota(jnp.int32, (tq, tq), 1)
    tail = jnp.where(key <= qry, s[n_keys - tq:], NEG_INF)
    s = tail if i == 0 else jnp.concatenate([s[:n_keys - tq], tail], axis=0)
    m = jnp.max(s, axis=0, keepdims=True)
    p = jnp.exp(s - m)
    l = jnp.sum(p, axis=0, keepdims=True)
    o_t = jnp.dot(vt_ref[:, 0:n_keys], p.astype(jnp.bfloat16), preferred_element_type=jnp.float32)
    return o_t, l


def _diff_attn_kernel(q_ref, k_ref, v_ref, g_ref, lam_ref, subw_ref, o_ref, vt_ref, *, tq, lam_init):
    _transpose_v(v_ref, vt_ref)
    t = lam_ref[...]
    lam = (jnp.exp(jnp.sum(t[0:1] * t[1:2], axis=-1, keepdims=True))
           - jnp.exp(jnp.sum(t[2:3] * t[3:4], axis=-1, keepdims=True)) + lam_init)
    lane = lax.broadcasted_iota(jnp.int32, (tq, LANES), 1)
    for i in range(q_ref.shape[0] // tq):
        rows = slice(i * tq, (i + 1) * tq)
        q = q_ref[rows, :]
        zero = jnp.zeros_like(q)
        q1 = jnp.where(lane < 64, q, zero)
        q2 = jnp.where(lane >= 64, q, zero)
        o1, l1 = _causal_rows(q1, k_ref, vt_ref, i, tq)
        o2, l2 = _causal_rows(q2, k_ref, vt_ref, i, tq)
        o = o1 * (1.0 / l1) - lam * (o2 * (1.0 / l2))
        ms = jnp.mean(o * o, axis=0, keepdims=True)
        o = (o * lax.rsqrt(ms + DA_SUBLN_EPS)).T
        o = o * subw_ref[...] * (1.0 - lam_init)
        o_ref[rows, :] = (o * g_ref[rows, :].astype(jnp.float32)).astype(o_ref.dtype)


def _diff_attn(proj, lam_rows, subln_w, batch, seq, lam_init):
    heads = 4
    slab = lambda first: pl.BlockSpec((seq, LANES), lambda b, h: (b, first + h))
    return pl.pallas_call(
        functools.partial(_diff_attn_kernel, tq=ATTN_TQ, lam_init=lam_init),
        grid=(batch, heads),
        in_specs=[
            slab(SLAB_AQ), slab(SLAB_AK), slab(SLAB_AV), slab(SLAB_AG),
            pl.BlockSpec((4, 64), lambda b, h: (0, 0)),
            pl.BlockSpec((1, LANES), lambda b, h: (0, 0)),
        ],
        out_specs=pl.BlockSpec((seq, LANES), lambda b, h: (b, h)),
        out_shape=jax.ShapeDtypeStruct((batch * seq, heads * LANES), jnp.bfloat16),
        scratch_shapes=[pltpu.VMEM((LANES, seq), jnp.bfloat16)],
        compiler_params=_cparams(("parallel", "parallel")),
        name="diff_attn",
    )(proj, proj, proj, proj, lam_rows, subln_w.reshape(1, LANES))


def _mla_attn_kernel(q_ref, k_ref, v_ref, g_ref, o_ref, vt_ref, *, tq):
    _transpose_v(v_ref, vt_ref)
    for i in range(q_ref.shape[0] // tq):
        rows = slice(i * tq, (i + 1) * tq)
        o_t, l = _causal_rows(q_ref[rows, :], k_ref, vt_ref, i, tq)
        o = (o_t * (1.0 / l)).T
        o_ref[rows, :] = (o * g_ref[rows, :].astype(jnp.float32)).astype(o_ref.dtype)


def _mla_attn(qf, kf, vc, proj, batch, seq):
    heads = 4
    return pl.pallas_call(
        functools.partial(_mla_attn_kernel, tq=ATTN_TQ),
        grid=(batch, heads),
        in_specs=[
            pl.BlockSpec((seq, 2 * LANES), lambda b, h: (b, h)),
            pl.BlockSpec((seq, 2 * LANES), lambda b, h: (b, h)),
            pl.BlockSpec((seq, LANES), lambda b, h: (b, h)),
            pl.BlockSpec((seq, LANES), lambda b, h: (b, SLAB_CG + h)),
        ],
        out_specs=pl.BlockSpec((seq, LANES), lambda b, h: (b, h)),
        out_shape=jax.ShapeDtypeStruct((batch * seq, heads * LANES), jnp.bfloat16),
        scratch_shapes=[pltpu.VMEM((LANES, seq), jnp.bfloat16)],
        compiler_params=_cparams(("parallel", "parallel")),
        name="mla_attn",
    )(qf, kf, vc, proj)


def _pad_w_uq(w):
    z = jnp.zeros((w.shape[0], 64), w.dtype)
    pieces = []
    for h in range(4):
        pieces += [w[:, h * 192:(h + 1) * 192], z]
    return jnp.concatenate(pieces, axis=1).astype(jnp.bfloat16)


def _reorder_w_ukv(w):
    k = [w[:, h * 256:h * 256 + 128] for h in range(4)]
    v = [w[:, h * 256 + 128:(h + 1) * 256] for h in range(4)]
    return jnp.concatenate(k + v, axis=1).astype(jnp.bfloat16)


def _mla_up_kernel(cq_ref, ckv_ref, ckr_ref, qnw_ref, kvnw_ref, wuq_ref, wukv_ref,
                   cos_ref, sa_ref, sb_ref, qf_ref, kf_ref, vc_ref):
    def norm(ref, w_ref):
        c = ref[...].astype(jnp.float32)
        ms = jnp.mean(c * c, axis=-1, keepdims=True)
        return (c * lax.rsqrt(ms + NORM_EPS) * w_ref[...]).astype(jnp.bfloat16)

    scale = 192 ** -0.5
    q = jnp.dot(norm(cq_ref, qnw_ref), wuq_ref[...], preferred_element_type=jnp.float32)
    kv = jnp.dot(norm(ckv_ref, kvnw_ref), wukv_ref[...], preferred_element_type=jnp.float32)
    kr = ckr_ref[...]
    for h in range(4):
        nope = q[:, 2 * h * LANES:(2 * h + 1) * LANES]
        rope = _rope(q[:, (2 * h + 1) * LANES:(2 * h + 2) * LANES], cos_ref[...], sa_ref[...], sb_ref[...])
        qf_ref[:, 2 * h * LANES:(2 * h + 1) * LANES] = (nope * scale).astype(qf_ref.dtype)
        qf_ref[:, (2 * h + 1) * LANES:(2 * h + 2) * LANES] = (rope * scale).astype(qf_ref.dtype)
        kf_ref[:, 2 * h * LANES:(2 * h + 1) * LANES] = kv[:, h * LANES:(h + 1) * LANES].astype(kf_ref.dtype)
        kf_ref[:, (2 * h + 1) * LANES:(2 * h + 2) * LANES] = kr
    vc_ref[...] = kv[:, 4 * LANES:].astype(vc_ref.dtype)


def _mla_up(proj, qn_w, kvn_w, w_uq_p, w_ukv_r, cos, sa, sb):
    m = proj.shape[0]
    tm = 512
    tab = pl.BlockSpec((tm, LANES), lambda i: (i, 0))
    full = lambda shape: pl.BlockSpec(shape, lambda i: (0, 0))
    return pl.pallas_call(
        _mla_up_kernel,
        grid=(m // tm,),
        in_specs=[
            pl.BlockSpec((tm, 512), lambda i: (i, SLAB_CQ // 4)),
            pl.BlockSpec((tm, 512), lambda i: (i, SLAB_CKV // 4)),
            pl.BlockSpec((tm, LANES), lambda i: (i, SLAB_CKR)),
            full((1, 512)), full((1, 512)), full((512, 1024)), full((512, 1024)),
            tab, tab, tab,
        ],
        out_specs=[
            pl.BlockSpec((tm, 1024), lambda i: (i, 0)),
            pl.BlockSpec((tm, 1024), lambda i: (i, 0)),
            pl.BlockSpec((tm, 512), lambda i: (i, 0)),
        ],
        out_shape=[
            jax.ShapeDtypeStruct((m, 1024), jnp.bfloat16),
            jax.ShapeDtypeStruct((m, 1024), jnp.bfloat16),
            jax.ShapeDtypeStruct((m, 512), jnp.bfloat16),
        ],
        compiler_params=_cparams(("parallel",)),
        name="mla_up",
    )(proj, proj, proj, qn_w.reshape(1, 512), kvn_w.reshape(1, 512), w_uq_p, w_ukv_r, cos, sa, sb)


def _swa_kernel(sink_ref, q_ref, kc_ref, kp_ref, vc_ref, vp_ref, g0_ref, g1_ref, o_ref, *, blocks):
    w = SW_WINDOW
    first = pl.program_id(1) == 0
    lane = lax.broadcasted_iota(jnp.int32, ((blocks + 1) * w, LANES), 1)
    lo = lane < 64

    def halves(prev_ref, cur_ref):
        t = jnp.concatenate([prev_ref[...], cur_ref[...]], axis=0).astype(jnp.float32)
        r = pltpu.roll(t, 64, 1)
        out = []
        for g in range(2):
            a, b = (t, r) if g == 0 else (r, t)
            out.append((jnp.where(lo, a, 0.0).astype(jnp.bfloat16), jnp.where(lo, 0.0, b).astype(jnp.bfloat16)))
        return out

    k_halves = halves(kp_ref, kc_ref)
    v_halves = halves(vp_ref, vc_ref)

    qrow = lax.broadcasted_iota(jnp.int32, (w, 2 * w), 0)
    kcol = lax.broadcasted_iota(jnp.int32, (w, 2 * w), 1)
    rel = qrow + w - kcol
    in_band = (rel >= 0) & (rel < w)
    first_key = jnp.where(first, w, 0)

    for n in range(blocks):
        mask = (in_band & (kcol >= first_key)) if n == 0 else in_band
        for g, g_ref in ((0, g0_ref), (1, g1_ref)):
            k_lo, k_hi = k_halves[g]
            v_lo, v_hi = v_halves[g]
            rows = slice(n * w, (n + 2) * w)
            kk = jnp.concatenate([k_lo[rows], k_hi[rows]], axis=0)
            vv = jnp.concatenate([v_lo[rows], v_hi[rows]], axis=0)
            qs = jnp.concatenate(
                [q_ref[n * w:(n + 1) * w, (4 * g + s) * LANES:(4 * g + s + 1) * LANES] for s in range(4)], axis=0)
            sc = lax.dot_general(qs, kk, _NT, preferred_element_type=jnp.float32)
            p_rows = []
            for s in range(4):
                p_par = []
                for e in range(2):
                    blk = sc[s * w:(s + 1) * w, e * 2 * w:(e + 1) * 2 * w]
                    blk = jnp.where(mask, blk, NEG_INF)
                    sink = sink_ref[8 * g + 2 * s + e]
                    mx = jnp.maximum(jnp.max(blk, axis=-1, keepdims=True), sink)
                    p = jnp.exp(blk - mx)
                    den = jnp.sum(p, axis=-1, keepdims=True) + jnp.exp(sink - mx)
                    p_par.append((p * (1.0 / den)).astype(jnp.bfloat16))
                p_rows.append(jnp.concatenate(p_par, axis=1))
            pm = jnp.concatenate(p_rows, axis=0)
            o = jnp.dot(pm, vv, preferred_element_type=jnp.float32)
            for s in range(4):
                gate = g_ref[n * w:(n + 1) * w, s * LANES:(s + 1) * LANES].astype(jnp.float32)
                o_ref[n * w:(n + 1) * w, (4 * g + s) * LANES:(4 * g + s + 1) * LANES] = (
                    o[s * w:(s + 1) * w] * gate).astype(o_ref.dtype)


def _swa(proj, sinks, batch, seq):
    w = SW_WINDOW
    blocks = 4
    rows = blocks * w
    steps = seq // rows
    nb = seq // w
    cur = lambda slab: pl.BlockSpec((rows, LANES), lambda b, i: (b * steps + i, slab))
    prev = lambda slab: pl.BlockSpec(
        (w, LANES), lambda b, i: (jnp.maximum(b * nb + i * blocks - 1, 0), slab))
    return pl.pallas_call(
        functools.partial(_swa_kernel, blocks=blocks),
        grid=(batch, steps),
        in_specs=[
            pl.BlockSpec(memory_space=pltpu.SMEM),
            pl.BlockSpec((rows, 1024), lambda b, i: (b * steps + i, SLAB_BQ // 8)),
            cur(SLAB_BK), prev(SLAB_BK), cur(SLAB_BV), prev(SLAB_BV),
            pl.BlockSpec((rows, 512), lambda b, i: (b * steps + i, SLAB_BG // 4)),
            pl.BlockSpec((rows, 512), lambda b, i: (b * steps + i, SLAB_BG // 4 + 1)),
        ],
        out_specs=pl.BlockSpec((rows, 1024), lambda b, i: (b * steps + i, 0)),
        out_shape=jax.ShapeDtypeStruct((batch * seq, 1024), jnp.bfloat16),
        compiler_params=_cparams(("parallel", "arbitrary")),
        name="swa",
    )(sinks.astype(jnp.float32), proj, proj, proj, proj, proj, proj, proj)


def _out_proj_kernel(ya_ref, yb_ref, yc_ref, w_ref, x_ref, postw_ref, o_ref):
    out = jnp.dot(ya_ref[...], w_ref[0:512, :], preferred_element_type=jnp.float32)
    out += jnp.dot(yb_ref[...], w_ref[512:1536, :], preferred_element_type=jnp.float32)
    out += jnp.dot(yc_ref[...], w_ref[1536:2048, :], preferred_element_type=jnp.float32)
    ms = jnp.mean(out * out, axis=-1, keepdims=True)
    o_ref[...] = x_ref[...] + out * lax.rsqrt(ms + NORM_EPS) * postw_ref[...]


def _out_proj(ya, yb, yc, w_out, x2, post_w):
    m = x2.shape[0]
    tm = 512
    row = lambda width: pl.BlockSpec((tm, width), lambda i: (i, 0))
    return pl.pallas_call(
        _out_proj_kernel,
        grid=(m // tm,),
        in_specs=[
            row(512), row(1024), row(512),
            pl.BlockSpec((D_MODEL, D_MODEL), lambda i: (0, 0)),
            row(D_MODEL),
            pl.BlockSpec((1, D_MODEL), lambda i: (0, 0)),
        ],
        out_specs=row(D_MODEL),
        out_shape=jax.ShapeDtypeStruct((m, D_MODEL), jnp.float32),
        compiler_params=_cparams(("parallel",)),
        name="out_proj",
    )(ya, yb, yc, w_out, x2, post_w.reshape(1, D_MODEL))


def kernel(x, positions, pre_norm_w, post_norm_w, w_in, diff_lambda_q1, diff_lambda_k1, diff_lambda_q2,
           diff_lambda_k2, diff_subln_w, sink_logits, mla_q_norm_w, mla_kv_norm_w, w_uq, w_ukv, w_out):
    batch, seq, d = x.shape
    depth = w_in.shape[0]
    cos, sa, sb = _rope_tables(positions)
    x2 = x.reshape(batch * seq, d)
    for layer in range(depth):
        lam_init = 0.8 - 0.6 * math.exp(-0.3 * layer)
        lam_rows = jnp.stack([diff_lambda_q1[layer], diff_lambda_k1[layer],
                              diff_lambda_q2[layer], diff_lambda_k2[layer]]).astype(jnp.float32)
        proj = _in_proj(x2, pre_norm_w[layer], _reorder_w_in(w_in[layer]), cos, sa, sb)
        ya = _diff_attn(proj, lam_rows, diff_subln_w[layer], batch, seq, lam_init)
        yb = _swa(proj, sink_logits[layer], batch, seq)
        qf, kf, vc = _mla_up(proj, mla_q_norm_w[layer], mla_kv_norm_w[layer],
                             _pad_w_uq(w_uq[layer]), _reorder_w_ukv(w_ukv[layer]), cos, sa, sb)
        yc = _mla_attn(qf, kf, vc, proj, batch, seq)
        x2 = _out_proj(ya, yb, yc, w_out[layer].astype(jnp.bfloat16), x2, post_norm_w[layer])
    return x2.reshape(batch, seq, d)
```

```python
import functools
import math

import jax
import jax.numpy as jnp
import numpy as np
from jax import lax
from jax.experimental import pallas as pl
from jax.experimental.pallas import tpu as pltpu

D_MODEL = 2048
ROPE_THETA = 10000.0
NORM_EPS = 1e-6
NEG_INF = -1e30
DA_SUBLN_EPS = 1e-5
SW_WINDOW = 128
LANES = 128

IN_TILE = 1024
D_IN = 5952
N_SLABS = 48
SLAB_AQ, SLAB_AK, SLAB_AV, SLAB_AG = 0, 4, 8, 12
SLAB_BQ, SLAB_BK, SLAB_BV, SLAB_BG = 16, 24, 25, 26
SLAB_CQ, SLAB_CKV, SLAB_CKR, SLAB_CG = 34, 38, 42, 43

VMEM_LIMIT = 48 * 1024 * 1024

_NT = (((1,), (1,)), ((), ()))


def _cparams(sem):
    return pltpu.CompilerParams(dimension_semantics=sem, vmem_limit_bytes=VMEM_LIMIT)


def _rope_table_kernel(pos_ref, freq_ref, cos_ref, sa_ref, sb_ref):
    ang = pos_ref[...].astype(jnp.float32) * freq_ref[...]
    c, s = jnp.cos(ang), jnp.sin(ang)
    lane = lax.broadcasted_iota(jnp.int32, ang.shape, 1)
    first_half = (lane % 64) < 32
    cos_ref[...] = c
    sa_ref[...] = jnp.where(first_half, -s, 0.0)
    sb_ref[...] = jnp.where(first_half, 0.0, s)


def _rope_tables(positions):
    m = positions.size
    pos_b = jnp.broadcast_to(positions.reshape(m, 1), (m, LANES))
    inv_freq = jnp.power(ROPE_THETA, -jnp.arange(0, 64, 2, dtype=jnp.float32) / 64)
    freq = jnp.tile(inv_freq, 4).reshape(1, LANES)
    tm = 1024
    spec = pl.BlockSpec((tm, LANES), lambda i: (i, 0))
    return pl.pallas_call(
        _rope_table_kernel,
        grid=(m // tm,),
        in_specs=[spec, pl.BlockSpec((1, LANES), lambda i: (0, 0))],
        out_specs=[spec, spec, spec],
        out_shape=[jax.ShapeDtypeStruct((m, LANES), jnp.float32)] * 3,
        compiler_params=_cparams(("parallel",)),
        name="rope_tables",
    )(pos_b, freq)


def _rope(a, cos, sa, sb):
    return a * cos + pltpu.roll(a, 96, 1) * sa + pltpu.roll(a, 32, 1) * sb


def _prep_w_in(w_in):
    return jnp.pad(w_in.astype(jnp.bfloat16), ((0, 0), (0, 0), (0, N_SLABS * LANES - D_IN)))


def _silu(a):
    return a * (1.0 / (1.0 + jnp.exp(-a)))


_IN_KINDS = (
    ("rope_q",) * 4 + ("rope",) * 4,
    ("plain",) * 4 + ("silu",) * 4,
    ("rope_q",) * 8,
    ("rope", "plain") + ("silu",) * 6,
    ("silu",) * 2 + ("plain",) * 6,
    ("plain",) * 2 + ("kr",) + ("gate_c",) * 4 + ("skip",),
)


def _in_proj_kernel(x_ref, prew_ref, w_ref, cos_ref, sa_ref, sb_ref, o_ref, h_ref):
    j = pl.program_id(1)

    @pl.when(j == 0)
    def _():
        x = x_ref[...]
        ms = jnp.mean(x * x, axis=-1, keepdims=True)
        h_ref[...] = (x * lax.rsqrt(ms + NORM_EPS) * prew_ref[...]).astype(jnp.bfloat16)

    def epilogue(kinds):
        acc = jnp.dot(h_ref[...], w_ref[...], preferred_element_type=jnp.float32)
        slab = lambda s: acc[:, s * LANES:(s + 1) * LANES]
        lane = lax.broadcasted_iota(jnp.int32, (acc.shape[0], LANES), 1)
        for s, kind in enumerate(kinds):
            a = slab(s)
            if kind in ("rope", "rope_q", "kr"):
                a = _rope(a, cos_ref[...], sa_ref[...], sb_ref[...])
                if kind == "rope_q":
                    a = a * 0.125
            elif kind == "silu":
                a = _silu(a)
            elif kind == "gate_c":
                a = jnp.where(lane < 64, pltpu.roll(_silu(slab(s - 1)), 64, 1), pltpu.roll(_silu(a), 64, 1))
            o_ref[:, s * LANES:(s + 1) * LANES] = a.astype(o_ref.dtype)

    for t, kinds in enumerate(_IN_KINDS):
        pl.when(j == t)(functools.partial(epilogue, kinds))


def _in_proj(x2, pre_w, w_in_b, layer, cos, sa, sb):
    m = x2.shape[0]
    tm, tn = 1024, IN_TILE
    tab = pl.BlockSpec((tm, LANES), lambda i, j: (i, 0))
    return pl.pallas_call(
        _in_proj_kernel,
        grid=(m // tm, N_SLABS * LANES // tn),
        in_specs=[
            pl.BlockSpec((tm, D_MODEL), lambda i, j: (i, 0)),
            pl.BlockSpec((1, D_MODEL), lambda i, j: (0, 0)),
            pl.BlockSpec((None, D_MODEL, tn), lambda i, j: (layer, 0, j)),
            tab, tab, tab,
        ],
        out_specs=pl.BlockSpec((tm, tn), lambda i, j: (i, j)),
        out_shape=jax.ShapeDtypeStruct((m, N_SLABS * LANES), jnp.bfloat16),
        scratch_shapes=[pltpu.VMEM((tm, D_MODEL), jnp.bfloat16)],
        compiler_params=_cparams(("parallel", "arbitrary")),
        name="in_proj",
    )(x2, pre_w.reshape(1, D_MODEL), w_in_b, cos, sa, sb)


ATTN_TQ = 256


def _transpose_v(v_ref, vt_ref):
    vt_ref[...] = v_ref[...].astype(jnp.float32).T.astype(vt_ref.dtype)


def _causal_rows(q, k_ref, vt_ref, i, tq):
    n_keys = (i + 1) * tq
    s = lax.dot_general(k_ref[0:n_keys, :], q, _NT, preferred_element_type=jnp.float32)
    key = lax.broadcasted_iota(jnp.int32, (tq, tq), 0)
    qry = lax.broadcasted_iota(jnp.int32, (tq, tq), 1)
    tail = jnp.where(key <= qry, s[n_keys - tq:], NEG_INF)
    s = tail if i == 0 else jnp.concatenate([s[:n_keys - tq], tail], axis=0)
    m = jnp.max(s, axis=0, keepdims=True)
    p = jnp.exp(s - m)
    l = jnp.sum(p, axis=0, keepdims=True)
    o_t = jnp.dot(vt_ref[:, 0:n_keys], p.astype(jnp.bfloat16), preferred_element_type=jnp.float32)
    return o_t, l


def _diff_attn_kernel(q_ref, k_ref, v_ref, g_ref, lam_ref, subw_ref, o_ref, vt_ref, *, tq, lam_init):
    _transpose_v(v_ref, vt_ref)
    t = lam_ref[...]
    lam = (jnp.exp(jnp.sum(t[0:1] * t[1:2], axis=-1, keepdims=True))
           - jnp.exp(jnp.sum(t[2:3] * t[3:4], axis=-1, keepdims=True)) + lam_init)
    lane = lax.broadcasted_iota(jnp.int32, (tq, LANES), 1)
    for i in range(q_ref.shape[0] // tq):
        rows = slice(i * tq, (i + 1) * tq)
        q = q_ref[rows, :]
        zero = jnp.zeros_like(q)
        q1 = jnp.where(lane < 64, q, zero)
        q2 = jnp.where(lane >= 64, q, zero)
        o1, l1 = _causal_rows(q1, k_ref, vt_ref, i, tq)
        o2, l2 = _causal_rows(q2, k_ref, vt_ref, i, tq)
        o = o1 * (1.0 / l1) - lam * (o2 * (1.0 / l2))
        ms = jnp.mean(o * o, axis=0, keepdims=True)
        o = (o * lax.rsqrt(ms + DA_SUBLN_EPS)).T
        o = o * subw_ref[...] * (1.0 - lam_init)
        o_ref[rows, :] = (o * g_ref[rows, :].astype(jnp.float32)).astype(o_ref.dtype)


def _diff_attn(proj, lam_rows, subln_w, batch, seq, lam_init):
    heads = 4
    slab = lambda first: pl.BlockSpec((seq, LANES), lambda b, h: (b, first + h))
    return pl.pallas_call(
        functools.partial(_diff_attn_kernel, tq=ATTN_TQ, lam_init=lam_init),
        grid=(batch, heads),
        in_specs=[
            slab(SLAB_AQ), slab(SLAB_AK), slab(SLAB_AV), slab(SLAB_AG),
            pl.BlockSpec((4, 64), lambda b, h: (0, 0)),
            pl.BlockSpec((1, LANES), lambda b, h: (0, 0)),
        ],
        out_specs=pl.BlockSpec((seq, LANES), lambda b, h: (b, h)),
        out_shape=jax.ShapeDtypeStruct((batch * seq, heads * LANES), jnp.bfloat16),
        scratch_shapes=[pltpu.VMEM((LANES, seq), jnp.bfloat16)],
        compiler_params=_cparams(("parallel", "parallel")),
        name="diff_attn",
    )(proj, proj, proj, proj, lam_rows, subln_w.reshape(1, LANES))


def _mla_attn_kernel(q_ref, k_ref, v_ref, g_ref, o_ref, vt_ref, *, tq):
    _transpose_v(v_ref, vt_ref)
    for i in range(q_ref.shape[0] // tq):
        rows = slice(i * tq, (i + 1) * tq)
        o_t, l = _causal_rows(q_ref[rows, :], k_ref, vt_ref, i, tq)
        o = (o_t * (1.0 / l)).T
        o_ref[rows, :] = (o * g_ref[rows, :].astype(jnp.float32)).astype(o_ref.dtype)


def _mla_attn(qf, kf, vc, proj, batch, seq):
    heads = 4
    return pl.pallas_call(
        functools.partial(_mla_attn_kernel, tq=ATTN_TQ),
        grid=(batch, heads),
        in_specs=[
            pl.BlockSpec((seq, 2 * LANES), lambda b, h: (b, h)),
            pl.BlockSpec((seq, 2 * LANES), lambda b, h: (b, h)),
            pl.BlockSpec((seq, LANES), lambda b, h: (b, h)),
            pl.BlockSpec((seq, LANES), lambda b, h: (b, SLAB_CG + h)),
        ],
        out_specs=pl.BlockSpec((seq, LANES), lambda b, h: (b, h)),
        out_shape=jax.ShapeDtypeStruct((batch * seq, heads * LANES), jnp.bfloat16),
        scratch_shapes=[pltpu.VMEM((LANES, seq), jnp.bfloat16)],
        compiler_params=_cparams(("parallel", "parallel")),
        name="mla_attn",
    )(qf, kf, vc, proj)


def _pad_w_uq(w):
    z = jnp.zeros((w.shape[0], 64), w.dtype)
    pieces = []
    for h in range(4):
        pieces += [w[:, h * 192:(h + 1) * 192], z]
    return jnp.concatenate(pieces, axis=1).astype(jnp.bfloat16)


def _reorder_w_ukv(w):
    k = [w[:, h * 256:h * 256 + 128] for h in range(4)]
    v = [w[:, h * 256 + 128:(h + 1) * 256] for h in range(4)]
    return jnp.concatenate(k + v, axis=1).astype(jnp.bfloat16)


def _mla_up_kernel(cq0_ref, cq1_ref, ckv0_ref, ckv1_ref, ckr_ref, qnw_ref, kvnw_ref, wuq_ref, wukv_ref,
                   cos_ref, sa_ref, sb_ref, qf_ref, kf_ref, vc_ref):
    def norm(ref0, ref1, w_ref):
        c = jnp.concatenate([ref0[...], ref1[...]], axis=1).astype(jnp.float32)
        ms = jnp.mean(c * c, axis=-1, keepdims=True)
        return (c * lax.rsqrt(ms + NORM_EPS) * w_ref[...]).astype(jnp.bfloat16)

    scale = 192 ** -0.5
    q = jnp.dot(norm(cq0_ref, cq1_ref, qnw_ref), wuq_ref[...], preferred_element_type=jnp.float32)
    kv = jnp.dot(norm(ckv0_ref, ckv1_ref, kvnw_ref), wukv_ref[...], preferred_element_type=jnp.float32)
    kr = ckr_ref[...]
    for h in range(4):
        nope = q[:, 2 * h * LANES:(2 * h + 1) * LANES]
        rope = _rope(q[:, (2 * h + 1) * LANES:(2 * h + 2) * LANES], cos_ref[...], sa_ref[...], sb_ref[...])
        qf_ref[:, 2 * h * LANES:(2 * h + 1) * LANES] = (nope * scale).astype(qf_ref.dtype)
        qf_ref[:, (2 * h + 1) * LANES:(2 * h + 2) * LANES] = (rope * scale).astype(qf_ref.dtype)
        kf_ref[:, 2 * h * LANES:(2 * h + 1) * LANES] = kv[:, h * LANES:(h + 1) * LANES].astype(kf_ref.dtype)
        kf_ref[:, (2 * h + 1) * LANES:(2 * h + 2) * LANES] = kr
    vc_ref[...] = kv[:, 4 * LANES:].astype(vc_ref.dtype)


def _mla_up(proj, qn_w, kvn_w, w_uq_p, w_ukv_r, cos, sa, sb):
    m = proj.shape[0]
    tm = 512
    tab = pl.BlockSpec((tm, LANES), lambda i: (i, 0))
    full = lambda shape: pl.BlockSpec(shape, lambda i: (0, 0))
    half = lambda col: pl.BlockSpec((tm, 2 * LANES), lambda i: (i, col))
    return pl.pallas_call(
        _mla_up_kernel,
        grid=(m // tm,),
        in_specs=[
            half(SLAB_CQ // 2), half(SLAB_CQ // 2 + 1), half(SLAB_CKV // 2), half(SLAB_CKV // 2 + 1),
            pl.BlockSpec((tm, LANES), lambda i: (i, SLAB_CKR)),
            full((1, 512)), full((1, 512)), full((512, 1024)), full((512, 1024)),
            tab, tab, tab,
        ],
        out_specs=[
            pl.BlockSpec((tm, 1024), lambda i: (i, 0)),
            pl.BlockSpec((tm, 1024), lambda i: (i, 0)),
            pl.BlockSpec((tm, 512), lambda i: (i, 0)),
        ],
        out_shape=[
            jax.ShapeDtypeStruct((m, 1024), jnp.bfloat16),
            jax.ShapeDtypeStruct((m, 1024), jnp.bfloat16),
            jax.ShapeDtypeStruct((m, 512), jnp.bfloat16),
        ],
        compiler_params=_cparams(("parallel",)),
        name="mla_up",
    )(proj, proj, proj, proj, proj, qn_w.reshape(1, 512), kvn_w.reshape(1, 512), w_uq_p, w_ukv_r, cos, sa, sb)


def _swa_kernel(sink_ref, q_ref, kc_ref, kp_ref, vc_ref, vp_ref, g0_ref, g1_ref, g2_ref, g3_ref, o_ref, *, blocks):
    w = SW_WINDOW
    gate_refs = (g0_ref, g1_ref, g2_ref, g3_ref)
    first = pl.program_id(1) == 0
    lane = lax.broadcasted_iota(jnp.int32, ((blocks + 1) * w, LANES), 1)
    lo = lane < 64

    def halves(prev_ref, cur_ref):
        t = jnp.concatenate([prev_ref[...], cur_ref[...]], axis=0).astype(jnp.float32)
        r = pltpu.roll(t, 64, 1)
        out = []
        for g in range(2):
            a, b = (t, r) if g == 0 else (r, t)
            out.append((jnp.where(lo, a, 0.0).astype(jnp.bfloat16), jnp.where(lo, 0.0, b).astype(jnp.bfloat16)))
        return out

    k_halves = halves(kp_ref, kc_ref)
    v_halves = halves(vp_ref, vc_ref)

    qrow = lax.broadcasted_iota(jnp.int32, (w, 2 * w), 0)
    kcol = lax.broadcasted_iota(jnp.int32, (w, 2 * w), 1)
    rel = qrow + w - kcol
    in_band = (rel >= 0) & (rel < w)
    first_key = jnp.where(first, w, 0)

    for n in range(blocks):
        mask = (in_band & (kcol >= first_key)) if n == 0 else in_band
        for g in range(2):
            k_lo, k_hi = k_halves[g]
            v_lo, v_hi = v_halves[g]
            rows = slice(n * w, (n + 2) * w)
            kk = jnp.concatenate([k_lo[rows], k_hi[rows]], axis=0)
            vv = jnp.concatenate([v_lo[rows], v_hi[rows]], axis=0)
            qs = jnp.concatenate(
                [q_ref[n * w:(n + 1) * w, (4 * g + s) * LANES:(4 * g + s + 1) * LANES] for s in range(4)], axis=0)
            sc = lax.dot_general(qs, kk, _NT, preferred_element_type=jnp.float32)
            p_rows = []
            for s in range(4):
                p_par = []
                for e in range(2):
                    blk = sc[s * w:(s + 1) * w, e * 2 * w:(e + 1) * 2 * w]
                    blk = jnp.where(mask, blk, NEG_INF)
                    sink = sink_ref[8 * g + 2 * s + e]
                    mx = jnp.maximum(jnp.max(blk, axis=-1, keepdims=True), sink)
                    p = jnp.exp(blk - mx)
                    den = jnp.sum(p, axis=-1, keepdims=True) + jnp.exp(sink - mx)
                    p_par.append((p * (1.0 / den)).astype(jnp.bfloat16))
                p_rows.append(jnp.concatenate(p_par, axis=1))
            pm = jnp.concatenate(p_rows, axis=0)
            o = jnp.dot(pm, vv, preferred_element_type=jnp.float32)
            for s in range(4):
                slab = 4 * g + s
                gate = gate_refs[slab // 2][n * w:(n + 1) * w, (slab % 2) * LANES:(slab % 2 + 1) * LANES]
                gate = gate.astype(jnp.float32)
                o_ref[n * w:(n + 1) * w, (4 * g + s) * LANES:(4 * g + s + 1) * LANES] = (
                    o[s * w:(s + 1) * w] * gate).astype(o_ref.dtype)


def _swa(proj, sinks, batch, seq):
    w = SW_WINDOW
    blocks = 4
    rows = blocks * w
    steps = seq // rows
    nb = seq // w
    cur = lambda slab: pl.BlockSpec((rows, LANES), lambda b, i: (b * steps + i, slab))
    prev = lambda slab: pl.BlockSpec(
        (w, LANES), lambda b, i: (jnp.maximum(b * nb + i * blocks - 1, 0), slab))
    gate = lambda pair: pl.BlockSpec((rows, 2 * LANES), lambda b, i: (b * steps + i, SLAB_BG // 2 + pair))
    return pl.pallas_call(
        functools.partial(_swa_kernel, blocks=blocks),
        grid=(batch, steps),
        in_specs=[
            pl.BlockSpec(memory_space=pltpu.SMEM),
            pl.BlockSpec((rows, 1024), lambda b, i: (b * steps + i, SLAB_BQ // 8)),
            cur(SLAB_BK), prev(SLAB_BK), cur(SLAB_BV), prev(SLAB_BV),
            gate(0), gate(1), gate(2), gate(3),
        ],
        out_specs=pl.BlockSpec((rows, 1024), lambda b, i: (b * steps + i, 0)),
        out_shape=jax.ShapeDtypeStruct((batch * seq, 1024), jnp.bfloat16),
        compiler_params=_cparams(("parallel", "arbitrary")),
        name="swa",
    )(sinks.astype(jnp.float32), proj, proj, proj, proj, proj, proj, proj, proj, proj)


def _out_proj_kernel(ya_ref, yb_ref, yc_ref, w_ref, x_ref, postw_ref, o_ref):
    out = jnp.dot(ya_ref[...], w_ref[0:512, :], preferred_element_type=jnp.float32)
    out += jnp.dot(yb_ref[...], w_ref[512:1536, :], preferred_element_type=jnp.float32)
    out += jnp.dot(yc_ref[...], w_ref[1536:2048, :], preferred_element_type=jnp.float32)
    ms = jnp.mean(out * out, axis=-1, keepdims=True)
    o_ref[...] = x_ref[...] + out * lax.rsqrt(ms + NORM_EPS) * postw_ref[...]


def _out_proj(ya, yb, yc, w_out_b, layer, x2, post_w):
    m = x2.shape[0]
    tm = 512
    row = lambda width: pl.BlockSpec((tm, width), lambda i: (i, 0))
    return pl.pallas_call(
        _out_proj_kernel,
        grid=(m // tm,),
        in_specs=[
            row(512), row(1024), row(512),
            pl.BlockSpec((None, D_MODEL, D_MODEL), lambda i: (layer, 0, 0)),
            row(D_MODEL),
            pl.BlockSpec((1, D_MODEL), lambda i: (0, 0)),
        ],
        out_specs=row(D_MODEL),
        out_shape=jax.ShapeDtypeStruct((m, D_MODEL), jnp.float32),
        compiler_params=_cparams(("parallel",)),
        name="out_proj",
    )(ya, yb, yc, w_out_b, x2, post_w.reshape(1, D_MODEL))


def kernel(x, positions, pre_norm_w, post_norm_w, w_in, diff_lambda_q1, diff_lambda_k1, diff_lambda_q2,
           diff_lambda_k2, diff_subln_w, sink_logits, mla_q_norm_w, mla_kv_norm_w, w_uq, w_ukv, w_out):
    batch, seq, d = x.shape
    depth = w_in.shape[0]
    cos, sa, sb = _rope_tables(positions)
    x2 = x.reshape(batch * seq, d)
    w_in_b = _prep_w_in(w_in)
    w_out_b = w_out.astype(jnp.bfloat16)
    for layer in range(depth):
        lam_init = 0.8 - 0.6 * math.exp(-0.3 * layer)
        lam_rows = jnp.stack([diff_lambda_q1[layer], diff_lambda_k1[layer],
                              diff_lambda_q2[layer], diff_lambda_k2[layer]]).astype(jnp.float32)
        proj = _in_proj(x2, pre_norm_w[layer], w_in_b, layer, cos, sa, sb)
        ya = _diff_attn(proj, lam_rows, diff_subln_w[layer], batch, seq, lam_init)
        yb = _swa(proj, sink_logits[layer], batch, seq)
        qf, kf, vc = _mla_up(proj, mla_q_norm_w[layer], mla_kv_norm_w[layer],
                             _pad_w_uq(w_uq[layer]), _reorder_w_ukv(w_ukv[layer]), cos, sa, sb)
        yc = _mla_attn(qf, kf, vc, proj, batch, seq)
        x2 = _out_proj(ya, yb, yc, w_out_b, layer, x2, post_norm_w[layer])
    return x2.reshape(batch, seq, d)
```

```python
import functools
import math

import jax
import jax.numpy as jnp
import numpy as np
from jax import lax
from jax.experimental import pallas as pl
from jax.experimental.pallas import tpu as pltpu

D_MODEL = 2048
ROPE_THETA = 10000.0
NORM_EPS = 1e-6
NEG_INF = -1e30
DA_SUBLN_EPS = 1e-5
SW_WINDOW = 128
LANES = 128

IN_TILE = 1024
D_IN = 5952
N_SLABS = 48
SLAB_AQ, SLAB_AK, SLAB_AV, SLAB_AG = 0, 4, 8, 12
SLAB_BQ, SLAB_BK, SLAB_BV, SLAB_BG = 16, 24, 25, 26
SLAB_CQ, SLAB_CKV, SLAB_CKR, SLAB_CG = 34, 38, 42, 43

VMEM_LIMIT = 56 * 1024 * 1024

_NT = (((1,), (1,)), ((), ()))


def _cparams(sem):
    return pltpu.CompilerParams(dimension_semantics=sem, vmem_limit_bytes=VMEM_LIMIT)


def _rope_table_kernel(pos_ref, freq_ref, cos_ref, sa_ref, sb_ref):
    ang = pos_ref[...].astype(jnp.float32) * freq_ref[...]
    c, s = jnp.cos(ang), jnp.sin(ang)
    lane = lax.broadcasted_iota(jnp.int32, ang.shape, 1)
    first_half = (lane % 64) < 32
    cos_ref[...] = c
    sa_ref[...] = jnp.where(first_half, -s, 0.0)
    sb_ref[...] = jnp.where(first_half, 0.0, s)


def _rope_tables(positions):
    m = positions.size
    pos_b = jnp.broadcast_to(positions.reshape(m, 1), (m, LANES))
    inv_freq = jnp.power(ROPE_THETA, -jnp.arange(0, 64, 2, dtype=jnp.float32) / 64)
    freq = jnp.tile(inv_freq, 4).reshape(1, LANES)
    tm = 1024
    spec = pl.BlockSpec((tm, LANES), lambda i: (i, 0))
    return pl.pallas_call(
        _rope_table_kernel,
        grid=(m // tm,),
        in_specs=[spec, pl.BlockSpec((1, LANES), lambda i: (0, 0))],
        out_specs=[spec, spec, spec],
        out_shape=[jax.ShapeDtypeStruct((m, LANES), jnp.float32)] * 3,
        compiler_params=_cparams(("parallel",)),
        name="rope_tables",
    )(pos_b, freq)


def _rope(a, cos, sa, sb):
    return a * cos + pltpu.roll(a, 96, 1) * sa + pltpu.roll(a, 32, 1) * sb


def _silu(a):
    return a * (1.0 / (1.0 + jnp.exp(-a)))


_IN_KINDS = (
    ("rope_q",) * 4 + ("rope",) * 4,
    ("plain",) * 4 + ("silu",) * 4,
    ("rope_q",) * 8,
    ("rope", "plain") + ("silu",) * 6,
    ("silu",) * 2 + ("plain",) * 6,
    ("plain",) * 2 + ("kr",) + ("gate_c",) * 4 + ("skip",),
)


def _in_proj_kernel(x_ref, prew_ref, w_ref, cos_ref, sa_ref, sb_ref, o_ref, h_ref):
    j = pl.program_id(1)

    @pl.when(j == 0)
    def _():
        x = x_ref[...]
        ms = jnp.mean(x * x, axis=-1, keepdims=True)
        h_ref[...] = (x * lax.rsqrt(ms + NORM_EPS) * prew_ref[...]).astype(jnp.bfloat16)

    def epilogue(kinds):
        acc = jnp.dot(h_ref[...], w_ref[...].astype(jnp.bfloat16), preferred_element_type=jnp.float32)
        slab = lambda s: acc[:, s * LANES:(s + 1) * LANES]
        lane = lax.broadcasted_iota(jnp.int32, (acc.shape[0], LANES), 1)
        for s, kind in enumerate(kinds):
            a = slab(s)
            if kind == "skip":
                a = jnp.zeros_like(a)
            if kind in ("rope", "rope_q", "kr"):
                a = _rope(a, cos_ref[...], sa_ref[...], sb_ref[...])
                if kind == "rope_q":
                    a = a * 0.125
            elif kind == "silu":
                a = _silu(a)
            elif kind == "gate_c":
                a = jnp.where(lane < 64, pltpu.roll(_silu(slab(s - 1)), 64, 1), pltpu.roll(_silu(a), 64, 1))
            o_ref[:, s * LANES:(s + 1) * LANES] = a.astype(o_ref.dtype)

    for t, kinds in enumerate(_IN_KINDS):
        pl.when(j == t)(functools.partial(epilogue, kinds))


def _in_proj(x2, pre_w, w_in, layer, cos, sa, sb):
    m = x2.shape[0]
    tm, tn = 1024, IN_TILE
    tab = pl.BlockSpec((tm, LANES), lambda i, j: (i, 0))
    return pl.pallas_call(
        _in_proj_kernel,
        grid=(m // tm, N_SLABS * LANES // tn),
        in_specs=[
            pl.BlockSpec((tm, D_MODEL), lambda i, j: (i, 0)),
            pl.BlockSpec((1, D_MODEL), lambda i, j: (0, 0)),
            pl.BlockSpec((None, D_MODEL, tn), lambda i, j: (layer, 0, j)),
            tab, tab, tab,
        ],
        out_specs=pl.BlockSpec((tm, tn), lambda i, j: (i, j)),
        out_shape=jax.ShapeDtypeStruct((m, N_SLABS * LANES), jnp.bfloat16),
        scratch_shapes=[pltpu.VMEM((tm, D_MODEL), jnp.bfloat16)],
        compiler_params=_cparams(("parallel", "arbitrary")),
        name="in_proj",
    )(x2, pre_w.reshape(1, D_MODEL), w_in, cos, sa, sb)


ATTN_TQ = 256


def _transpose_v(v_ref, vt_ref):
    vt_ref[...] = v_ref[...].astype(jnp.float32).T.astype(vt_ref.dtype)


def _causal_rows(q, k_ref, vt_ref, i, tq):
    n_keys = (i + 1) * tq
    s = lax.dot_general(k_ref[0:n_keys, :], q, _NT, preferred_element_type=jnp.float32)
    key = lax.broadcasted_iota(jnp.int32, (tq, tq), 0)
    qry = lax.broadcasted_iota(jnp.int32, (tq, tq), 1)
    tail = jnp.where(key <= qry, s[n_keys - tq:], NEG_INF)
    s = tail if i == 0 else jnp.concatenate([s[:n_keys - tq], tail], axis=0)
    m = jnp.max(s, axis=0, keepdims=True)
    p = jnp.exp(s - m)
    l = jnp.sum(p, axis=0, keepdims=True)
    o_t = jnp.dot(vt_ref[:, 0:n_keys], p.astype(jnp.bfloat16), preferred_element_type=jnp.float32)
    return o_t, l


def _diff_attn_kernel(q_ref, k_ref, v_ref, g_ref, lam_ref, subw_ref, o_ref, vt_ref, *, tq, lam_init):
    _transpose_v(v_ref, vt_ref)
    t = lam_ref[...]
    lam = (jnp.exp(jnp.sum(t[0:1] * t[1:2], axis=-1, keepdims=True))
           - jnp.exp(jnp.sum(t[2:3] * t[3:4], axis=-1, keepdims=True)) + lam_init)
    lane = lax.broadcasted_iota(jnp.int32, (tq, LANES), 1)
    for i in range(q_ref.shape[0] // tq):
        rows = slice(i * tq, (i + 1) * tq)
        q = q_ref[rows, :]
        zero = jnp.zeros_like(q)
        q1 = jnp.where(lane < 64, q, zero)
        q2 = jnp.where(lane >= 64, q, zero)
        o1, l1 = _causal_rows(q1, k_ref, vt_ref, i, tq)
        o2, l2 = _causal_rows(q2, k_ref, vt_ref, i, tq)
        o = o1 * (1.0 / l1) - lam * (o2 * (1.0 / l2))
        ms = jnp.mean(o * o, axis=0, keepdims=True)
        o = (o * lax.rsqrt(ms + DA_SUBLN_EPS)).T
        o = o * subw_ref[...] * (1.0 - lam_init)
        o_ref[rows, :] = (o * g_ref[rows, :].astype(jnp.float32)).astype(o_ref.dtype)


def _diff_attn(proj, lam_rows, subln_w, batch, seq, lam_init):
    heads = 4
    slab = lambda first: pl.BlockSpec((seq, LANES), lambda b, h: (b, first + h))
    return pl.pallas_call(
        functools.partial(_diff_attn_kernel, tq=ATTN_TQ, lam_init=lam_init),
        grid=(batch, heads),
        in_specs=[
            slab(SLAB_AQ), slab(SLAB_AK), slab(SLAB_AV), slab(SLAB_AG),
            pl.BlockSpec((4, 64), lambda b, h: (0, 0)),
            pl.BlockSpec((1, LANES), lambda b, h: (0, 0)),
        ],
        out_specs=pl.BlockSpec((seq, LANES), lambda b, h: (b, h)),
        out_shape=jax.ShapeDtypeStruct((batch * seq, heads * LANES), jnp.bfloat16),
        scratch_shapes=[pltpu.VMEM((LANES, seq), jnp.bfloat16)],
        compiler_params=_cparams(("parallel", "parallel")),
        name="diff_attn",
    )(proj, proj, proj, proj, lam_rows, subln_w.reshape(1, LANES))


def _mla_attn_kernel(q_ref, k_ref, v_ref, g_ref, o_ref, vt_ref, *, tq):
    _transpose_v(v_ref, vt_ref)
    for i in range(q_ref.shape[0] // tq):
        rows = slice(i * tq, (i + 1) * tq)
        o_t, l = _causal_rows(q_ref[rows, :], k_ref, vt_ref, i, tq)
        o = (o_t * (1.0 / l)).T
        o_ref[rows, :] = (o * g_ref[rows, :].astype(jnp.float32)).astype(o_ref.dtype)


def _mla_attn(qf, kf, vc, proj, batch, seq):
    heads = 4
    return pl.pallas_call(
        functools.partial(_mla_attn_kernel, tq=ATTN_TQ),
        grid=(batch, heads),
        in_specs=[
            pl.BlockSpec((seq, 2 * LANES), lambda b, h: (b, h)),
            pl.BlockSpec((seq, 2 * LANES), lambda b, h: (b, h)),
            pl.BlockSpec((seq, LANES), lambda b, h: (b, h)),
            pl.BlockSpec((seq, LANES), lambda b, h: (b, SLAB_CG + h)),
        ],
        out_specs=pl.BlockSpec((seq, LANES), lambda b, h: (b, h)),
        out_shape=jax.ShapeDtypeStruct((batch * seq, heads * LANES), jnp.bfloat16),
        scratch_shapes=[pltpu.VMEM((LANES, seq), jnp.bfloat16)],
        compiler_params=_cparams(("parallel", "parallel")),
        name="mla_attn",
    )(qf, kf, vc, proj)


def _pad_w_uq(w):
    z = jnp.zeros((w.shape[0], 64), w.dtype)
    pieces = []
    for h in range(4):
        pieces += [w[:, h * 192:(h + 1) * 192], z]
    return jnp.concatenate(pieces, axis=1).astype(jnp.bfloat16)


def _reorder_w_ukv(w):
    k = [w[:, h * 256:h * 256 + 128] for h in range(4)]
    v = [w[:, h * 256 + 128:(h + 1) * 256] for h in range(4)]
    return jnp.concatenate(k + v, axis=1).astype(jnp.bfloat16)


def _mla_up_kernel(cq0_ref, cq1_ref, ckv0_ref, ckv1_ref, ckr_ref, qnw_ref, kvnw_ref, wuq_ref, wukv_ref,
                   cos_ref, sa_ref, sb_ref, qf_ref, kf_ref, vc_ref):
    def norm(ref0, ref1, w_ref):
        c = jnp.concatenate([ref0[...], ref1[...]], axis=1).astype(jnp.float32)
        ms = jnp.mean(c * c, axis=-1, keepdims=True)
        return (c * lax.rsqrt(ms + NORM_EPS) * w_ref[...]).astype(jnp.bfloat16)

    scale = 192 ** -0.5
    q = jnp.dot(norm(cq0_ref, cq1_ref, qnw_ref), wuq_ref[...], preferred_element_type=jnp.float32)
    kv = jnp.dot(norm(ckv0_ref, ckv1_ref, kvnw_ref), wukv_ref[...], preferred_element_type=jnp.float32)
    kr = ckr_ref[...]
    for h in range(4):
        nope = q[:, 2 * h * LANES:(2 * h + 1) * LANES]
        rope = _rope(q[:, (2 * h + 1) * LANES:(2 * h + 2) * LANES], cos_ref[...], sa_ref[...], sb_ref[...])
        qf_ref[:, 2 * h * LANES:(2 * h + 1) * LANES] = (nope * scale).astype(qf_ref.dtype)
        qf_ref[:, (2 * h + 1) * LANES:(2 * h + 2) * LANES] = (rope * scale).astype(qf_ref.dtype)
        kf_ref[:, 2 * h * LANES:(2 * h + 1) * LANES] = kv[:, h * LANES:(h + 1) * LANES].astype(kf_ref.dtype)
        kf_ref[:, (2 * h + 1) * LANES:(2 * h + 2) * LANES] = kr
    vc_ref[...] = kv[:, 4 * LANES:].astype(vc_ref.dtype)


def _mla_up(proj, qn_w, kvn_w, w_uq_p, w_ukv_r, cos, sa, sb):
    m = proj.shape[0]
    tm = 512
    tab = pl.BlockSpec((tm, LANES), lambda i: (i, 0))
    full = lambda shape: pl.BlockSpec(shape, lambda i: (0, 0))
    half = lambda col: pl.BlockSpec((tm, 2 * LANES), lambda i: (i, col))
    return pl.pallas_call(
        _mla_up_kernel,
        grid=(m // tm,),
        in_specs=[
            half(SLAB_CQ // 2), half(SLAB_CQ // 2 + 1), half(SLAB_CKV // 2), half(SLAB_CKV // 2 + 1),
            pl.BlockSpec((tm, LANES), lambda i: (i, SLAB_CKR)),
            full((1, 512)), full((1, 512)), full((512, 1024)), full((512, 1024)),
            tab, tab, tab,
        ],
        out_specs=[
            pl.BlockSpec((tm, 1024), lambda i: (i, 0)),
            pl.BlockSpec((tm, 1024), lambda i: (i, 0)),
            pl.BlockSpec((tm, 512), lambda i: (i, 0)),
        ],
        out_shape=[
            jax.ShapeDtypeStruct((m, 1024), jnp.bfloat16),
            jax.ShapeDtypeStruct((m, 1024), jnp.bfloat16),
            jax.ShapeDtypeStruct((m, 512), jnp.bfloat16),
        ],
        compiler_params=_cparams(("parallel",)),
        name="mla_up",
    )(proj, proj, proj, proj, proj, qn_w.reshape(1, 512), kvn_w.reshape(1, 512), w_uq_p, w_ukv_r, cos, sa, sb)


def _swa_kernel(sink_ref, q_ref, kc_ref, kp_ref, vc_ref, vp_ref, g0_ref, g1_ref, g2_ref, g3_ref, o_ref, *, blocks):
    w = SW_WINDOW
    gate_refs = (g0_ref, g1_ref, g2_ref, g3_ref)
    first = pl.program_id(1) == 0
    lane = lax.broadcasted_iota(jnp.int32, ((blocks + 1) * w, LANES), 1)
    lo = lane < 64

    def halves(prev_ref, cur_ref):
        t = jnp.concatenate([prev_ref[...], cur_ref[...]], axis=0).astype(jnp.float32)
        r = pltpu.roll(t, 64, 1)
        out = []
        for g in range(2):
            a, b = (t, r) if g == 0 else (r, t)
            out.append((jnp.where(lo, a, 0.0).astype(jnp.bfloat16), jnp.where(lo, 0.0, b).astype(jnp.bfloat16)))
        return out

    k_halves = halves(kp_ref, kc_ref)
    v_halves = halves(vp_ref, vc_ref)

    qrow = lax.broadcasted_iota(jnp.int32, (w, 2 * w), 0)
    kcol = lax.broadcasted_iota(jnp.int32, (w, 2 * w), 1)
    rel = qrow + w - kcol
    in_band = (rel >= 0) & (rel < w)
    first_key = jnp.where(first, w, 0)

    for n in range(blocks):
        mask = (in_band & (kcol >= first_key)) if n == 0 else in_band
        for g in range(2):
            k_lo, k_hi = k_halves[g]
            v_lo, v_hi = v_halves[g]
            rows = slice(n * w, (n + 2) * w)
            kk = jnp.concatenate([k_lo[rows], k_hi[rows]], axis=0)
            vv = jnp.concatenate([v_lo[rows], v_hi[rows]], axis=0)
            qs = jnp.concatenate(
                [q_ref[n * w:(n + 1) * w, (4 * g + s) * LANES:(4 * g + s + 1) * LANES] for s in range(4)], axis=0)
            sc = lax.dot_general(qs, kk, _NT, preferred_element_type=jnp.float32)
            p_rows = []
            for s in range(4):
                p_par = []
                for e in range(2):
                    blk = sc[s * w:(s + 1) * w, e * 2 * w:(e + 1) * 2 * w]
                    blk = jnp.where(mask, blk, NEG_INF)
                    sink = sink_ref[8 * g + 2 * s + e]
                    mx = jnp.maximum(jnp.max(blk, axis=-1, keepdims=True), sink)
                    p = jnp.exp(blk - mx)
                    den = jnp.sum(p, axis=-1, keepdims=True) + jnp.exp(sink - mx)
                    p_par.append((p * (1.0 / den)).astype(jnp.bfloat16))
                p_rows.append(jnp.concatenate(p_par, axis=1))
            pm = jnp.concatenate(p_rows, axis=0)
            o = jnp.dot(pm, vv, preferred_element_type=jnp.float32)
            for s in range(4):
                slab = 4 * g + s
                gate = gate_refs[slab // 2][n * w:(n + 1) * w, (slab % 2) * LANES:(slab % 2 + 1) * LANES]
                gate = gate.astype(jnp.float32)
                o_ref[n * w:(n + 1) * w, (4 * g + s) * LANES:(4 * g + s + 1) * LANES] = (
                    o[s * w:(s + 1) * w] * gate).astype(o_ref.dtype)


def _swa(proj, sinks, batch, seq):
    w = SW_WINDOW
    blocks = 4
    rows = blocks * w
    steps = seq // rows
    nb = seq // w
    cur = lambda slab: pl.BlockSpec((rows, LANES), lambda b, i: (b * steps + i, slab))
    prev = lambda slab: pl.BlockSpec(
        (w, LANES), lambda b, i: (jnp.maximum(b * nb + i * blocks - 1, 0), slab))
    gate = lambda pair: pl.BlockSpec((rows, 2 * LANES), lambda b, i: (b * steps + i, SLAB_BG // 2 + pair))
    return pl.pallas_call(
        functools.partial(_swa_kernel, blocks=blocks),
        grid=(batch, steps),
        in_specs=[
            pl.BlockSpec(memory_space=pltpu.SMEM),
            pl.BlockSpec((rows, 1024), lambda b, i: (b * steps + i, SLAB_BQ // 8)),
            cur(SLAB_BK), prev(SLAB_BK), cur(SLAB_BV), prev(SLAB_BV),
            gate(0), gate(1), gate(2), gate(3),
        ],
        out_specs=pl.BlockSpec((rows, 1024), lambda b, i: (b * steps + i, 0)),
        out_shape=jax.ShapeDtypeStruct((batch * seq, 1024), jnp.bfloat16),
        compiler_params=_cparams(("parallel", "arbitrary")),
        name="swa",
    )(sinks.astype(jnp.float32), proj, proj, proj, proj, proj, proj, proj, proj, proj)


def _out_proj_kernel(ya_ref, yb_ref, yc_ref, wf_ref, x_ref, postw_ref, o_ref, w_ref):
    @pl.when(pl.program_id(0) == 0)
    def _():
        w_ref[...] = wf_ref[...].astype(w_ref.dtype)

    out = jnp.dot(ya_ref[...], w_ref[0:512, :], preferred_element_type=jnp.float32)
    out += jnp.dot(yb_ref[...], w_ref[512:1536, :], preferred_element_type=jnp.float32)
    out += jnp.dot(yc_ref[...], w_ref[1536:2048, :], preferred_element_type=jnp.float32)
    ms = jnp.mean(out * out, axis=-1, keepdims=True)
    o_ref[...] = x_ref[...] + out * lax.rsqrt(ms + NORM_EPS) * postw_ref[...]


def _out_proj(ya, yb, yc, w_out, layer, x2, post_w):
    m = x2.shape[0]
    tm = 512
    row = lambda width: pl.BlockSpec((tm, width), lambda i: (i, 0))
    return pl.pallas_call(
        _out_proj_kernel,
        grid=(m // tm,),
        in_specs=[
            row(512), row(1024), row(512),
            pl.BlockSpec((None, D_MODEL, D_MODEL), lambda i: (layer, 0, 0), pipeline_mode=pl.Buffered(1)),
            row(D_MODEL),
            pl.BlockSpec((1, D_MODEL), lambda i: (0, 0)),
        ],
        out_specs=row(D_MODEL),
        out_shape=jax.ShapeDtypeStruct((m, D_MODEL), jnp.float32),
        scratch_shapes=[pltpu.VMEM((D_MODEL, D_MODEL), jnp.bfloat16)],
        compiler_params=_cparams(("arbitrary",)),
        name="out_proj",
    )(ya, yb, yc, w_out, x2, post_w.reshape(1, D_MODEL))


def kernel(x, positions, pre_norm_w, post_norm_w, w_in, diff_lambda_q1, diff_lambda_k1, diff_lambda_q2,
           diff_lambda_k2, diff_subln_w, sink_logits, mla_q_norm_w, mla_kv_norm_w, w_uq, w_ukv, w_out):
    batch, seq, d = x.shape
    depth = w_in.shape[0]
    cos, sa, sb = _rope_tables(positions)
    x2 = x.reshape(batch * seq, d)
    for layer in range(depth):
        lam_init = 0.8 - 0.6 * math.exp(-0.3 * layer)
        lam_rows = jnp.stack([diff_lambda_q1[layer], diff_lambda_k1[layer],
                              diff_lambda_q2[layer], diff_lambda_k2[layer]]).astype(jnp.float32)
        proj = _in_proj(x2, pre_norm_w[layer], w_in, layer, cos, sa, sb)
        ya = _diff_attn(proj, lam_rows, diff_subln_w[layer], batch, seq, lam_init)
        yb = _swa(proj, sink_logits[layer], batch, seq)
        qf, kf, vc = _mla_up(proj, mla_q_norm_w[layer], mla_kv_norm_w[layer],
                             _pad_w_uq(w_uq[layer]), _reorder_w_ukv(w_ukv[layer]), cos, sa, sb)
        yc = _mla_attn(qf, kf, vc, proj, batch, seq)
        x2 = _out_proj(ya, yb, yc, w_out, layer, x2, post_norm_w[layer])
    return x2.reshape(batch, seq, d)
```

```python
import functools
import math

import jax
import jax.numpy as jnp
import numpy as np
from jax import lax
from jax.experimental import pallas as pl
from jax.experimental.pallas import tpu as pltpu

D_MODEL = 2048
ROPE_THETA = 10000.0
NORM_EPS = 1e-6
NEG_INF = -1e30
DA_SUBLN_EPS = 1e-5
SW_WINDOW = 128
LANES = 128

IN_TILE = 1024
D_IN = 5952
N_SLABS = 48
SLAB_AQ, SLAB_AK, SLAB_AV, SLAB_AG = 0, 4, 8, 12
SLAB_BQ, SLAB_BK, SLAB_BV, SLAB_BG = 16, 24, 25, 26
SLAB_CQ, SLAB_CKV, SLAB_CKR, SLAB_CG = 34, 38, 42, 43

VMEM_LIMIT = 56 * 1024 * 1024

_NT = (((1,), (1,)), ((), ()))


def _cparams(sem):
    return pltpu.CompilerParams(dimension_semantics=sem, vmem_limit_bytes=VMEM_LIMIT)


def _rope_table_kernel(pos_ref, freq_ref, cos_ref, sa_ref, sb_ref):
    ang = pos_ref[...].astype(jnp.float32) * freq_ref[...]
    c, s = jnp.cos(ang), jnp.sin(ang)
    lane = lax.broadcasted_iota(jnp.int32, ang.shape, 1)
    first_half = (lane % 64) < 32
    cos_ref[...] = c
    sa_ref[...] = jnp.where(first_half, -s, 0.0)
    sb_ref[...] = jnp.where(first_half, 0.0, s)


def _rope_tables(positions):
    m = positions.size
    pos_b = jnp.broadcast_to(positions.reshape(m, 1), (m, LANES))
    inv_freq = jnp.power(ROPE_THETA, -jnp.arange(0, 64, 2, dtype=jnp.float32) / 64)
    freq = jnp.tile(inv_freq, 4).reshape(1, LANES)
    tm = 1024
    spec = pl.BlockSpec((tm, LANES), lambda i: (i, 0))
    return pl.pallas_call(
        _rope_table_kernel,
        grid=(m // tm,),
        in_specs=[spec, pl.BlockSpec((1, LANES), lambda i: (0, 0))],
        out_specs=[spec, spec, spec],
        out_shape=[jax.ShapeDtypeStruct((m, LANES), jnp.float32)] * 3,
        compiler_params=_cparams(("parallel",)),
        name="rope_tables",
    )(pos_b, freq)


def _rope(a, cos, sa, sb):
    return a * cos + pltpu.roll(a, 96, 1) * sa + pltpu.roll(a, 32, 1) * sb


def _silu(a):
    return a * (1.0 / (1.0 + jnp.exp(-a)))


_IN_KINDS = (
    ("rope_q",) * 4 + ("rope",) * 4,
    ("plain",) * 4 + ("silu",) * 4,
    ("rope_q",) * 8,
    ("rope", "plain") + ("silu",) * 6,
    ("silu",) * 2 + ("plain",) * 6,
    ("plain",) * 2 + ("kr",) + ("gate_c",) * 4 + ("skip",),
)


def _in_proj_kernel(x_ref, prew_ref, w_ref, cos_ref, sa_ref, sb_ref, o_ref, h_ref):
    j = pl.program_id(1)

    @pl.when(j == 0)
    def _():
        x = x_ref[...]
        ms = jnp.mean(x * x, axis=-1, keepdims=True)
        h_ref[...] = (x * lax.rsqrt(ms + NORM_EPS) * prew_ref[...]).astype(jnp.bfloat16)

    def epilogue(kinds):
        acc = lax.dot_general(h_ref[...], w_ref[...].astype(jnp.bfloat16), _NT, preferred_element_type=jnp.float32)
        slab = lambda s: acc[:, s * LANES:(s + 1) * LANES]
        lane = lax.broadcasted_iota(jnp.int32, (acc.shape[0], LANES), 1)
        for s, kind in enumerate(kinds):
            a = slab(s)
            if kind == "skip":
                a = jnp.zeros_like(a)
            if kind in ("rope", "rope_q", "kr"):
                a = _rope(a, cos_ref[...], sa_ref[...], sb_ref[...])
                if kind == "rope_q":
                    a = a * 0.125
            elif kind == "silu":
                a = _silu(a)
            elif kind == "gate_c":
                a = jnp.where(lane < 64, pltpu.roll(_silu(slab(s - 1)), 64, 1), pltpu.roll(_silu(a), 64, 1))
            o_ref[:, s * LANES:(s + 1) * LANES] = a.astype(o_ref.dtype)

    for t, kinds in enumerate(_IN_KINDS):
        pl.when(j == t)(functools.partial(epilogue, kinds))


def _in_proj(x2, pre_w, w_in, layer, cos, sa, sb):
    m = x2.shape[0]
    tm, tn = 1024, IN_TILE
    tab = pl.BlockSpec((tm, LANES), lambda i, j: (i, 0))
    return pl.pallas_call(
        _in_proj_kernel,
        grid=(m // tm, N_SLABS * LANES // tn),
        in_specs=[
            pl.BlockSpec((tm, D_MODEL), lambda i, j: (i, 0)),
            pl.BlockSpec((1, D_MODEL), lambda i, j: (0, 0)),
            pl.BlockSpec((None, tn, D_MODEL), lambda i, j: (layer, j, 0)),
            tab, tab, tab,
        ],
        out_specs=pl.BlockSpec((tm, tn), lambda i, j: (i, j)),
        out_shape=jax.ShapeDtypeStruct((m, N_SLABS * LANES), jnp.bfloat16),
        scratch_shapes=[pltpu.VMEM((tm, D_MODEL), jnp.bfloat16)],
        compiler_params=_cparams(("parallel", "arbitrary")),
        name="in_proj",
    )(x2, pre_w.reshape(1, D_MODEL), jnp.swapaxes(w_in, 1, 2), cos, sa, sb)


ATTN_TQ = 256


def _transpose_v(v_ref, vt_ref):
    vt_ref[...] = v_ref[...].astype(jnp.float32).T.astype(vt_ref.dtype)


def _causal_rows(q, k_ref, vt_ref, i, tq):
    n_keys = (i + 1) * tq
    s = lax.dot_general(k_ref[0:n_keys, :], q, _NT, preferred_element_type=jnp.float32)
    key = lax.broadcasted_iota(jnp.int32, (tq, tq), 0)
    qry = lax.broadcasted_iota(jnp.int32, (tq, tq), 1)
    tail = jnp.where(key <= qry, s[n_keys - tq:], NEG_INF)
    s = tail if i == 0 else jnp.concatenate([s[:n_keys - tq], tail], axis=0)
    m = jnp.max(s, axis=0, keepdims=True)
    p = jnp.exp(s - m)
    l = jnp.sum(p, axis=0, keepdims=True)
    o_t = jnp.dot(vt_ref[:, 0:n_keys], p.astype(jnp.bfloat16), preferred_element_type=jnp.float32)
    return o_t, l


def _diff_attn_kernel(q_ref, k_ref, v_ref, g_ref, lam_ref, subw_ref, o_ref, vt_ref, *, tq, lam_init):
    _transpose_v(v_ref, vt_ref)
    t = lam_ref[...]
    lam = (jnp.exp(jnp.sum(t[0:1] * t[1:2], axis=-1, keepdims=True))
           - jnp.exp(jnp.sum(t[2:3] * t[3:4], axis=-1, keepdims=True)) + lam_init)
    lane = lax.broadcasted_iota(jnp.int32, (tq, LANES), 1)
    for i in range(q_ref.shape[0] // tq):
        rows = slice(i * tq, (i + 1) * tq)
        q = q_ref[rows, :]
        zero = jnp.zeros_like(q)
        q1 = jnp.where(lane < 64, q, zero)
        q2 = jnp.where(lane >= 64, q, zero)
        o1, l1 = _causal_rows(q1, k_ref, vt_ref, i, tq)
        o2, l2 = _causal_rows(q2, k_ref, vt_ref, i, tq)
        o = o1 * (1.0 / l1) - lam * (o2 * (1.0 / l2))
        ms = jnp.mean(o * o, axis=0, keepdims=True)
        o = (o * lax.rsqrt(ms + DA_SUBLN_EPS)).T
        o = o * subw_ref[...] * (1.0 - lam_init)
        o_ref[rows, :] = (o * g_ref[rows, :].astype(jnp.float32)).astype(o_ref.dtype)


def _diff_attn(proj, lam_rows, subln_w, batch, seq, lam_init):
    heads = 4
    slab = lambda first: pl.BlockSpec((seq, LANES), lambda b, h: (b, first + h))
    return pl.pallas_call(
        functools.partial(_diff_attn_kernel, tq=ATTN_TQ, lam_init=lam_init),
        grid=(batch, heads),
        in_specs=[
            slab(SLAB_AQ), slab(SLAB_AK), slab(SLAB_AV), slab(SLAB_AG),
            pl.BlockSpec((4, 64), lambda b, h: (0, 0)),
            pl.BlockSpec((1, LANES), lambda b, h: (0, 0)),
        ],
        out_specs=pl.BlockSpec((seq, LANES), lambda b, h: (b, h)),
        out_shape=jax.ShapeDtypeStruct((batch * seq, heads * LANES), jnp.bfloat16),
        scratch_shapes=[pltpu.VMEM((LANES, seq), jnp.bfloat16)],
        compiler_params=_cparams(("parallel", "parallel")),
        name="diff_attn",
    )(proj, proj, proj, proj, lam_rows, subln_w.reshape(1, LANES))


def _mla_attn_kernel(q_ref, k_ref, v_ref, g_ref, o_ref, vt_ref, *, tq):
    _transpose_v(v_ref, vt_ref)
    for i in range(q_ref.shape[0] // tq):
        rows = slice(i * tq, (i + 1) * tq)
        o_t, l = _causal_rows(q_ref[rows, :], k_ref, vt_ref, i, tq)
        o = (o_t * (1.0 / l)).T
        o_ref[rows, :] = (o * g_ref[rows, :].astype(jnp.float32)).astype(o_ref.dtype)


def _mla_attn(qf, kf, vc, proj, batch, seq):
    heads = 4
    return pl.pallas_call(
        functools.partial(_mla_attn_kernel, tq=ATTN_TQ),
        grid=(batch, heads),
        in_specs=[
            pl.BlockSpec((seq, 2 * LANES), lambda b, h: (b, h)),
            pl.BlockSpec((seq, 2 * LANES), lambda b, h: (b, h)),
            pl.BlockSpec((seq, LANES), lambda b, h: (b, h)),
            pl.BlockSpec((seq, LANES), lambda b, h: (b, SLAB_CG + h)),
        ],
        out_specs=pl.BlockSpec((seq, LANES), lambda b, h: (b, h)),
        out_shape=jax.ShapeDtypeStruct((batch * seq, heads * LANES), jnp.bfloat16),
        scratch_shapes=[pltpu.VMEM((LANES, seq), jnp.bfloat16)],
        compiler_params=_cparams(("parallel", "parallel")),
        name="mla_attn",
    )(qf, kf, vc, proj)


def _pad_w_uq(w):
    z = jnp.zeros((w.shape[0], 64), w.dtype)
    pieces = []
    for h in range(4):
        pieces += [w[:, h * 192:(h + 1) * 192], z]
    return jnp.concatenate(pieces, axis=1).astype(jnp.bfloat16)


def _reorder_w_ukv(w):
    k = [w[:, h * 256:h * 256 + 128] for h in range(4)]
    v = [w[:, h * 256 + 128:(h + 1) * 256] for h in range(4)]
    return jnp.concatenate(k + v, axis=1).astype(jnp.bfloat16)


def _mla_up_kernel(cq0_ref, cq1_ref, ckv0_ref, ckv1_ref, ckr_ref, qnw_ref, kvnw_ref, wuq_ref, wukv_ref,
                   cos_ref, sa_ref, sb_ref, qf_ref, kf_ref, vc_ref):
    def norm(ref0, ref1, w_ref):
        c = jnp.concatenate([ref0[...], ref1[...]], axis=1).astype(jnp.float32)
        ms = jnp.mean(c * c, axis=-1, keepdims=True)
        return (c * lax.rsqrt(ms + NORM_EPS) * w_ref[...]).astype(jnp.bfloat16)

    scale = 192 ** -0.5
    q = jnp.dot(norm(cq0_ref, cq1_ref, qnw_ref), wuq_ref[...], preferred_element_type=jnp.float32)
    kv = jnp.dot(norm(ckv0_ref, ckv1_ref, kvnw_ref), wukv_ref[...], preferred_element_type=jnp.float32)
    kr = ckr_ref[...]
    for h in range(4):
        nope = q[:, 2 * h * LANES:(2 * h + 1) * LANES]
        rope = _rope(q[:, (2 * h + 1) * LANES:(2 * h + 2) * LANES], cos_ref[...], sa_ref[...], sb_ref[...])
        qf_ref[:, 2 * h * LANES:(2 * h + 1) * LANES] = (nope * scale).astype(qf_ref.dtype)
        qf_ref[:, (2 * h + 1) * LANES:(2 * h + 2) * LANES] = (rope * scale).astype(qf_ref.dtype)
        kf_ref[:, 2 * h * LANES:(2 * h + 1) * LANES] = kv[:, h * LANES:(h + 1) * LANES].astype(kf_ref.dtype)
        kf_ref[:, (2 * h + 1) * LANES:(2 * h + 2) * LANES] = kr
    vc_ref[...] = kv[:, 4 * LANES:].astype(vc_ref.dtype)


def _mla_up(proj, qn_w, kvn_w, w_uq_p, w_ukv_r, cos, sa, sb):
    m = proj.shape[0]
    tm = 512
    tab = pl.BlockSpec((tm, LANES), lambda i: (i, 0))
    full = lambda shape: pl.BlockSpec(shape, lambda i: (0, 0))
    half = lambda col: pl.BlockSpec((tm, 2 * LANES), lambda i: (i, col))
    return pl.pallas_call(
        _mla_up_kernel,
        grid=(m // tm,),
        in_specs=[
            half(SLAB_CQ // 2), half(SLAB_CQ // 2 + 1), half(SLAB_CKV // 2), half(SLAB_CKV // 2 + 1),
            pl.BlockSpec((tm, LANES), lambda i: (i, SLAB_CKR)),
            full((1, 512)), full((1, 512)), full((512, 1024)), full((512, 1024)),
            tab, tab, tab,
        ],
        out_specs=[
            pl.BlockSpec((tm, 1024), lambda i: (i, 0)),
            pl.BlockSpec((tm, 1024), lambda i: (i, 0)),
            pl.BlockSpec((tm, 512), lambda i: (i, 0)),
        ],
        out_shape=[
            jax.ShapeDtypeStruct((m, 1024), jnp.bfloat16),
            jax.ShapeDtypeStruct((m, 1024), jnp.bfloat16),
            jax.ShapeDtypeStruct((m, 512), jnp.bfloat16),
        ],
        compiler_params=_cparams(("parallel",)),
        name="mla_up",
    )(proj, proj, proj, proj, proj, qn_w.reshape(1, 512), kvn_w.reshape(1, 512), w_uq_p, w_ukv_r, cos, sa, sb)


def _swa_kernel(sink_ref, q_ref, kc_ref, kp_ref, vc_ref, vp_ref, g0_ref, g1_ref, g2_ref, g3_ref, o_ref, *, blocks):
    w = SW_WINDOW
    gate_refs = (g0_ref, g1_ref, g2_ref, g3_ref)
    first = pl.program_id(1) == 0
    lane = lax.broadcasted_iota(jnp.int32, ((blocks + 1) * w, LANES), 1)
    lo = lane < 64

    def halves(prev_ref, cur_ref):
        t = jnp.concatenate([prev_ref[...], cur_ref[...]], axis=0).astype(jnp.float32)
        r = pltpu.roll(t, 64, 1)
        out = []
        for g in range(2):
            a, b = (t, r) if g == 0 else (r, t)
            out.append((jnp.where(lo, a, 0.0).astype(jnp.bfloat16), jnp.where(lo, 0.0, b).astype(jnp.bfloat16)))
        return out

    k_halves = halves(kp_ref, kc_ref)
    v_halves = halves(vp_ref, vc_ref)

    qrow = lax.broadcasted_iota(jnp.int32, (w, 2 * w), 0)
    kcol = lax.broadcasted_iota(jnp.int32, (w, 2 * w), 1)
    rel = qrow + w - kcol
    in_band = (rel >= 0) & (rel < w)
    first_key = jnp.where(first, w, 0)

    for n in range(blocks):
        mask = (in_band & (kcol >= first_key)) if n == 0 else in_band
        for g in range(2):
            k_lo, k_hi = k_halves[g]
            v_lo, v_hi = v_halves[g]
            rows = slice(n * w, (n + 2) * w)
            kk = jnp.concatenate([k_lo[rows], k_hi[rows]], axis=0)
            vv = jnp.concatenate([v_lo[rows], v_hi[rows]], axis=0)
            qs = jnp.concatenate(
                [q_ref[n * w:(n + 1) * w, (4 * g + s) * LANES:(4 * g + s + 1) * LANES] for s in range(4)], axis=0)
            sc = lax.dot_general(qs, kk, _NT, preferred_element_type=jnp.float32)
            p_rows = []
            for s in range(4):
                p_par = []
                for e in range(2):
                    blk = sc[s * w:(s + 1) * w, e * 2 * w:(e + 1) * 2 * w]
                    blk = jnp.where(mask, blk, NEG_INF)
                    sink = sink_ref[8 * g + 2 * s + e]
                    mx = jnp.maximum(jnp.max(blk, axis=-1, keepdims=True), sink)
                    p = jnp.exp(blk - mx)
                    den = jnp.sum(p, axis=-1, keepdims=True) + jnp.exp(sink - mx)
                    p_par.append((p * (1.0 / den)).astype(jnp.bfloat16))
                p_rows.append(jnp.concatenate(p_par, axis=1))
            pm = jnp.concatenate(p_rows, axis=0)
            o = jnp.dot(pm, vv, preferred_element_type=jnp.float32)
            for s in range(4):
                slab = 4 * g + s
                gate = gate_refs[slab // 2][n * w:(n + 1) * w, (slab % 2) * LANES:(slab % 2 + 1) * LANES]
                gate = gate.astype(jnp.float32)
                o_ref[n * w:(n + 1) * w, (4 * g + s) * LANES:(4 * g + s + 1) * LANES] = (
                    o[s * w:(s + 1) * w] * gate).astype(o_ref.dtype)


def _swa(proj, sinks, batch, seq):
    w = SW_WINDOW
    blocks = 4
    rows = blocks * w
    steps = seq // rows
    nb = seq // w
    cur = lambda slab: pl.BlockSpec((rows, LANES), lambda b, i: (b * steps + i, slab))
    prev = lambda slab: pl.BlockSpec(
        (w, LANES), lambda b, i: (jnp.maximum(b * nb + i * blocks - 1, 0), slab))
    gate = lambda pair: pl.BlockSpec((rows, 2 * LANES), lambda b, i: (b * steps + i, SLAB_BG // 2 + pair))
    return pl.pallas_call(
        functools.partial(_swa_kernel, blocks=blocks),
        grid=(batch, steps),
        in_specs=[
            pl.BlockSpec(memory_space=pltpu.SMEM),
            pl.BlockSpec((rows, 1024), lambda b, i: (b * steps + i, SLAB_BQ // 8)),
            cur(SLAB_BK), prev(SLAB_BK), cur(SLAB_BV), prev(SLAB_BV),
            gate(0), gate(1), gate(2), gate(3),
        ],
        out_specs=pl.BlockSpec((rows, 1024), lambda b, i: (b * steps + i, 0)),
        out_shape=jax.ShapeDtypeStruct((batch * seq, 1024), jnp.bfloat16),
        compiler_params=_cparams(("parallel", "arbitrary")),
        name="swa",
    )(sinks.astype(jnp.float32), proj, proj, proj, proj, proj, proj, proj, proj, proj)


def _out_proj_kernel(ya_ref, yb_ref, yc_ref, wf_ref, x_ref, postw_ref, o_ref, w_ref):
    @pl.when(pl.program_id(0) == 0)
    def _():
        w_ref[...] = wf_ref[...].astype(w_ref.dtype)

    out = jnp.dot(ya_ref[...], w_ref[0:512, :], preferred_element_type=jnp.float32)
    out += jnp.dot(yb_ref[...], w_ref[512:1536, :], preferred_element_type=jnp.float32)
    out += jnp.dot(yc_ref[...], w_ref[1536:2048, :], preferred_element_type=jnp.float32)
    ms = jnp.mean(out * out, axis=-1, keepdims=True)
    o_ref[...] = x_ref[...] + out * lax.rsqrt(ms + NORM_EPS) * postw_ref[...]


def _out_proj(ya, yb, yc, w_out, layer, x2, post_w):
    m = x2.shape[0]
    tm = 512
    row = lambda width: pl.BlockSpec((tm, width), lambda i: (i, 0))
    return pl.pallas_call(
        _out_proj_kernel,
        grid=(m // tm,),
        in_specs=[
            row(512), row(1024), row(512),
            pl.BlockSpec((None, D_MODEL, D_MODEL), lambda i: (layer, 0, 0), pipeline_mode=pl.Buffered(1)),
            row(D_MODEL),
            pl.BlockSpec((1, D_MODEL), lambda i: (0, 0)),
        ],
        out_specs=row(D_MODEL),
        out_shape=jax.ShapeDtypeStruct((m, D_MODEL), jnp.float32),
        scratch_shapes=[pltpu.VMEM((D_MODEL, D_MODEL), jnp.bfloat16)],
        compiler_params=_cparams(("arbitrary",)),
        name="out_proj",
    )(ya, yb, yc, w_out, x2, post_w.reshape(1, D_MODEL))


def kernel(x, positions, pre_norm_w, post_norm_w, w_in, diff_lambda_q1, diff_lambda_k1, diff_lambda_q2,
           diff_lambda_k2, diff_subln_w, sink_logits, mla_q_norm_w, mla_kv_norm_w, w_uq, w_ukv, w_out):
    batch, seq, d = x.shape
    depth = w_in.shape[0]
    cos, sa, sb = _rope_tables(positions)
    x2 = x.reshape(batch * seq, d)
    for layer in range(depth):
        lam_init = 0.8 - 0.6 * math.exp(-0.3 * layer)
        lam_rows = jnp.stack([diff_lambda_q1[layer], diff_lambda_k1[layer],
                              diff_lambda_q2[layer], diff_lambda_k2[layer]]).astype(jnp.float32)
        proj = _in_proj(x2, pre_norm_w[layer], w_in, layer, cos, sa, sb)
        ya = _diff_attn(proj, lam_rows, diff_subln_w[layer], batch, seq, lam_init)
        yb = _swa(proj, sink_logits[layer], batch, seq)
        qf, kf, vc = _mla_up(proj, mla_q_norm_w[layer], mla_kv_norm_w[layer],
                             _pad_w_uq(w_uq[layer]), _reorder_w_ukv(w_ukv[layer]), cos, sa, sb)
        yc = _mla_attn(qf, kf, vc, proj, batch, seq)
        x2 = _out_proj(ya, yb, yc, w_out, layer, x2, post_norm_w[layer])
    return x2.reshape(batch, seq, d)
```

```python
import functools
import math

import jax
import jax.numpy as jnp
import numpy as np
from jax import lax
from jax.experimental import pallas as pl
from jax.experimental.pallas import tpu as pltpu

D_MODEL = 2048
ROPE_THETA = 10000.0
NORM_EPS = 1e-6
NEG_INF = -1e30
LOG2E = math.log2(math.e)
DA_SUBLN_EPS = 1e-5
SW_WINDOW = 128
LANES = 128

IN_TILE = 1024
D_IN = 5952
N_SLABS = 48
SLAB_AQ, SLAB_AK, SLAB_AV, SLAB_AG = 0, 4, 8, 12
SLAB_BQ, SLAB_BK, SLAB_BV, SLAB_BG = 16, 24, 25, 26
SLAB_CQ, SLAB_CKV, SLAB_CKR, SLAB_CG = 34, 38, 42, 43

VMEM_LIMIT = 56 * 1024 * 1024

_NT = (((1,), (1,)), ((), ()))


def _cparams(sem):
    return pltpu.CompilerParams(dimension_semantics=sem, vmem_limit_bytes=VMEM_LIMIT)


def _rope_table_kernel(pos_ref, freq_ref, cos_ref, sa_ref, sb_ref):
    ang = pos_ref[...].astype(jnp.float32) * freq_ref[...]
    c, s = jnp.cos(ang), jnp.sin(ang)
    lane = lax.broadcasted_iota(jnp.int32, ang.shape, 1)
    first_half = (lane % 64) < 32
    cos_ref[...] = c
    sa_ref[...] = jnp.where(first_half, -s, 0.0)
    sb_ref[...] = jnp.where(first_half, 0.0, s)


def _rope_tables(positions):
    m = positions.size
    pos_b = jnp.broadcast_to(positions.reshape(m, 1), (m, LANES))
    inv_freq = jnp.power(ROPE_THETA, -jnp.arange(0, 64, 2, dtype=jnp.float32) / 64)
    freq = jnp.tile(inv_freq, 4).reshape(1, LANES)
    tm = 1024
    spec = pl.BlockSpec((tm, LANES), lambda i: (i, 0))
    return pl.pallas_call(
        _rope_table_kernel,
        grid=(m // tm,),
        in_specs=[spec, pl.BlockSpec((1, LANES), lambda i: (0, 0))],
        out_specs=[spec, spec, spec],
        out_shape=[jax.ShapeDtypeStruct((m, LANES), jnp.float32)] * 3,
        compiler_params=_cparams(("parallel",)),
        name="rope_tables",
    )(pos_b, freq)


def _rope(a, cos, sa, sb):
    return a * cos + pltpu.roll(a, 96, 1) * sa + pltpu.roll(a, 32, 1) * sb


def _silu(a):
    return a * (1.0 / (1.0 + jnp.exp(-a)))


_IN_KINDS = (
    ("rope_q2",) * 4 + ("rope",) * 4,
    ("plain",) * 4 + ("silu",) * 4,
    ("rope_q",) * 8,
    ("rope", "plain") + ("silu",) * 6,
    ("silu",) * 2 + ("plain",) * 6,
    ("plain",) * 2 + ("kr",) + ("gate_c",) * 4 + ("skip",),
)


def _in_proj_kernel(x_ref, prew_ref, w_ref, cos_ref, sa_ref, sb_ref, o_ref, h_ref):
    j = pl.program_id(1)

    @pl.when(j == 0)
    def _():
        x = x_ref[...]
        ms = jnp.mean(x * x, axis=-1, keepdims=True)
        h_ref[...] = (x * lax.rsqrt(ms + NORM_EPS) * prew_ref[...]).astype(jnp.bfloat16)

    def epilogue(kinds):
        acc = lax.dot_general(h_ref[...], w_ref[...].astype(jnp.bfloat16), _NT, preferred_element_type=jnp.float32)
        slab = lambda s: acc[:, s * LANES:(s + 1) * LANES]
        lane = lax.broadcasted_iota(jnp.int32, (acc.shape[0], LANES), 1)
        for s, kind in enumerate(kinds):
            a = slab(s)
            if kind == "skip":
                a = jnp.zeros_like(a)
            if kind in ("rope", "rope_q", "rope_q2", "kr"):
                a = _rope(a, cos_ref[...], sa_ref[...], sb_ref[...])
                if kind == "rope_q":
                    a = a * 0.125
                elif kind == "rope_q2":
                    a = a * (0.125 * LOG2E)
            elif kind == "silu":
                a = _silu(a)
            elif kind == "gate_c":
                a = jnp.where(lane < 64, pltpu.roll(_silu(slab(s - 1)), 64, 1), pltpu.roll(_silu(a), 64, 1))
            o_ref[:, s * LANES:(s + 1) * LANES] = a.astype(o_ref.dtype)

    for t, kinds in enumerate(_IN_KINDS):
        pl.when(j == t)(functools.partial(epilogue, kinds))


def _in_proj(x2, pre_w, w_in, layer, cos, sa, sb):
    m = x2.shape[0]
    tm, tn = 1024, IN_TILE
    tab = pl.BlockSpec((tm, LANES), lambda i, j: (i, 0))
    return pl.pallas_call(
        _in_proj_kernel,
        grid=(m // tm, N_SLABS * LANES // tn),
        in_specs=[
            pl.BlockSpec((tm, D_MODEL), lambda i, j: (i, 0)),
            pl.BlockSpec((1, D_MODEL), lambda i, j: (0, 0)),
            pl.BlockSpec((None, tn, D_MODEL), lambda i, j: (layer, j, 0)),
            tab, tab, tab,
        ],
        out_specs=pl.BlockSpec((tm, tn), lambda i, j: (i, j)),
        out_shape=jax.ShapeDtypeStruct((m, N_SLABS * LANES), jnp.bfloat16),
        scratch_shapes=[pltpu.VMEM((tm, D_MODEL), jnp.bfloat16)],
        compiler_params=_cparams(("parallel", "arbitrary")),
        name="in_proj",
    )(x2, pre_w.reshape(1, D_MODEL), jnp.swapaxes(w_in, 1, 2), cos, sa, sb)


ATTN_TQ = 256


ONES_ROWS = 16


def _transpose_v(v_ref, vt_ref):
    dv = v_ref.shape[1]
    vt_ref[0:dv, :] = v_ref[...].astype(jnp.float32).T.astype(vt_ref.dtype)
    vt_ref[dv:, :] = jnp.ones((ONES_ROWS, vt_ref.shape[1]), vt_ref.dtype)


def _causal_attention(items, k_ref, vt_ref, s_ref, tq, emit):
    key = lax.broadcasted_iota(jnp.int32, (tq, tq), 0)
    qry = lax.broadcasted_iota(jnp.int32, (tq, tq), 1)
    on_or_below_diag = key <= qry

    def phase_a(n, c, m_run):
        load_q, i = items[n]
        s = lax.dot_general(k_ref[c * tq:(c + 1) * tq, :], load_q(), _NT, preferred_element_type=jnp.float32)
        if c == i:
            s = jnp.where(on_or_below_diag, s, NEG_INF)
        s_ref[n % 2, c * tq:(c + 1) * tq, :] = s
        part = jnp.max(s.reshape(tq // 8, 8, tq), axis=0)
        return part if m_run is None else jnp.maximum(m_run, part)

    def phase_b(n, c, m8, acc):
        s = s_ref[n % 2, c * tq:(c + 1) * tq, :]
        p = jnp.exp2(s.reshape(tq // 8, 8, tq) - m8[None]).reshape(tq, tq).astype(jnp.bfloat16)
        d = jnp.dot(vt_ref[:, c * tq:(c + 1) * tq], p, preferred_element_type=jnp.float32)
        return d if acc is None else acc + d

    m8_prev = None
    for n in range(len(items) + 1):
        chunks_a = items[n][1] + 1 if n < len(items) else 0
        chunks_b = items[n - 1][1] + 1 if n > 0 else 0
        m_run, acc = None, None
        for c in range(max(chunks_a, chunks_b)):
            if c < chunks_b:
                acc = phase_b(n - 1, c, m8_prev, acc)
            if c < chunks_a:
                m_run = phase_a(n, c, m_run)
        if chunks_b:
            emit(n - 1, acc)
        if chunks_a:
            m8_prev = jnp.broadcast_to(jnp.max(m_run, axis=0, keepdims=True), (8, tq))


def _attn_scratch(seq):
    return [pltpu.VMEM((LANES + ONES_ROWS, seq), jnp.bfloat16),
            pltpu.VMEM((2, seq, ATTN_TQ), jnp.float32)]


def _diff_attn_kernel(q_ref, k_ref, v_ref, g_ref, lam_ref, subw_ref, o_ref, vt_ref, s_ref, *, tq, lam_init):
    _transpose_v(v_ref, vt_ref)
    dv = v_ref.shape[1]
    t = lam_ref[...]
    lam = (jnp.exp(jnp.sum(t[0:1] * t[1:2], axis=-1, keepdims=True))
           - jnp.exp(jnp.sum(t[2:3] * t[3:4], axis=-1, keepdims=True)) + lam_init)
    lane = lax.broadcasted_iota(jnp.int32, (tq, LANES), 1)

    def component(i, second):
        def load_q():
            q = q_ref[i * tq:(i + 1) * tq, :]
            keep = (lane >= 64) if second else (lane < 64)
            return jnp.where(keep, q, jnp.zeros_like(q))
        return load_q

    items = [(component(i, second), i) for i in range(q_ref.shape[0] // tq) for second in (False, True)]
    first = {}

    def emit(n, acc):
        o_n = acc[0:dv] * (1.0 / acc[dv:dv + 1])
        if n % 2 == 0:
            first[n // 2] = o_n
            return
        i = n // 2
        rows = slice(i * tq, (i + 1) * tq)
        o = first.pop(i) - lam * o_n
        ms = jnp.mean(o * o, axis=0, keepdims=True)
        o = (o * lax.rsqrt(ms + DA_SUBLN_EPS)).T
        o = o * subw_ref[...] * (1.0 - lam_init)
        o_ref[rows, :] = (o * g_ref[rows, :].astype(jnp.float32)).astype(o_ref.dtype)

    _causal_attention(items, k_ref, vt_ref, s_ref, tq, emit)


def _diff_attn(proj, lam_rows, subln_w, batch, seq, lam_init):
    heads = 4
    slab = lambda first: pl.BlockSpec((seq, LANES), lambda b, h: (b, first + h))
    return pl.pallas_call(
        functools.partial(_diff_attn_kernel, tq=ATTN_TQ, lam_init=lam_init),
        grid=(batch, heads),
        in_specs=[
            slab(SLAB_AQ), slab(SLAB_AK), slab(SLAB_AV), slab(SLAB_AG),
            pl.BlockSpec((4, 64), lambda b, h: (0, 0)),
            pl.BlockSpec((1, LANES), lambda b, h: (0, 0)),
        ],
        out_specs=pl.BlockSpec((seq, LANES), lambda b, h: (b, h)),
        out_shape=jax.ShapeDtypeStruct((batch * seq, heads * LANES), jnp.bfloat16),
        scratch_shapes=_attn_scratch(seq),
        compiler_params=_cparams(("parallel", "parallel")),
        name="diff_attn",
    )(proj, proj, proj, proj, lam_rows, subln_w.reshape(1, LANES))


def _mla_attn_kernel(q_ref, k_ref, v_ref, g_ref, o_ref, vt_ref, s_ref, *, tq):
    _transpose_v(v_ref, vt_ref)
    dv = v_ref.shape[1]
    items = [(functools.partial(lambda i: q_ref[i * tq:(i + 1) * tq, :], i), i)
             for i in range(q_ref.shape[0] // tq)]

    def emit(i, acc):
        rows = slice(i * tq, (i + 1) * tq)
        o = (acc[0:dv] * (1.0 / acc[dv:dv + 1])).T
        o_ref[rows, :] = (o * g_ref[rows, :].astype(jnp.float32)).astype(o_ref.dtype)

    _causal_attention(items, k_ref, vt_ref, s_ref, tq, emit)


def _mla_attn(qf, kf, vc, proj, batch, seq):
    heads = 4
    return pl.pallas_call(
        functools.partial(_mla_attn_kernel, tq=ATTN_TQ),
        grid=(batch, heads),
        in_specs=[
            pl.BlockSpec((seq, 2 * LANES), lambda b, h: (b, h)),
            pl.BlockSpec((seq, 2 * LANES), lambda b, h: (b, h)),
            pl.BlockSpec((seq, LANES), lambda b, h: (b, h)),
            pl.BlockSpec((seq, LANES), lambda b, h: (b, SLAB_CG + h)),
        ],
        out_specs=pl.BlockSpec((seq, LANES), lambda b, h: (b, h)),
        out_shape=jax.ShapeDtypeStruct((batch * seq, heads * LANES), jnp.bfloat16),
        scratch_shapes=_attn_scratch(seq),
        compiler_params=_cparams(("parallel", "parallel")),
        name="mla_attn",
    )(qf, kf, vc, proj)


def _pad_w_uq(w):
    z = jnp.zeros((w.shape[0], 64), w.dtype)
    pieces = []
    for h in range(4):
        pieces += [w[:, h * 192:(h + 1) * 192], z]
    return jnp.concatenate(pieces, axis=1).astype(jnp.bfloat16)


def _reorder_w_ukv(w):
    k = [w[:, h * 256:h * 256 + 128] for h in range(4)]
    v = [w[:, h * 256 + 128:(h + 1) * 256] for h in range(4)]
    return jnp.concatenate(k + v, axis=1).astype(jnp.bfloat16)


def _mla_up_kernel(cq0_ref, cq1_ref, ckv0_ref, ckv1_ref, ckr_ref, qnw_ref, kvnw_ref, wuq_ref, wukv_ref,
                   cos_ref, sa_ref, sb_ref, qf_ref, kf_ref, vc_ref):
    def norm(ref0, ref1, w_ref):
        c = jnp.concatenate([ref0[...], ref1[...]], axis=1).astype(jnp.float32)
        ms = jnp.mean(c * c, axis=-1, keepdims=True)
        return (c * lax.rsqrt(ms + NORM_EPS) * w_ref[...]).astype(jnp.bfloat16)

    scale = 192 ** -0.5 * LOG2E
    q = jnp.dot(norm(cq0_ref, cq1_ref, qnw_ref), wuq_ref[...], preferred_element_type=jnp.float32)
    kv = jnp.dot(norm(ckv0_ref, ckv1_ref, kvnw_ref), wukv_ref[...], preferred_element_type=jnp.float32)
    kr = ckr_ref[...]
    for h in range(4):
        nope = q[:, 2 * h * LANES:(2 * h + 1) * LANES]
        rope = _rope(q[:, (2 * h + 1) * LANES:(2 * h + 2) * LANES], cos_ref[...], sa_ref[...], sb_ref[...])
        qf_ref[:, 2 * h * LANES:(2 * h + 1) * LANES] = (nope * scale).astype(qf_ref.dtype)
        qf_ref[:, (2 * h + 1) * LANES:(2 * h + 2) * LANES] = (rope * scale).astype(qf_ref.dtype)
        kf_ref[:, 2 * h * LANES:(2 * h + 1) * LANES] = kv[:, h * LANES:(h + 1) * LANES].astype(kf_ref.dtype)
        kf_ref[:, (2 * h + 1) * LANES:(2 * h + 2) * LANES] = kr
    vc_ref[...] = kv[:, 4 * LANES:].astype(vc_ref.dtype)


def _mla_up(proj, qn_w, kvn_w, w_uq_p, w_ukv_r, cos, sa, sb):
    m = proj.shape[0]
    tm = 512
    tab = pl.BlockSpec((tm, LANES), lambda i: (i, 0))
    full = lambda shape: pl.BlockSpec(shape, lambda i: (0, 0))
    half = lambda col: pl.BlockSpec((tm, 2 * LANES), lambda i: (i, col))
    return pl.pallas_call(
        _mla_up_kernel,
        grid=(m // tm,),
        in_specs=[
            half(SLAB_CQ // 2), half(SLAB_CQ // 2 + 1), half(SLAB_CKV // 2), half(SLAB_CKV // 2 + 1),
            pl.BlockSpec((tm, LANES), lambda i: (i, SLAB_CKR)),
            full((1, 512)), full((1, 512)), full((512, 1024)), full((512, 1024)),
            tab, tab, tab,
        ],
        out_specs=[
            pl.BlockSpec((tm, 1024), lambda i: (i, 0)),
            pl.BlockSpec((tm, 1024), lambda i: (i, 0)),
            pl.BlockSpec((tm, 512), lambda i: (i, 0)),
        ],
        out_shape=[
            jax.ShapeDtypeStruct((m, 1024), jnp.bfloat16),
            jax.ShapeDtypeStruct((m, 1024), jnp.bfloat16),
            jax.ShapeDtypeStruct((m, 512), jnp.bfloat16),
        ],
        compiler_params=_cparams(("parallel",)),
        name="mla_up",
    )(proj, proj, proj, proj, proj, qn_w.reshape(1, 512), kvn_w.reshape(1, 512), w_uq_p, w_ukv_r, cos, sa, sb)


def _swa_kernel(sink_ref, q_ref, kc_ref, kp_ref, vc_ref, vp_ref, g0_ref, g1_ref, g2_ref, g3_ref, o_ref, *, blocks):
    w = SW_WINDOW
    gate_refs = (g0_ref, g1_ref, g2_ref, g3_ref)
    first = pl.program_id(1) == 0
    lane = lax.broadcasted_iota(jnp.int32, ((blocks + 1) * w, LANES), 1)
    lo = lane < 64

    def halves(prev_ref, cur_ref):
        t = jnp.concatenate([prev_ref[...], cur_ref[...]], axis=0).astype(jnp.float32)
        r = pltpu.roll(t, 64, 1)
        out = []
        for g in range(2):
            a, b = (t, r) if g == 0 else (r, t)
            out.append((jnp.where(lo, a, 0.0).astype(jnp.bfloat16), jnp.where(lo, 0.0, b).astype(jnp.bfloat16)))
        return out

    k_halves = halves(kp_ref, kc_ref)
    v_halves = halves(vp_ref, vc_ref)

    qrow = lax.broadcasted_iota(jnp.int32, (w, 2 * w), 0)
    kcol = lax.broadcasted_iota(jnp.int32, (w, 2 * w), 1)
    rel = qrow + w - kcol
    in_band = (rel >= 0) & (rel < w)
    first_key = jnp.where(first, w, 0)

    for n in range(blocks):
        mask = (in_band & (kcol >= first_key)) if n == 0 else in_band
        for g in range(2):
            k_lo, k_hi = k_halves[g]
            v_lo, v_hi = v_halves[g]
            rows = slice(n * w, (n + 2) * w)
            kk = jnp.concatenate([k_lo[rows], k_hi[rows]], axis=0)
            vv = jnp.concatenate([v_lo[rows], v_hi[rows]], axis=0)
            qs = jnp.concatenate(
                [q_ref[n * w:(n + 1) * w, (4 * g + s) * LANES:(4 * g + s + 1) * LANES] for s in range(4)], axis=0)
            sc = lax.dot_general(qs, kk, _NT, preferred_element_type=jnp.float32)
            p_rows = []
            for s in range(4):
                p_par = []
                for e in range(2):
                    blk = sc[s * w:(s + 1) * w, e * 2 * w:(e + 1) * 2 * w]
                    blk = jnp.where(mask, blk, NEG_INF)
                    sink = sink_ref[8 * g + 2 * s + e]
                    mx = jnp.maximum(jnp.max(blk, axis=-1, keepdims=True), sink)
                    p = jnp.exp(blk - mx)
                    den = jnp.sum(p, axis=-1, keepdims=True) + jnp.exp(sink - mx)
                    p_par.append((p * (1.0 / den)).astype(jnp.bfloat16))
                p_rows.append(jnp.concatenate(p_par, axis=1))
            pm = jnp.concatenate(p_rows, axis=0)
            o = jnp.dot(pm, vv, preferred_element_type=jnp.float32)
            for s in range(4):
                slab = 4 * g + s
                gate = gate_refs[slab // 2][n * w:(n + 1) * w, (slab % 2) * LANES:(slab % 2 + 1) * LANES]
                gate = gate.astype(jnp.float32)
                o_ref[n * w:(n + 1) * w, (4 * g + s) * LANES:(4 * g + s + 1) * LANES] = (
                    o[s * w:(s + 1) * w] * gate).astype(o_ref.dtype)


def _swa(proj, sinks, batch, seq):
    w = SW_WINDOW
    blocks = 4
    rows = blocks * w
    steps = seq // rows
    nb = seq // w
    cur = lambda slab: pl.BlockSpec((rows, LANES), lambda b, i: (b * steps + i, slab))
    prev = lambda slab: pl.BlockSpec(
        (w, LANES), lambda b, i: (jnp.maximum(b * nb + i * blocks - 1, 0), slab))
    gate = lambda pair: pl.BlockSpec((rows, 2 * LANES), lambda b, i: (b * steps + i, SLAB_BG // 2 + pair))
    return pl.pallas_call(
        functools.partial(_swa_kernel, blocks=blocks),
        grid=(batch, steps),
        in_specs=[
            pl.BlockSpec(memory_space=pltpu.SMEM),
            pl.BlockSpec((rows, 1024), lambda b, i: (b * steps + i, SLAB_BQ // 8)),
            cur(SLAB_BK), prev(SLAB_BK), cur(SLAB_BV), prev(SLAB_BV),
            gate(0), gate(1), gate(2), gate(3),
        ],
        out_specs=pl.BlockSpec((rows, 1024), lambda b, i: (b * steps + i, 0)),
        out_shape=jax.ShapeDtypeStruct((batch * seq, 1024), jnp.bfloat16),
        compiler_params=_cparams(("parallel", "arbitrary")),
        name="swa",
    )(sinks.astype(jnp.float32), proj, proj, proj, proj, proj, proj, proj, proj, proj)


def _out_proj_kernel(ya_ref, yb_ref, yc_ref, wf_ref, x_ref, postw_ref, o_ref, w_ref):
    @pl.when(pl.program_id(0) == 0)
    def _():
        w_ref[...] = wf_ref[...].astype(w_ref.dtype)

    out = jnp.dot(ya_ref[...], w_ref[0:512, :], preferred_element_type=jnp.float32)
    out += jnp.dot(yb_ref[...], w_ref[512:1536, :], preferred_element_type=jnp.float32)
    out += jnp.dot(yc_ref[...], w_ref[1536:2048, :], preferred_element_type=jnp.float32)
    ms = jnp.mean(out * out, axis=-1, keepdims=True)
    o_ref[...] = x_ref[...] + out * lax.rsqrt(ms + NORM_EPS) * postw_ref[...]


def _out_proj(ya, yb, yc, w_out, layer, x2, post_w):
    m = x2.shape[0]
    tm = 512
    row = lambda width: pl.BlockSpec((tm, width), lambda i: (i, 0))
    return pl.pallas_call(
        _out_proj_kernel,
        grid=(m // tm,),
        in_specs=[
            row(512), row(1024), row(512),
            pl.BlockSpec((None, D_MODEL, D_MODEL), lambda i: (layer, 0, 0), pipeline_mode=pl.Buffered(1)),
            row(D_MODEL),
            pl.BlockSpec((1, D_MODEL), lambda i: (0, 0)),
        ],
        out_specs=row(D_MODEL),
        out_shape=jax.ShapeDtypeStruct((m, D_MODEL), jnp.float32),
        scratch_shapes=[pltpu.VMEM((D_MODEL, D_MODEL), jnp.bfloat16)],
        compiler_params=_cparams(("arbitrary",)),
        name="out_proj",
    )(ya, yb, yc, w_out, x2, post_w.reshape(1, D_MODEL))


def kernel(x, positions, pre_norm_w, post_norm_w, w_in, diff_lambda_q1, diff_lambda_k1, diff_lambda_q2,
           diff_lambda_k2, diff_subln_w, sink_logits, mla_q_norm_w, mla_kv_norm_w, w_uq, w_ukv, w_out):
    batch, seq, d = x.shape
    depth = w_in.shape[0]
    cos, sa, sb = _rope_tables(positions)
    x2 = x.reshape(batch * seq, d)
    for layer in range(depth):
        lam_init = 0.8 - 0.6 * math.exp(-0.3 * layer)
        lam_rows = jnp.stack([diff_lambda_q1[layer], diff_lambda_k1[layer],
                              diff_lambda_q2[layer], diff_lambda_k2[layer]]).astype(jnp.float32)
        proj = _in_proj(x2, pre_norm_w[layer], w_in, layer, cos, sa, sb)
        ya = _diff_attn(proj, lam_rows, diff_subln_w[layer], batch, seq, lam_init)
        yb = _swa(proj, sink_logits[layer], batch, seq)
        qf, kf, vc = _mla_up(proj, mla_q_norm_w[layer], mla_kv_norm_w[layer],
                             _pad_w_uq(w_uq[layer]), _reorder_w_ukv(w_ukv[layer]), cos, sa, sb)
        yc = _mla_attn(qf, kf, vc, proj, batch, seq)
        x2 = _out_proj(ya, yb, yc, w_out, layer, x2, post_norm_w[layer])
    return x2.reshape(batch, seq, d)
```

```python
import functools
import math

import jax
import jax.numpy as jnp
import numpy as np
from jax import lax
from jax.experimental import pallas as pl
from jax.experimental.pallas import tpu as pltpu

D_MODEL = 2048
ROPE_THETA = 10000.0
NORM_EPS = 1e-6
NEG_INF = -1e30
LOG2E = math.log2(math.e)
DA_SUBLN_EPS = 1e-5
SW_WINDOW = 128
LANES = 128

IN_TILE = 1024
D_IN = 5952
N_SLABS = 48
SLAB_AQ, SLAB_AK, SLAB_AV, SLAB_AG = 0, 4, 8, 12
SLAB_BQ, SLAB_BK, SLAB_BV, SLAB_BG = 16, 24, 25, 26
SLAB_CQ, SLAB_CKV, SLAB_CKR, SLAB_CG = 34, 38, 42, 43

VMEM_LIMIT = 56 * 1024 * 1024

_NT = (((1,), (1,)), ((), ()))


def _cparams(sem):
    return pltpu.CompilerParams(dimension_semantics=sem, vmem_limit_bytes=VMEM_LIMIT)


def _rope_table_kernel(pos_ref, freq_ref, cos_ref, sa_ref, sb_ref):
    ang = pos_ref[...].astype(jnp.float32) * freq_ref[...]
    c, s = jnp.cos(ang), jnp.sin(ang)
    lane = lax.broadcasted_iota(jnp.int32, ang.shape, 1)
    first_half = (lane % 64) < 32
    cos_ref[...] = c
    sa_ref[...] = jnp.where(first_half, -s, 0.0)
    sb_ref[...] = jnp.where(first_half, 0.0, s)


def _rope_tables(positions):
    m = positions.size
    pos_b = jnp.broadcast_to(positions.reshape(m, 1), (m, LANES))
    inv_freq = jnp.power(ROPE_THETA, -jnp.arange(0, 64, 2, dtype=jnp.float32) / 64)
    freq = jnp.tile(inv_freq, 4).reshape(1, LANES)
    tm = 1024
    spec = pl.BlockSpec((tm, LANES), lambda i: (i, 0))
    return pl.pallas_call(
        _rope_table_kernel,
        grid=(m // tm,),
        in_specs=[spec, pl.BlockSpec((1, LANES), lambda i: (0, 0))],
        out_specs=[spec, spec, spec],
        out_shape=[jax.ShapeDtypeStruct((m, LANES), jnp.float32)] * 3,
        compiler_params=_cparams(("parallel",)),
        name="rope_tables",
    )(pos_b, freq)


def _rope(a, cos, sa, sb):
    return a * cos + pltpu.roll(a, 96, 1) * sa + pltpu.roll(a, 32, 1) * sb


def _silu(a):
    return a * (1.0 / (1.0 + jnp.exp(-a)))


_IN_KINDS = (
    ("rope_q",) * 4 + ("rope",) * 4,
    ("plain",) * 4 + ("silu",) * 4,
    ("rope_q",) * 8,
    ("rope", "plain") + ("silu",) * 6,
    ("silu",) * 2 + ("plain",) * 6,
    ("plain",) * 2 + ("kr",) + ("gate_c",) * 4 + ("skip",),
)


def _in_proj_kernel(x_ref, prew_ref, w_ref, cos_ref, sa_ref, sb_ref, o_ref, h_ref):
    j = pl.program_id(1)

    @pl.when(j == 0)
    def _():
        x = x_ref[...]
        ms = jnp.mean(x * x, axis=-1, keepdims=True)
        h_ref[...] = (x * lax.rsqrt(ms + NORM_EPS) * prew_ref[...]).astype(jnp.bfloat16)

    def epilogue(kinds):
        acc = lax.dot_general(h_ref[...], w_ref[...].astype(jnp.bfloat16), _NT, preferred_element_type=jnp.float32)
        slab = lambda s: acc[:, s * LANES:(s + 1) * LANES]
        lane = lax.broadcasted_iota(jnp.int32, (acc.shape[0], LANES), 1)
        for s, kind in enumerate(kinds):
            a = slab(s)
            if kind == "skip":
                a = jnp.zeros_like(a)
            if kind in ("rope", "rope_q", "kr"):
                a = _rope(a, cos_ref[...], sa_ref[...], sb_ref[...])
                if kind == "rope_q":
                    a = a * (0.125 * LOG2E)
            elif kind == "silu":
                a = _silu(a)
            elif kind == "gate_c":
                a = jnp.where(lane < 64, pltpu.roll(_silu(slab(s - 1)), 64, 1), pltpu.roll(_silu(a), 64, 1))
            o_ref[:, s * LANES:(s + 1) * LANES] = a.astype(o_ref.dtype)

    for t, kinds in enumerate(_IN_KINDS):
        pl.when(j == t)(functools.partial(epilogue, kinds))


def _in_proj(x2, pre_w, w_in, layer, cos, sa, sb):
    m = x2.shape[0]
    tm, tn = 1024, IN_TILE
    tab = pl.BlockSpec((tm, LANES), lambda i, j: (i, 0))
    return pl.pallas_call(
        _in_proj_kernel,
        grid=(m // tm, N_SLABS * LANES // tn),
        in_specs=[
            pl.BlockSpec((tm, D_MODEL), lambda i, j: (i, 0)),
            pl.BlockSpec((1, D_MODEL), lambda i, j: (0, 0)),
            pl.BlockSpec((None, tn, D_MODEL), lambda i, j: (layer, j, 0)),
            tab, tab, tab,
        ],
        out_specs=pl.BlockSpec((tm, tn), lambda i, j: (i, j)),
        out_shape=jax.ShapeDtypeStruct((m, N_SLABS * LANES), jnp.bfloat16),
        scratch_shapes=[pltpu.VMEM((tm, D_MODEL), jnp.bfloat16)],
        compiler_params=_cparams(("parallel", "arbitrary")),
        name="in_proj",
    )(x2, pre_w.reshape(1, D_MODEL), jnp.swapaxes(w_in, 1, 2), cos, sa, sb)


ATTN_TQ = 256


ONES_ROWS = 16


def _transpose_v(v_ref, vt_ref):
    dv = v_ref.shape[1]
    vt_ref[0:dv, :] = v_ref[...].astype(jnp.float32).T.astype(vt_ref.dtype)
    vt_ref[dv:, :] = jnp.ones((ONES_ROWS, vt_ref.shape[1]), vt_ref.dtype)


def _causal_attention(items, k_ref, vt_ref, s_ref, tq, emit):
    key = lax.broadcasted_iota(jnp.int32, (tq, tq), 0)
    qry = lax.broadcasted_iota(jnp.int32, (tq, tq), 1)
    on_or_below_diag = key <= qry

    def phase_a(n, c, m_run):
        load_q, i = items[n]
        s = lax.dot_general(k_ref[c * tq:(c + 1) * tq, :], load_q(), _NT, preferred_element_type=jnp.float32)
        if c == i:
            s = jnp.where(on_or_below_diag, s, NEG_INF)
        s_ref[n % 2, c * tq:(c + 1) * tq, :] = s
        part = jnp.max(s.reshape(tq // 8, 8, tq), axis=0)
        return part if m_run is None else jnp.maximum(m_run, part)

    def phase_b(n, c, m8, acc):
        s = s_ref[n % 2, c * tq:(c + 1) * tq, :]
        p = jnp.exp2(s.reshape(tq // 8, 8, tq) - m8[None]).reshape(tq, tq).astype(jnp.bfloat16)
        d = jnp.dot(vt_ref[:, c * tq:(c + 1) * tq], p, preferred_element_type=jnp.float32)
        return d if acc is None else acc + d

    m8_prev = None
    for n in range(len(items) + 1):
        chunks_a = items[n][1] + 1 if n < len(items) else 0
        chunks_b = items[n - 1][1] + 1 if n > 0 else 0
        m_run, acc = None, None
        for c in range(max(chunks_a, chunks_b)):
            if c < chunks_b:
                acc = phase_b(n - 1, c, m8_prev, acc)
            if c < chunks_a:
                m_run = phase_a(n, c, m_run)
        if chunks_b:
            emit(n - 1, acc)
        if chunks_a:
            m8_prev = jnp.broadcast_to(jnp.max(m_run, axis=0, keepdims=True), (8, tq))


def _attn_scratch(seq):
    return [pltpu.VMEM((LANES + ONES_ROWS, seq), jnp.bfloat16),
            pltpu.VMEM((2, seq, ATTN_TQ), jnp.float32)]


def _diff_attn_kernel(q_ref, k_ref, v_ref, g_ref, lam_ref, subw_ref, o_ref, vt_ref, s_ref, *, tq, lam_init):
    _transpose_v(v_ref, vt_ref)
    dv = v_ref.shape[1]
    t = lam_ref[...]
    lam = (jnp.exp(jnp.sum(t[0:1] * t[1:2], axis=-1, keepdims=True))
           - jnp.exp(jnp.sum(t[2:3] * t[3:4], axis=-1, keepdims=True)) + lam_init)
    lane = lax.broadcasted_iota(jnp.int32, (tq, LANES), 1)

    def component(i, second):
        def load_q():
            q = q_ref[i * tq:(i + 1) * tq, :]
            keep = (lane >= 64) if second else (lane < 64)
            return jnp.where(keep, q, jnp.zeros_like(q))
        return load_q

    items = [(component(i, second), i) for i in range(q_ref.shape[0] // tq) for second in (False, True)]
    first = {}

    def emit(n, acc):
        o_n = acc[0:dv] * (1.0 / acc[dv:dv + 1])
        if n % 2 == 0:
            first[n // 2] = o_n
            return
        i = n // 2
        rows = slice(i * tq, (i + 1) * tq)
        o = first.pop(i) - lam * o_n
        ms = jnp.mean(o * o, axis=0, keepdims=True)
        o = (o * lax.rsqrt(ms + DA_SUBLN_EPS)).T
        o = o * subw_ref[...] * (1.0 - lam_init)
        o_ref[rows, :] = (o * g_ref[rows, :].astype(jnp.float32)).astype(o_ref.dtype)

    _causal_attention(items, k_ref, vt_ref, s_ref, tq, emit)


def _diff_attn(proj, lam_rows, subln_w, batch, seq, lam_init):
    heads = 4
    slab = lambda first: pl.BlockSpec((seq, LANES), lambda b, h: (b, first + h))
    return pl.pallas_call(
        functools.partial(_diff_attn_kernel, tq=ATTN_TQ, lam_init=lam_init),
        grid=(batch, heads),
        in_specs=[
            slab(SLAB_AQ), slab(SLAB_AK), slab(SLAB_AV), slab(SLAB_AG),
            pl.BlockSpec((4, 64), lambda b, h: (0, 0)),
            pl.BlockSpec((1, LANES), lambda b, h: (0, 0)),
        ],
        out_specs=pl.BlockSpec((seq, LANES), lambda b, h: (b, h)),
        out_shape=jax.ShapeDtypeStruct((batch * seq, heads * LANES), jnp.bfloat16),
        scratch_shapes=_attn_scratch(seq),
        compiler_params=_cparams(("parallel", "parallel")),
        name="diff_attn",
    )(proj, proj, proj, proj, lam_rows, subln_w.reshape(1, LANES))


def _mla_attn_kernel(q_ref, k_ref, v_ref, g_ref, o_ref, vt_ref, s_ref, *, tq):
    _transpose_v(v_ref, vt_ref)
    dv = v_ref.shape[1]
    items = [(functools.partial(lambda i: q_ref[i * tq:(i + 1) * tq, :], i), i)
             for i in range(q_ref.shape[0] // tq)]

    def emit(i, acc):
        rows = slice(i * tq, (i + 1) * tq)
        o = (acc[0:dv] * (1.0 / acc[dv:dv + 1])).T
        o_ref[rows, :] = (o * g_ref[rows, :].astype(jnp.float32)).astype(o_ref.dtype)

    _causal_attention(items, k_ref, vt_ref, s_ref, tq, emit)


def _mla_attn(qf, kf, vc, proj, batch, seq):
    heads = 4
    return pl.pallas_call(
        functools.partial(_mla_attn_kernel, tq=ATTN_TQ),
        grid=(batch, heads),
        in_specs=[
            pl.BlockSpec((seq, 2 * LANES), lambda b, h: (b, h)),
            pl.BlockSpec((seq, 2 * LANES), lambda b, h: (b, h)),
            pl.BlockSpec((seq, LANES), lambda b, h: (b, h)),
            pl.BlockSpec((seq, LANES), lambda b, h: (b, SLAB_CG + h)),
        ],
        out_specs=pl.BlockSpec((seq, LANES), lambda b, h: (b, h)),
        out_shape=jax.ShapeDtypeStruct((batch * seq, heads * LANES), jnp.bfloat16),
        scratch_shapes=_attn_scratch(seq),
        compiler_params=_cparams(("parallel", "parallel")),
        name="mla_attn",
    )(qf, kf, vc, proj)


def _pad_w_uq(w):
    z = jnp.zeros((w.shape[0], 64), w.dtype)
    pieces = []
    for h in range(4):
        pieces += [w[:, h * 192:(h + 1) * 192], z]
    return jnp.concatenate(pieces, axis=1).astype(jnp.bfloat16)


def _reorder_w_ukv(w):
    k = [w[:, h * 256:h * 256 + 128] for h in range(4)]
    v = [w[:, h * 256 + 128:(h + 1) * 256] for h in range(4)]
    return jnp.concatenate(k + v, axis=1).astype(jnp.bfloat16)


def _mla_up_kernel(cq0_ref, cq1_ref, ckv0_ref, ckv1_ref, ckr_ref, qnw_ref, kvnw_ref, wuq_ref, wukv_ref,
                   cos_ref, sa_ref, sb_ref, qf_ref, kf_ref, vc_ref):
    def norm(ref0, ref1, w_ref):
        c = jnp.concatenate([ref0[...], ref1[...]], axis=1).astype(jnp.float32)
        ms = jnp.mean(c * c, axis=-1, keepdims=True)
        return (c * lax.rsqrt(ms + NORM_EPS) * w_ref[...]).astype(jnp.bfloat16)

    scale = 192 ** -0.5 * LOG2E
    q = jnp.dot(norm(cq0_ref, cq1_ref, qnw_ref), wuq_ref[...], preferred_element_type=jnp.float32)
    kv = jnp.dot(norm(ckv0_ref, ckv1_ref, kvnw_ref), wukv_ref[...], preferred_element_type=jnp.float32)
    kr = ckr_ref[...]
    for h in range(4):
        nope = q[:, 2 * h * LANES:(2 * h + 1) * LANES]
        rope = _rope(q[:, (2 * h + 1) * LANES:(2 * h + 2) * LANES], cos_ref[...], sa_ref[...], sb_ref[...])
        qf_ref[:, 2 * h * LANES:(2 * h + 1) * LANES] = (nope * scale).astype(qf_ref.dtype)
        qf_ref[:, (2 * h + 1) * LANES:(2 * h + 2) * LANES] = (rope * scale).astype(qf_ref.dtype)
        kf_ref[:, 2 * h * LANES:(2 * h + 1) * LANES] = kv[:, h * LANES:(h + 1) * LANES].astype(kf_ref.dtype)
        kf_ref[:, (2 * h + 1) * LANES:(2 * h + 2) * LANES] = kr
    vc_ref[...] = kv[:, 4 * LANES:].astype(vc_ref.dtype)


def _mla_up(proj, qn_w, kvn_w, w_uq_p, w_ukv_r, cos, sa, sb):
    m = proj.shape[0]
    tm = 512
    tab = pl.BlockSpec((tm, LANES), lambda i: (i, 0))
    full = lambda shape: pl.BlockSpec(shape, lambda i: (0, 0))
    half = lambda col: pl.BlockSpec((tm, 2 * LANES), lambda i: (i, col))
    return pl.pallas_call(
        _mla_up_kernel,
        grid=(m // tm,),
        in_specs=[
            half(SLAB_CQ // 2), half(SLAB_CQ // 2 + 1), half(SLAB_CKV // 2), half(SLAB_CKV // 2 + 1),
            pl.BlockSpec((tm, LANES), lambda i: (i, SLAB_CKR)),
            full((1, 512)), full((1, 512)), full((512, 1024)), full((512, 1024)),
            tab, tab, tab,
        ],
        out_specs=[
            pl.BlockSpec((tm, 1024), lambda i: (i, 0)),
            pl.BlockSpec((tm, 1024), lambda i: (i, 0)),
            pl.BlockSpec((tm, 512), lambda i: (i, 0)),
        ],
        out_shape=[
            jax.ShapeDtypeStruct((m, 1024), jnp.bfloat16),
            jax.ShapeDtypeStruct((m, 1024), jnp.bfloat16),
            jax.ShapeDtypeStruct((m, 512), jnp.bfloat16),
        ],
        compiler_params=_cparams(("parallel",)),
        name="mla_up",
    )(proj, proj, proj, proj, proj, qn_w.reshape(1, 512), kvn_w.reshape(1, 512), w_uq_p, w_ukv_r, cos, sa, sb)


def _swa_kernel(sink_ref, q_ref, kc_ref, kp_ref, vc_ref, vp_ref, g0_ref, g1_ref, g2_ref, g3_ref, o_ref, *, blocks):
    w = SW_WINDOW
    gate_refs = (g0_ref, g1_ref, g2_ref, g3_ref)
    first = pl.program_id(1) == 0
    lane = lax.broadcasted_iota(jnp.int32, ((blocks + 1) * w, LANES), 1)
    lo = lane < 64

    kt = jnp.concatenate([kp_ref[...], kc_ref[...]], axis=0).astype(jnp.float32)
    kr = pltpu.roll(kt, 64, 1)
    k_halves = []
    for g in range(2):
        a, b = (kt, kr) if g == 0 else (kr, kt)
        k_halves.append((jnp.where(lo, a, 0.0).astype(jnp.bfloat16), jnp.where(lo, 0.0, b).astype(jnp.bfloat16)))

    vt = jnp.concatenate([vp_ref[...], vc_ref[...]], axis=0).astype(jnp.float32).T
    zeros = jnp.zeros((64, 2 * w), jnp.bfloat16)
    ones = jnp.ones((ONES_ROWS, 2 * w), jnp.bfloat16)
    no_ones = jnp.zeros((ONES_ROWS, 2 * w), jnp.bfloat16)

    key = lax.broadcasted_iota(jnp.int32, (2 * w, 4 * w), 0)
    qry = lax.broadcasted_iota(jnp.int32, (2 * w, 4 * w), 1) & (w - 1)
    rel = qry + w - key
    in_band = (rel >= 0) & (rel < w)
    first_key = jnp.where(first, w, 0)
    bias = jnp.where(in_band, 0.0, NEG_INF)
    bias_first = jnp.where(in_band & (key >= first_key), 0.0, NEG_INF)

    def scores(n, g):
        k_lo, k_hi = k_halves[g]
        rows = slice(n * w, (n + 2) * w)
        kk = jnp.concatenate([k_lo[rows], k_hi[rows]], axis=0)
        qs = jnp.concatenate(
            [q_ref[n * w:(n + 1) * w, (4 * g + s) * LANES:(4 * g + s + 1) * LANES] for s in range(4)], axis=0)
        st = lax.dot_general(kk, qs, _NT, preferred_element_type=jnp.float32)
        ps, sink_terms = [], []
        for e in range(2):
            blk = st[e * 2 * w:(e + 1) * 2 * w] + (bias_first if n == 0 else bias)
            sink = jnp.concatenate(
                [jnp.full((1, w), sink_ref[8 * g + 2 * s + e] * LOG2E, jnp.float32) for s in range(4)], axis=1)
            m = jnp.maximum(jnp.max(blk, axis=0, keepdims=True), sink)
            ps.append(jnp.exp2(blk - m).astype(jnp.bfloat16))
            sink_terms.append(jnp.exp2(sink - m))
        return jnp.concatenate(ps, axis=0), sink_terms

    def outputs(n, g, pt, sink_terms):
        vg = vt[64 * g:64 * (g + 1), n * w:(n + 2) * w].astype(jnp.bfloat16)
        lhs = jnp.concatenate([
            jnp.concatenate([vg, zeros], axis=1), jnp.concatenate([zeros, vg], axis=1),
            jnp.concatenate([ones, no_ones], axis=1), jnp.concatenate([no_ones, ones], axis=1)], axis=0)
        acc = jnp.dot(lhs, pt, preferred_element_type=jnp.float32)
        den_even = acc[128:129] + sink_terms[0]
        den_odd = acc[128 + ONES_ROWS:129 + ONES_ROWS] + sink_terms[1]
        ot = jnp.concatenate([acc[0:64] * (1.0 / den_even), acc[64:128] * (1.0 / den_odd)], axis=0)
        for s in range(4):
            slab = 4 * g + s
            gate = gate_refs[slab // 2][n * w:(n + 1) * w, (slab % 2) * LANES:(slab % 2 + 1) * LANES]
            o = ot[:, s * w:(s + 1) * w].T
            o_ref[n * w:(n + 1) * w, slab * LANES:(slab + 1) * LANES] = (
                o * gate.astype(jnp.float32)).astype(o_ref.dtype)

    pending = None
    for item in [(n, g) for n in range(blocks) for g in range(2)] + [None]:
        computed = scores(*item) if item is not None else None
        if pending is not None:
            outputs(*pending[0], *pending[1])
        pending = (item, computed)


def _swa(proj, sinks, batch, seq):
    w = SW_WINDOW
    blocks = 4
    rows = blocks * w
    steps = seq // rows
    nb = seq // w
    cur = lambda slab: pl.BlockSpec((rows, LANES), lambda b, i: (b * steps + i, slab))
    prev = lambda slab: pl.BlockSpec(
        (w, LANES), lambda b, i: (jnp.maximum(b * nb + i * blocks - 1, 0), slab))
    gate = lambda pair: pl.BlockSpec((rows, 2 * LANES), lambda b, i: (b * steps + i, SLAB_BG // 2 + pair))
    return pl.pallas_call(
        functools.partial(_swa_kernel, blocks=blocks),
        grid=(batch, steps),
        in_specs=[
            pl.BlockSpec(memory_space=pltpu.SMEM),
            pl.BlockSpec((rows, 1024), lambda b, i: (b * steps + i, SLAB_BQ // 8)),
            cur(SLAB_BK), prev(SLAB_BK), cur(SLAB_BV), prev(SLAB_BV),
            gate(0), gate(1), gate(2), gate(3),
        ],
        out_specs=pl.BlockSpec((rows, 1024), lambda b, i: (b * steps + i, 0)),
        out_shape=jax.ShapeDtypeStruct((batch * seq, 1024), jnp.bfloat16),
        compiler_params=_cparams(("parallel", "arbitrary")),
        name="swa",
    )(sinks.astype(jnp.float32), proj, proj, proj, proj, proj, proj, proj, proj, proj)


def _out_proj_kernel(ya_ref, yb_ref, yc_ref, wf_ref, x_ref, postw_ref, o_ref, w_ref):
    @pl.when(pl.program_id(0) == 0)
    def _():
        w_ref[...] = wf_ref[...].astype(w_ref.dtype)

    out = jnp.dot(ya_ref[...], w_ref[0:512, :], preferred_element_type=jnp.float32)
    out += jnp.dot(yb_ref[...], w_ref[512:1536, :], preferred_element_type=jnp.float32)
    out += jnp.dot(yc_ref[...], w_ref[1536:2048, :], preferred_element_type=jnp.float32)
    ms = jnp.mean(out * out, axis=-1, keepdims=True)
    o_ref[...] = x_ref[...] + out * lax.rsqrt(ms + NORM_EPS) * postw_ref[...]


def _out_proj(ya, yb, yc, w_out, layer, x2, post_w):
    m = x2.shape[0]
    tm = 512
    row = lambda width: pl.BlockSpec((tm, width), lambda i: (i, 0))
    return pl.pallas_call(
        _out_proj_kernel,
        grid=(m // tm,),
        in_specs=[
            row(512), row(1024), row(512),
            pl.BlockSpec((None, D_MODEL, D_MODEL), lambda i: (layer, 0, 0), pipeline_mode=pl.Buffered(1)),
            row(D_MODEL),
            pl.BlockSpec((1, D_MODEL), lambda i: (0, 0)),
        ],
        out_specs=row(D_MODEL),
        out_shape=jax.ShapeDtypeStruct((m, D_MODEL), jnp.float32),
        scratch_shapes=[pltpu.VMEM((D_MODEL, D_MODEL), jnp.bfloat16)],
        compiler_params=_cparams(("arbitrary",)),
        name="out_proj",
    )(ya, yb, yc, w_out, x2, post_w.reshape(1, D_MODEL))


def kernel(x, positions, pre_norm_w, post_norm_w, w_in, diff_lambda_q1, diff_lambda_k1, diff_lambda_q2,
           diff_lambda_k2, diff_subln_w, sink_logits, mla_q_norm_w, mla_kv_norm_w, w_uq, w_ukv, w_out):
    batch, seq, d = x.shape
    depth = w_in.shape[0]
    cos, sa, sb = _rope_tables(positions)
    x2 = x.reshape(batch * seq, d)
    for layer in range(depth):
        lam_init = 0.8 - 0.6 * math.exp(-0.3 * layer)
        lam_rows = jnp.stack([diff_lambda_q1[layer], diff_lambda_k1[layer],
                              diff_lambda_q2[layer], diff_lambda_k2[layer]]).astype(jnp.float32)
        proj = _in_proj(x2, pre_norm_w[layer], w_in, layer, cos, sa, sb)
        ya = _diff_attn(proj, lam_rows, diff_subln_w[layer], batch, seq, lam_init)
        yb = _swa(proj, sink_logits[layer], batch, seq)
        qf, kf, vc = _mla_up(proj, mla_q_norm_w[layer], mla_kv_norm_w[layer],
                             _pad_w_uq(w_uq[layer]), _reorder_w_ukv(w_ukv[layer]), cos, sa, sb)
        yc = _mla_attn(qf, kf, vc, proj, batch, seq)
        x2 = _out_proj(ya, yb, yc, w_out, layer, x2, post_norm_w[layer])
    return x2.reshape(batch, seq, d)
```

```python
import functools
import math

import jax
import jax.numpy as jnp
import numpy as np
from jax import lax
from jax.experimental import pallas as pl
from jax.experimental.pallas import tpu as pltpu

D_MODEL = 2048
ROPE_THETA = 10000.0
NORM_EPS = 1e-6
NEG_INF = -1e30
LOG2E = math.log2(math.e)
DA_SUBLN_EPS = 1e-5
SW_WINDOW = 128
LANES = 128

IN_TILE = 1024
D_IN = 5952
N_SLABS = 48
SLAB_AQ, SLAB_AK, SLAB_AV, SLAB_AG = 0, 4, 8, 12
SLAB_BQ, SLAB_BK, SLAB_BV, SLAB_BG = 16, 24, 25, 26
SLAB_CQ, SLAB_CKV, SLAB_CKR, SLAB_CG = 34, 38, 42, 43

VMEM_LIMIT = 56 * 1024 * 1024

_NT = (((1,), (1,)), ((), ()))


def _cparams(sem):
    return pltpu.CompilerParams(dimension_semantics=sem, vmem_limit_bytes=VMEM_LIMIT)


def _rope_table_kernel(pos_ref, freq_ref, cos_ref, sa_ref, sb_ref):
    ang = pos_ref[...].astype(jnp.float32) * freq_ref[...]
    c, s = jnp.cos(ang), jnp.sin(ang)
    lane = lax.broadcasted_iota(jnp.int32, ang.shape, 1)
    first_half = (lane % 64) < 32
    cos_ref[...] = c
    sa_ref[...] = jnp.where(first_half, -s, 0.0)
    sb_ref[...] = jnp.where(first_half, 0.0, s)


def _rope_tables(positions):
    m = positions.size
    pos_b = jnp.broadcast_to(positions.reshape(m, 1), (m, LANES))
    inv_freq = jnp.power(ROPE_THETA, -jnp.arange(0, 64, 2, dtype=jnp.float32) / 64)
    freq = jnp.tile(inv_freq, 4).reshape(1, LANES)
    tm = 1024
    spec = pl.BlockSpec((tm, LANES), lambda i: (i, 0))
    return pl.pallas_call(
        _rope_table_kernel,
        grid=(m // tm,),
        in_specs=[spec, pl.BlockSpec((1, LANES), lambda i: (0, 0))],
        out_specs=[spec, spec, spec],
        out_shape=[jax.ShapeDtypeStruct((m, LANES), jnp.float32)] * 3,
        compiler_params=_cparams(("parallel",)),
        name="rope_tables",
    )(pos_b, freq)


def _rope(a, cos, sa, sb):
    return a * cos + pltpu.roll(a, 96, 1) * sa + pltpu.roll(a, 32, 1) * sb


def _silu(a):
    return a * (1.0 / (1.0 + jnp.exp(-a)))


_IN_KINDS = (
    ("rope_q",) * 4 + ("rope",) * 4,
    ("plain",) * 4 + ("silu",) * 4,
    ("rope_q",) * 8,
    ("rope", "plain") + ("silu",) * 6,
    ("silu",) * 2 + ("plain",) * 6,
    ("plain",) * 2 + ("kr",) + ("gate_c",) * 4 + ("skip",),
)


def _in_proj_kernel(x_ref, prew_ref, w_ref, cos_ref, sa_ref, sb_ref, o_ref, h_ref):
    j = pl.program_id(1)

    @pl.when(j == 0)
    def _():
        x = x_ref[...]
        ms = jnp.mean(x * x, axis=-1, keepdims=True)
        h_ref[...] = (x * lax.rsqrt(ms + NORM_EPS) * prew_ref[...]).astype(jnp.bfloat16)

    def epilogue(kinds):
        acc = lax.dot_general(h_ref[...], w_ref[...].astype(jnp.bfloat16), _NT, preferred_element_type=jnp.float32)
        slab = lambda s: acc[:, s * LANES:(s + 1) * LANES]
        lane = lax.broadcasted_iota(jnp.int32, (acc.shape[0], LANES), 1)
        for s, kind in enumerate(kinds):
            a = slab(s)
            if kind == "skip":
                a = jnp.zeros_like(a)
            if kind in ("rope", "rope_q", "kr"):
                a = _rope(a, cos_ref[...], sa_ref[...], sb_ref[...])
                if kind == "rope_q":
                    a = a * (0.125 * LOG2E)
            elif kind == "silu":
                a = _silu(a)
            elif kind == "gate_c":
                a = jnp.where(lane < 64, pltpu.roll(_silu(slab(s - 1)), 64, 1), pltpu.roll(_silu(a), 64, 1))
            o_ref[:, s * LANES:(s + 1) * LANES] = a.astype(o_ref.dtype)

    for t, kinds in enumerate(_IN_KINDS):
        pl.when(j == t)(functools.partial(epilogue, kinds))


def _in_proj(x2, pre_w, w_in, layer, cos, sa, sb):
    m = x2.shape[0]
    tm, tn = 1024, IN_TILE
    tab = pl.BlockSpec((tm, LANES), lambda i, j: (i, 0))
    return pl.pallas_call(
        _in_proj_kernel,
        grid=(m // tm, N_SLABS * LANES // tn),
        in_specs=[
            pl.BlockSpec((tm, D_MODEL), lambda i, j: (i, 0)),
            pl.BlockSpec((1, D_MODEL), lambda i, j: (0, 0)),
            pl.BlockSpec((None, tn, D_MODEL), lambda i, j: (layer, j, 0)),
            tab, tab, tab,
        ],
        out_specs=pl.BlockSpec((tm, tn), lambda i, j: (i, j)),
        out_shape=jax.ShapeDtypeStruct((m, N_SLABS * LANES), jnp.bfloat16),
        scratch_shapes=[pltpu.VMEM((tm, D_MODEL), jnp.bfloat16)],
        compiler_params=_cparams(("parallel", "arbitrary")),
        name="in_proj",
    )(x2, pre_w.reshape(1, D_MODEL), jnp.swapaxes(w_in, 1, 2), cos, sa, sb)


ATTN_TQ = 256


ONES_ROWS = 16


def _transpose_v(v_ref, vt_ref):
    dv = v_ref.shape[1]
    vt_ref[0:dv, :] = v_ref[...].astype(jnp.float32).T.astype(vt_ref.dtype)
    vt_ref[dv:, :] = jnp.ones((ONES_ROWS, vt_ref.shape[1]), vt_ref.dtype)


ATTN_STREAMS = 4


def _tile_streams(n_tiles):
    up, down = list(range(0, n_tiles, 2)), list(range(n_tiles - 1, 0, -2))
    half = ATTN_STREAMS // 2
    return [tiles[r::half] for r in range(half) for tiles in (up, down)]


def _causal_attention(streams, k_ref, vt_ref, s_ref, tq, emit):
    key = lax.broadcasted_iota(jnp.int32, (tq, tq), 0)
    qry = lax.broadcasted_iota(jnp.int32, (tq, tq), 1)
    on_or_below_diag = key <= qry

    def phase_a(p, n, c, m_run):
        load_q, i, _ = streams[p][n]
        s = lax.dot_general(k_ref[c * tq:(c + 1) * tq, :], load_q(), _NT, preferred_element_type=jnp.float32)
        if c == i:
            s = jnp.where(on_or_below_diag, s, NEG_INF)
        s_ref[2 * p + n % 2, c * tq:(c + 1) * tq, :] = s
        part = jnp.max(s.reshape(tq // 8, 8, tq), axis=0)
        return part if m_run is None else jnp.maximum(m_run, part)

    def phase_b(p, n, c, m8, acc):
        s = s_ref[2 * p + n % 2, c * tq:(c + 1) * tq, :]
        pt = jnp.exp2(s.reshape(tq // 8, 8, tq) - m8[None]).reshape(tq, tq).astype(jnp.bfloat16)
        d = jnp.dot(vt_ref[:, c * tq:(c + 1) * tq], pt, preferred_element_type=jnp.float32)
        return d if acc is None else acc + d

    chunks = lambda p, n: streams[p][n][1] + 1 if 0 <= n < len(streams[p]) else 0
    m8_prev = [None] * len(streams)
    for n in range(max(len(st) for st in streams) + 1):
        m_run = [None] * len(streams)
        acc = [None] * len(streams)
        for c in range(max(max(chunks(p, n), chunks(p, n - 1)) for p in range(len(streams)))):
            for p in range(len(streams)):
                if c < chunks(p, n - 1):
                    acc[p] = phase_b(p, n - 1, c, m8_prev[p], acc[p])
                if c < chunks(p, n):
                    m_run[p] = phase_a(p, n, c, m_run[p])
        for p in range(len(streams)):
            if chunks(p, n - 1):
                emit(streams[p][n - 1][2], acc[p])
            if chunks(p, n):
                m8_prev[p] = jnp.broadcast_to(jnp.max(m_run[p], axis=0, keepdims=True), (8, tq))


def _attn_scratch(seq):
    return [pltpu.VMEM((LANES + ONES_ROWS, seq), jnp.bfloat16),
            pltpu.VMEM((2 * ATTN_STREAMS, seq, ATTN_TQ), jnp.float32)]


def _diff_attn_kernel(q_ref, k_ref, v_ref, g_ref, lam_ref, subw_ref, o_ref, vt_ref, s_ref, *, tq, lam_init):
    _transpose_v(v_ref, vt_ref)
    dv = v_ref.shape[1]
    t = lam_ref[...]
    lam = (jnp.exp(jnp.sum(t[0:1] * t[1:2], axis=-1, keepdims=True))
           - jnp.exp(jnp.sum(t[2:3] * t[3:4], axis=-1, keepdims=True)) + lam_init)
    lane = lax.broadcasted_iota(jnp.int32, (tq, LANES), 1)

    def component(i, second):
        def load_q():
            q = q_ref[i * tq:(i + 1) * tq, :]
            keep = (lane >= 64) if second else (lane < 64)
            return jnp.where(keep, q, jnp.zeros_like(q))
        return load_q

    streams = [[(component(i, second), i, (i, second)) for i in tiles for second in (False, True)]
               for tiles in _tile_streams(q_ref.shape[0] // tq)]
    first = {}

    def emit(tag, acc):
        i, second = tag
        o_n = acc[0:dv] * (1.0 / acc[dv:dv + 1])
        if not second:
            first[i] = o_n
            return
        rows = slice(i * tq, (i + 1) * tq)
        o = first.pop(i) - lam * o_n
        ms = jnp.mean(o * o, axis=0, keepdims=True)
        o = (o * lax.rsqrt(ms + DA_SUBLN_EPS)).T
        o = o * subw_ref[...] * (1.0 - lam_init)
        o_ref[rows, :] = (o * g_ref[rows, :].astype(jnp.float32)).astype(o_ref.dtype)

    _causal_attention(streams, k_ref, vt_ref, s_ref, tq, emit)


def _diff_attn(proj, lam_rows, subln_w, batch, seq, lam_init):
    heads = 4
    slab = lambda first: pl.BlockSpec((seq, LANES), lambda b, h: (b, first + h))
    return pl.pallas_call(
        functools.partial(_diff_attn_kernel, tq=ATTN_TQ, lam_init=lam_init),
        grid=(batch, heads),
        in_specs=[
            slab(SLAB_AQ), slab(SLAB_AK), slab(SLAB_AV), slab(SLAB_AG),
            pl.BlockSpec((4, 64), lambda b, h: (0, 0)),
            pl.BlockSpec((1, LANES), lambda b, h: (0, 0)),
        ],
        out_specs=pl.BlockSpec((seq, LANES), lambda b, h: (b, h)),
        out_shape=jax.ShapeDtypeStruct((batch * seq, heads * LANES), jnp.bfloat16),
        scratch_shapes=_attn_scratch(seq),
        compiler_params=_cparams(("parallel", "parallel")),
        name="diff_attn",
    )(proj, proj, proj, proj, lam_rows, subln_w.reshape(1, LANES))


def _mla_attn_kernel(q_ref, k_ref, v_ref, g_ref, o_ref, vt_ref, s_ref, *, tq):
    _transpose_v(v_ref, vt_ref)
    dv = v_ref.shape[1]
    streams = [[(functools.partial(lambda i: q_ref[i * tq:(i + 1) * tq, :], i), i, i) for i in tiles]
               for tiles in _tile_streams(q_ref.shape[0] // tq)]

    def emit(i, acc):
        rows = slice(i * tq, (i + 1) * tq)
        o = (acc[0:dv] * (1.0 / acc[dv:dv + 1])).T
        o_ref[rows, :] = (o * g_ref[rows, :].astype(jnp.float32)).astype(o_ref.dtype)

    _causal_attention(streams, k_ref, vt_ref, s_ref, tq, emit)


def _mla_attn(qf, kf, vc, proj, batch, seq):
    heads = 4
    return pl.pallas_call(
        functools.partial(_mla_attn_kernel, tq=ATTN_TQ),
        grid=(batch, heads),
        in_specs=[
            pl.BlockSpec((seq, 2 * LANES), lambda b, h: (b, h)),
            pl.BlockSpec((seq, 2 * LANES), lambda b, h: (b, h)),
            pl.BlockSpec((seq, LANES), lambda b, h: (b, h)),
            pl.BlockSpec((seq, LANES), lambda b, h: (b, SLAB_CG + h)),
        ],
        out_specs=pl.BlockSpec((seq, LANES), lambda b, h: (b, h)),
        out_shape=jax.ShapeDtypeStruct((batch * seq, heads * LANES), jnp.bfloat16),
        scratch_shapes=_attn_scratch(seq),
        compiler_params=_cparams(("parallel", "parallel")),
        name="mla_attn",
    )(qf, kf, vc, proj)


def _pad_w_uq(w):
    z = jnp.zeros((w.shape[0], 64), w.dtype)
    pieces = []
    for h in range(4):
        pieces += [w[:, h * 192:(h + 1) * 192], z]
    return jnp.concatenate(pieces, axis=1).astype(jnp.bfloat16)


def _reorder_w_ukv(w):
    k = [w[:, h * 256:h * 256 + 128] for h in range(4)]
    v = [w[:, h * 256 + 128:(h + 1) * 256] for h in range(4)]
    return jnp.concatenate(k + v, axis=1).astype(jnp.bfloat16)


def _mla_up_kernel(cq0_ref, cq1_ref, ckv0_ref, ckv1_ref, ckr_ref, qnw_ref, kvnw_ref, wuq_ref, wukv_ref,
                   cos_ref, sa_ref, sb_ref, qf_ref, kf_ref, vc_ref):
    def norm(ref0, ref1, w_ref):
        c = jnp.concatenate([ref0[...], ref1[...]], axis=1).astype(jnp.float32)
        ms = jnp.mean(c * c, axis=-1, keepdims=True)
        return (c * lax.rsqrt(ms + NORM_EPS) * w_ref[...]).astype(jnp.bfloat16)

    scale = 192 ** -0.5 * LOG2E
    q = jnp.dot(norm(cq0_ref, cq1_ref, qnw_ref), wuq_ref[...], preferred_element_type=jnp.float32)
    kv = jnp.dot(norm(ckv0_ref, ckv1_ref, kvnw_ref), wukv_ref[...], preferred_element_type=jnp.float32)
    kr = ckr_ref[...]
    for h in range(4):
        nope = q[:, 2 * h * LANES:(2 * h + 1) * LANES]
        rope = _rope(q[:, (2 * h + 1) * LANES:(2 * h + 2) * LANES], cos_ref[...], sa_ref[...], sb_ref[...])
        qf_ref[:, 2 * h * LANES:(2 * h + 1) * LANES] = (nope * scale).astype(qf_ref.dtype)
        qf_ref[:, (2 * h + 1) * LANES:(2 * h + 2) * LANES] = (rope * scale).astype(qf_ref.dtype)
        kf_ref[:, 2 * h * LANES:(2 * h + 1) * LANES] = kv[:, h * LANES:(h + 1) * LANES].astype(kf_ref.dtype)
        kf_ref[:, (2 * h + 1) * LANES:(2 * h + 2) * LANES] = kr
    vc_ref[...] = kv[:, 4 * LANES:].astype(vc_ref.dtype)


def _mla_up(proj, qn_w, kvn_w, w_uq_p, w_ukv_r, cos, sa, sb):
    m = proj.shape[0]
    tm = 512
    tab = pl.BlockSpec((tm, LANES), lambda i: (i, 0))
    full = lambda shape: pl.BlockSpec(shape, lambda i: (0, 0))
    half = lambda col: pl.BlockSpec((tm, 2 * LANES), lambda i: (i, col))
    return pl.pallas_call(
        _mla_up_kernel,
        grid=(m // tm,),
        in_specs=[
            half(SLAB_CQ // 2), half(SLAB_CQ // 2 + 1), half(SLAB_CKV // 2), half(SLAB_CKV // 2 + 1),
            pl.BlockSpec((tm, LANES), lambda i: (i, SLAB_CKR)),
            full((1, 512)), full((1, 512)), full((512, 1024)), full((512, 1024)),
            tab, tab, tab,
        ],
        out_specs=[
            pl.BlockSpec((tm, 1024), lambda i: (i, 0)),
            pl.BlockSpec((tm, 1024), lambda i: (i, 0)),
            pl.BlockSpec((tm, 512), lambda i: (i, 0)),
        ],
        out_shape=[
            jax.ShapeDtypeStruct((m, 1024), jnp.bfloat16),
            jax.ShapeDtypeStruct((m, 1024), jnp.bfloat16),
            jax.ShapeDtypeStruct((m, 512), jnp.bfloat16),
        ],
        compiler_params=_cparams(("parallel",)),
        name="mla_up",
    )(proj, proj, proj, proj, proj, qn_w.reshape(1, 512), kvn_w.reshape(1, 512), w_uq_p, w_ukv_r, cos, sa, sb)


def _swa_kernel(sink_ref, q_ref, kc_ref, kp_ref, vc_ref, vp_ref, g0_ref, g1_ref, g2_ref, g3_ref, o_ref, *, blocks):
    w = SW_WINDOW
    gate_refs = (g0_ref, g1_ref, g2_ref, g3_ref)
    first = pl.program_id(1) == 0
    lane = lax.broadcasted_iota(jnp.int32, ((blocks + 1) * w, LANES), 1)
    lo = lane < 64

    kt = jnp.concatenate([kp_ref[...], kc_ref[...]], axis=0).astype(jnp.float32)
    kr = pltpu.roll(kt, 64, 1)
    k_halves = []
    for g in range(2):
        a, b = (kt, kr) if g == 0 else (kr, kt)
        k_halves.append((jnp.where(lo, a, 0.0).astype(jnp.bfloat16), jnp.where(lo, 0.0, b).astype(jnp.bfloat16)))

    vt = jnp.concatenate([vp_ref[...], vc_ref[...]], axis=0).astype(jnp.float32).T
    zeros = jnp.zeros((64, 2 * w), jnp.bfloat16)
    ones = jnp.ones((ONES_ROWS, 2 * w), jnp.bfloat16)
    no_ones = jnp.zeros((ONES_ROWS, 2 * w), jnp.bfloat16)

    key = lax.broadcasted_iota(jnp.int32, (2 * w, 4 * w), 0)
    qry = lax.broadcasted_iota(jnp.int32, (2 * w, 4 * w), 1) & (w - 1)
    rel = qry + w - key
    in_band = (rel >= 0) & (rel < w)
    first_key = jnp.where(first, w, 0)
    bias = jnp.where(in_band, 0.0, NEG_INF)
    bias_first = jnp.where(in_band & (key >= first_key), 0.0, NEG_INF)

    def scores(n, g):
        k_lo, k_hi = k_halves[g]
        rows = slice(n * w, (n + 2) * w)
        kk = jnp.concatenate([k_lo[rows], k_hi[rows]], axis=0)
        qs = jnp.concatenate(
            [q_ref[n * w:(n + 1) * w, (4 * g + s) * LANES:(4 * g + s + 1) * LANES] for s in range(4)], axis=0)
        st = lax.dot_general(kk, qs, _NT, preferred_element_type=jnp.float32)
        ps, sink_terms = [], []
        for e in range(2):
            blk = st[e * 2 * w:(e + 1) * 2 * w] + (bias_first if n == 0 else bias)
            sink = jnp.concatenate(
                [jnp.full((1, w), sink_ref[8 * g + 2 * s + e] * LOG2E, jnp.float32) for s in range(4)], axis=1)
            m = jnp.maximum(jnp.max(blk, axis=0, keepdims=True), sink)
            ps.append(jnp.exp2(blk - m).astype(jnp.bfloat16))
            sink_terms.append(jnp.exp2(sink - m))
        return jnp.concatenate(ps, axis=0), sink_terms

    def outputs(n, g, pt, sink_terms):
        vg = vt[64 * g:64 * (g + 1), n * w:(n + 2) * w].astype(jnp.bfloat16)
        lhs = jnp.concatenate([
            jnp.concatenate([vg, zeros], axis=1), jnp.concatenate([zeros, vg], axis=1),
            jnp.concatenate([ones, no_ones], axis=1), jnp.concatenate([no_ones, ones], axis=1)], axis=0)
        acc = jnp.dot(lhs, pt, preferred_element_type=jnp.float32)
        den_even = acc[128:129] + sink_terms[0]
        den_odd = acc[128 + ONES_ROWS:129 + ONES_ROWS] + sink_terms[1]
        ot = jnp.concatenate([acc[0:64] * (1.0 / den_even), acc[64:128] * (1.0 / den_odd)], axis=0)
        for s in range(4):
            slab = 4 * g + s
            gate = gate_refs[slab // 2][n * w:(n + 1) * w, (slab % 2) * LANES:(slab % 2 + 1) * LANES]
            o = ot[:, s * w:(s + 1) * w].T
            o_ref[n * w:(n + 1) * w, slab * LANES:(slab + 1) * LANES] = (
                o * gate.astype(jnp.float32)).astype(o_ref.dtype)

    pending = None
    for item in [(n, g) for n in range(blocks) for g in range(2)] + [None]:
        computed = scores(*item) if item is not None else None
        if pending is not None:
            outputs(*pending[0], *pending[1])
        pending = (item, computed)


def _swa(proj, sinks, batch, seq):
    w = SW_WINDOW
    blocks = 4
    rows = blocks * w
    steps = seq // rows
    nb = seq // w
    cur = lambda slab: pl.BlockSpec((rows, LANES), lambda b, i: (b * steps + i, slab))
    prev = lambda slab: pl.BlockSpec(
        (w, LANES), lambda b, i: (jnp.maximum(b * nb + i * blocks - 1, 0), slab))
    gate = lambda pair: pl.BlockSpec((rows, 2 * LANES), lambda b, i: (b * steps + i, SLAB_BG // 2 + pair))
    return pl.pallas_call(
        functools.partial(_swa_kernel, blocks=blocks),
        grid=(batch, steps),
        in_specs=[
            pl.BlockSpec(memory_space=pltpu.SMEM),
            pl.BlockSpec((rows, 1024), lambda b, i: (b * steps + i, SLAB_BQ // 8)),
            cur(SLAB_BK), prev(SLAB_BK), cur(SLAB_BV), prev(SLAB_BV),
            gate(0), gate(1), gate(2), gate(3),
        ],
        out_specs=pl.BlockSpec((rows, 1024), lambda b, i: (b * steps + i, 0)),
        out_shape=jax.ShapeDtypeStruct((batch * seq, 1024), jnp.bfloat16),
        compiler_params=_cparams(("parallel", "arbitrary")),
        name="swa",
    )(sinks.astype(jnp.float32), proj, proj, proj, proj, proj, proj, proj, proj, proj)


OUT_SUB_ROWS = 256


def _out_proj_kernel(ya_ref, yb_ref, yc_ref, wf_ref, x_ref, postw_ref, o_ref, w_ref):
    @pl.when(pl.program_id(0) == 0)
    def _():
        w_ref[...] = wf_ref[...].astype(w_ref.dtype)

    for r in range(0, o_ref.shape[0], OUT_SUB_ROWS):
        rows = slice(r, r + OUT_SUB_ROWS)
        out = jnp.dot(ya_ref[rows, :], w_ref[0:512, :], preferred_element_type=jnp.float32)
        out += jnp.dot(yb_ref[rows, :], w_ref[512:1536, :], preferred_element_type=jnp.float32)
        out += jnp.dot(yc_ref[rows, :], w_ref[1536:2048, :], preferred_element_type=jnp.float32)
        ms = jnp.mean(out * out, axis=-1, keepdims=True)
        o_ref[rows, :] = x_ref[rows, :] + out * lax.rsqrt(ms + NORM_EPS) * postw_ref[...]


def _out_proj(ya, yb, yc, w_out, layer, x2, post_w):
    m = x2.shape[0]
    tm = 512
    row = lambda width: pl.BlockSpec((tm, width), lambda i: (i, 0))
    return pl.pallas_call(
        _out_proj_kernel,
        grid=(m // tm,),
        in_specs=[
            row(512), row(1024), row(512),
            pl.BlockSpec((None, D_MODEL, D_MODEL), lambda i: (layer, 0, 0), pipeline_mode=pl.Buffered(1)),
            row(D_MODEL),
            pl.BlockSpec((1, D_MODEL), lambda i: (0, 0)),
        ],
        out_specs=row(D_MODEL),
        out_shape=jax.ShapeDtypeStruct((m, D_MODEL), jnp.float32),
        scratch_shapes=[pltpu.VMEM((D_MODEL, D_MODEL), jnp.bfloat16)],
        compiler_params=_cparams(("arbitrary",)),
        name="out_proj",
    )(ya, yb, yc, w_out, x2, post_w.reshape(1, D_MODEL))


def kernel(x, positions, pre_norm_w, post_norm_w, w_in, diff_lambda_q1, diff_lambda_k1, diff_lambda_q2,
           diff_lambda_k2, diff_subln_w, sink_logits, mla_q_norm_w, mla_kv_norm_w, w_uq, w_ukv, w_out):
    batch, seq, d = x.shape
    depth = w_in.shape[0]
    cos, sa, sb = _rope_tables(positions)
    x2 = x.reshape(batch * seq, d)
    for layer in range(depth):
        lam_init = 0.8 - 0.6 * math.exp(-0.3 * layer)
        lam_rows = jnp.stack([diff_lambda_q1[layer], diff_lambda_k1[layer],
                              diff_lambda_q2[layer], diff_lambda_k2[layer]]).astype(jnp.float32)
        proj = _in_proj(x2, pre_norm_w[layer], w_in, layer, cos, sa, sb)
        ya = _diff_attn(proj, lam_rows, diff_subln_w[layer], batch, seq, lam_init)
        yb = _swa(proj, sink_logits[layer], batch, seq)
        qf, kf, vc = _mla_up(proj, mla_q_norm_w[layer], mla_kv_norm_w[layer],
                             _pad_w_uq(w_uq[layer]), _reorder_w_ukv(w_ukv[layer]), cos, sa, sb)
        yc = _mla_attn(qf, kf, vc, proj, batch, seq)
        x2 = _out_proj(ya, yb, yc, w_out, layer, x2, post_norm_w[layer])
    return x2.reshape(batch, seq, d)
```

```python
import functools
import math

import jax
import jax.numpy as jnp
import numpy as np
from jax import lax
from jax.experimental import pallas as pl
from jax.experimental.pallas import tpu as pltpu

D_MODEL = 2048
ROPE_THETA = 10000.0
NORM_EPS = 1e-6
NEG_INF = -1e30
LOG2E = math.log2(math.e)
DA_SUBLN_EPS = 1e-5
SW_WINDOW = 128
LANES = 128

IN_TILES = (
    ((0, 1, 2, 3, 8, 9, 10, 11), ("rope_q",) * 4 + ("plain",) * 4),
    ((4, 5, 6, 7, 34, 35, 36, 37), ("rope",) * 4 + ("plain",) * 4),
    ((16, 17, 18, 19, 38, 39, 40, 41), ("rope_q",) * 4 + ("plain",) * 4),
    ((20, 21, 22, 23, 12, 13, 14, 15), ("rope_q",) * 4 + ("silu",) * 4),
    ((26, 27, 28, 29, 30, 31, 32, 33), ("silu",) * 8),
    ((42, 43, 44, 45, 46, 24, 25, 25), ("kr",) + ("gate_c",) * 4 + ("rope", "plain", "skip")),
)
IN_TILE = 8 * LANES
N_SLABS = 8 * len(IN_TILES)
SLAB_AQ, SLAB_AV, SLAB_AK, SLAB_CQ = 0, 4, 8, 12
SLAB_BQ_LO, SLAB_CKV, SLAB_BQ_HI, SLAB_AG = 16, 20, 24, 28
SLAB_BG, SLAB_CKR, SLAB_CG, SLAB_BK, SLAB_BV = 32, 40, 41, 45, 46

VMEM_LIMIT = 56 * 1024 * 1024

_NT = (((1,), (1,)), ((), ()))


def _cparams(sem):
    return pltpu.CompilerParams(dimension_semantics=sem, vmem_limit_bytes=VMEM_LIMIT)


def _rope_table_kernel(pos_ref, freq_ref, cos_ref, sa_ref, sb_ref):
    ang = pos_ref[...].astype(jnp.float32) * freq_ref[...]
    c, s = jnp.cos(ang), jnp.sin(ang)
    lane = lax.broadcasted_iota(jnp.int32, ang.shape, 1)
    first_half = (lane % 64) < 32
    cos_ref[...] = c
    sa_ref[...] = jnp.where(first_half, -s, 0.0)
    sb_ref[...] = jnp.where(first_half, 0.0, s)


def _rope_tables(positions):
    m = positions.size
    pos_b = jnp.broadcast_to(positions.reshape(m, 1), (m, LANES))
    inv_freq = jnp.power(ROPE_THETA, -jnp.arange(0, 64, 2, dtype=jnp.float32) / 64)
    freq = jnp.tile(inv_freq, 4).reshape(1, LANES)
    tm = 1024
    spec = pl.BlockSpec((tm, LANES), lambda i: (i, 0))
    return pl.pallas_call(
        _rope_table_kernel,
        grid=(m // tm,),
        in_specs=[spec, pl.BlockSpec((1, LANES), lambda i: (0, 0))],
        out_specs=[spec, spec, spec],
        out_shape=[jax.ShapeDtypeStruct((m, LANES), jnp.float32)] * 3,
        compiler_params=_cparams(("parallel",)),
        name="rope_tables",
    )(pos_b, freq)


def _rope(a, cos, sa, sb):
    return a * cos + pltpu.roll(a, 96, 1) * sa + pltpu.roll(a, 32, 1) * sb


def _silu(a):
    return a * (0.5 + 0.5 * jnp.tanh(0.5 * a))


def _in_proj_kernel(x_ref, prew_ref, *rest):
    w_refs, (cos_ref, sa_ref, sb_ref, o_ref, h_ref) = rest[:8], rest[8:]
    j = pl.program_id(1)

    @pl.when(j == 0)
    def _():
        x = x_ref[...]
        ms = jnp.mean(x * x, axis=-1, keepdims=True)
        h_ref[...] = (x * lax.rsqrt(ms + NORM_EPS) * prew_ref[...]).astype(jnp.bfloat16)

    def epilogue(kinds):
        w = jnp.concatenate([w_ref[...] for w_ref in w_refs], axis=0).astype(jnp.bfloat16)
        acc = lax.dot_general(h_ref[...], w, _NT, preferred_element_type=jnp.float32)
        slab = lambda s: acc[:, s * LANES:(s + 1) * LANES]
        lane = lax.broadcasted_iota(jnp.int32, (acc.shape[0], LANES), 1)
        for s, kind in enumerate(kinds):
            a = slab(s)
            if kind == "skip":
                a = jnp.zeros_like(a)
            if kind in ("rope", "rope_q", "kr"):
                a = _rope(a, cos_ref[...], sa_ref[...], sb_ref[...])
                if kind == "rope_q":
                    a = a * (0.125 * LOG2E)
            elif kind == "silu":
                a = _silu(a)
            elif kind == "gate_c":
                a = jnp.where(lane < 64, pltpu.roll(_silu(slab(s - 1)), 64, 1), pltpu.roll(_silu(a), 64, 1))
            o_ref[:, s * LANES:(s + 1) * LANES] = a.astype(o_ref.dtype)

    for t, (_, kinds) in enumerate(IN_TILES):
        pl.when(j == t)(functools.partial(epilogue, kinds))


def _in_proj(x2, pre_w, w_in, layer, cos, sa, sb):
    m = x2.shape[0]
    tm = 1024
    tab = pl.BlockSpec((tm, LANES), lambda i, j: (i, 0))

    def slab_spec(position):
        sources = [slabs[position] for slabs, _ in IN_TILES]

        def index_map(i, j):
            source = sources[-1]
            for t in range(len(sources) - 2, -1, -1):
                source = jnp.where(j == t, sources[t], source)
            return layer, source, 0
        return pl.BlockSpec((None, LANES, D_MODEL), index_map)

    wt = jnp.swapaxes(w_in, 1, 2)
    return pl.pallas_call(
        _in_proj_kernel,
        grid=(m // tm, len(IN_TILES)),
        in_specs=[
            pl.BlockSpec((tm, D_MODEL), lambda i, j: (i, 0)),
            pl.BlockSpec((1, D_MODEL), lambda i, j: (0, 0)),
            *[slab_spec(position) for position in range(8)],
            tab, tab, tab,
        ],
        out_specs=pl.BlockSpec((tm, IN_TILE), lambda i, j: (i, j)),
        out_shape=jax.ShapeDtypeStruct((m, N_SLABS * LANES), jnp.bfloat16),
        scratch_shapes=[pltpu.VMEM((tm, D_MODEL), jnp.bfloat16)],
        compiler_params=_cparams(("parallel", "arbitrary")),
        name="in_proj",
    )(x2, pre_w.reshape(1, D_MODEL), *([wt] * 8), cos, sa, sb)


ATTN_TQ = 256


ONES_ROWS = 16


def _transpose_v(v_ref, vt_ref):
    dv = v_ref.shape[1]
    vt_ref[0:dv, :] = v_ref[...].astype(jnp.float32).T.astype(vt_ref.dtype)
    vt_ref[dv:, :] = jnp.ones((ONES_ROWS, vt_ref.shape[1]), vt_ref.dtype)


ATTN_STREAMS = 4


def _tile_streams(n_tiles):
    up, down = list(range(0, n_tiles, 2)), list(range(n_tiles - 1, 0, -2))
    half = ATTN_STREAMS // 2
    return [tiles[r::half] for r in range(half) for tiles in (up, down)]


def _causal_attention(streams, k_ref, vt_ref, s_ref, tq, emit):
    key = lax.broadcasted_iota(jnp.int32, (tq, tq), 0)
    qry = lax.broadcasted_iota(jnp.int32, (tq, tq), 1)
    on_or_below_diag = key <= qry

    def phase_a(p, n, c, m_run):
        load_q, i, _ = streams[p][n]
        s = lax.dot_general(k_ref[c * tq:(c + 1) * tq, :], load_q(), _NT, preferred_element_type=jnp.float32)
        if c == i:
            s = jnp.where(on_or_below_diag, s, NEG_INF)
        s_ref[2 * p + n % 2, c * tq:(c + 1) * tq, :] = s
        part = jnp.max(s.reshape(tq // 8, 8, tq), axis=0)
        return part if m_run is None else jnp.maximum(m_run, part)

    def phase_b(p, n, c, m8, acc):
        s = s_ref[2 * p + n % 2, c * tq:(c + 1) * tq, :]
        pt = jnp.exp2(s.reshape(tq // 8, 8, tq) - m8[None]).reshape(tq, tq).astype(jnp.bfloat16)
        d = jnp.dot(vt_ref[:, c * tq:(c + 1) * tq], pt, preferred_element_type=jnp.float32)
        return d if acc is None else acc + d

    chunks = lambda p, n: streams[p][n][1] + 1 if 0 <= n < len(streams[p]) else 0
    m8_prev = [None] * len(streams)
    for n in range(max(len(st) for st in streams) + 1):
        m_run = [None] * len(streams)
        acc = [None] * len(streams)
        for c in range(max(max(chunks(p, n), chunks(p, n - 1)) for p in range(len(streams)))):
            for p in range(len(streams)):
                if c < chunks(p, n - 1):
                    acc[p] = phase_b(p, n - 1, c, m8_prev[p], acc[p])
                if c < chunks(p, n):
                    m_run[p] = phase_a(p, n, c, m_run[p])
        for p in range(len(streams)):
            if chunks(p, n - 1):
                emit(streams[p][n - 1][2], acc[p])
            if chunks(p, n):
                m8_prev[p] = jnp.broadcast_to(jnp.max(m_run[p], axis=0, keepdims=True), (8, tq))


def _attn_scratch(seq):
    return [pltpu.VMEM((LANES + ONES_ROWS, seq), jnp.bfloat16),
            pltpu.VMEM((2 * ATTN_STREAMS, seq, ATTN_TQ), jnp.float32)]


def _diff_attn_kernel(q_ref, k_ref, v_ref, g_ref, lam_ref, subw_ref, o_ref, vt_ref, s_ref, *, tq, lam_init):
    _transpose_v(v_ref, vt_ref)
    dv = v_ref.shape[1]
    t = lam_ref[...]
    lam = (jnp.exp(jnp.sum(t[0:1] * t[1:2], axis=-1, keepdims=True))
           - jnp.exp(jnp.sum(t[2:3] * t[3:4], axis=-1, keepdims=True)) + lam_init)
    lane = lax.broadcasted_iota(jnp.int32, (tq, LANES), 1)

    def component(i, second):
        def load_q():
            q = q_ref[i * tq:(i + 1) * tq, :]
            keep = (lane >= 64) if second else (lane < 64)
            return jnp.where(keep, q, jnp.zeros_like(q))
        return load_q

    streams = [[(component(i, second), i, (i, second)) for i in tiles for second in (False, True)]
               for tiles in _tile_streams(q_ref.shape[0] // tq)]
    first = {}

    def emit(tag, acc):
        i, second = tag
        o_n = acc[0:dv] * (1.0 / acc[dv:dv + 1])
        if not second:
            first[i] = o_n
            return
        rows = slice(i * tq, (i + 1) * tq)
        o = first.pop(i) - lam * o_n
        ms = jnp.mean(o * o, axis=0, keepdims=True)
        o = (o * lax.rsqrt(ms + DA_SUBLN_EPS)).T
        o = o * subw_ref[...] * (1.0 - lam_init)
        o_ref[rows, :] = (o * g_ref[rows, :].astype(jnp.float32)).astype(o_ref.dtype)

    _causal_attention(streams, k_ref, vt_ref, s_ref, tq, emit)


def _diff_attn(proj, lam_rows, subln_w, batch, seq, lam_init):
    heads = 4
    slab = lambda first: pl.BlockSpec((seq, LANES), lambda b, h: (b, first + h))
    return pl.pallas_call(
        functools.partial(_diff_attn_kernel, tq=ATTN_TQ, lam_init=lam_init),
        grid=(batch, heads),
        in_specs=[
            slab(SLAB_AQ), slab(SLAB_AK), slab(SLAB_AV), slab(SLAB_AG),
            pl.BlockSpec((4, 64), lambda b, h: (0, 0)),
            pl.BlockSpec((1, LANES), lambda b, h: (0, 0)),
        ],
        out_specs=pl.BlockSpec((seq, LANES), lambda b, h: (b, h)),
        out_shape=jax.ShapeDtypeStruct((batch * seq, heads * LANES), jnp.bfloat16),
        scratch_shapes=_attn_scratch(seq),
        compiler_params=_cparams(("parallel", "parallel")),
        name="diff_attn",
    )(proj, proj, proj, proj, lam_rows, subln_w.reshape(1, LANES))


def _mla_attn_kernel(q_ref, k_ref, v_ref, g_ref, o_ref, vt_ref, s_ref, *, tq):
    _transpose_v(v_ref, vt_ref)
    dv = v_ref.shape[1]
    streams = [[(functools.partial(lambda i: q_ref[i * tq:(i + 1) * tq, :], i), i, i) for i in tiles]
               for tiles in _tile_streams(q_ref.shape[0] // tq)]

    def emit(i, acc):
        rows = slice(i * tq, (i + 1) * tq)
        o = (acc[0:dv] * (1.0 / acc[dv:dv + 1])).T
        o_ref[rows, :] = (o * g_ref[rows, :].astype(jnp.float32)).astype(o_ref.dtype)

    _causal_attention(streams, k_ref, vt_ref, s_ref, tq, emit)


def _mla_attn(qf, kf, vc, proj, batch, seq):
    heads = 4
    return pl.pallas_call(
        functools.partial(_mla_attn_kernel, tq=ATTN_TQ),
        grid=(batch, heads),
        in_specs=[
            pl.BlockSpec((seq, 2 * LANES), lambda b, h: (b, h)),
            pl.BlockSpec((seq, 2 * LANES), lambda b, h: (b, h)),
            pl.BlockSpec((seq, LANES), lambda b, h: (b, h)),
            pl.BlockSpec((seq, LANES), lambda b, h: (b, SLAB_CG + h)),
        ],
        out_specs=pl.BlockSpec((seq, LANES), lambda b, h: (b, h)),
        out_shape=jax.ShapeDtypeStruct((batch * seq, heads * LANES), jnp.bfloat16),
        scratch_shapes=_attn_scratch(seq),
        compiler_params=_cparams(("parallel", "parallel")),
        name="mla_attn",
    )(qf, kf, vc, proj)


def _pad_w_uq(w):
    z = jnp.zeros((w.shape[0], 64), w.dtype)
    pieces = []
    for h in range(4):
        pieces += [w[:, h * 192:(h + 1) * 192], z]
    return jnp.concatenate(pieces, axis=1).astype(jnp.bfloat16)


def _reorder_w_ukv(w):
    k = [w[:, h * 256:h * 256 + 128] for h in range(4)]
    v = [w[:, h * 256 + 128:(h + 1) * 256] for h in range(4)]
    return jnp.concatenate(k + v, axis=1).astype(jnp.bfloat16)


def _mla_up_kernel(cq_ref, ckv_ref, ckr_ref, qnw_ref, kvnw_ref, wuq_ref, wukv_ref,
                   cos_ref, sa_ref, sb_ref, qf_ref, kf_ref, vc_ref):
    def norm(ref, w_ref):
        c = ref[...].astype(jnp.float32)
        ms = jnp.mean(c * c, axis=-1, keepdims=True)
        return (c * lax.rsqrt(ms + NORM_EPS) * w_ref[...]).astype(jnp.bfloat16)

    scale = 192 ** -0.5 * LOG2E
    q = jnp.dot(norm(cq_ref, qnw_ref), wuq_ref[...], preferred_element_type=jnp.float32)
    kv = jnp.dot(norm(ckv_ref, kvnw_ref), wukv_ref[...], preferred_element_type=jnp.float32)
    kr = ckr_ref[...]
    for h in range(4):
        nope = q[:, 2 * h * LANES:(2 * h + 1) * LANES]
        rope = _rope(q[:, (2 * h + 1) * LANES:(2 * h + 2) * LANES], cos_ref[...], sa_ref[...], sb_ref[...])
        qf_ref[:, 2 * h * LANES:(2 * h + 1) * LANES] = (nope * scale).astype(qf_ref.dtype)
        qf_ref[:, (2 * h + 1) * LANES:(2 * h + 2) * LANES] = (rope * scale).astype(qf_ref.dtype)
        kf_ref[:, 2 * h * LANES:(2 * h + 1) * LANES] = kv[:, h * LANES:(h + 1) * LANES].astype(kf_ref.dtype)
        kf_ref[:, (2 * h + 1) * LANES:(2 * h + 2) * LANES] = kr
    vc_ref[...] = kv[:, 4 * LANES:].astype(vc_ref.dtype)


def _mla_up(proj, qn_w, kvn_w, w_uq_p, w_ukv_r, cos, sa, sb):
    m = proj.shape[0]
    tm = 512
    tab = pl.BlockSpec((tm, LANES), lambda i: (i, 0))
    full = lambda shape: pl.BlockSpec(shape, lambda i: (0, 0))
    return pl.pallas_call(
        _mla_up_kernel,
        grid=(m // tm,),
        in_specs=[
            pl.BlockSpec((tm, 4 * LANES), lambda i: (i, SLAB_CQ // 4)),
            pl.BlockSpec((tm, 4 * LANES), lambda i: (i, SLAB_CKV // 4)),
            pl.BlockSpec((tm, LANES), lambda i: (i, SLAB_CKR)),
            full((1, 512)), full((1, 512)), full((512, 1024)), full((512, 1024)),
            tab, tab, tab,
        ],
        out_specs=[
            pl.BlockSpec((tm, 1024), lambda i: (i, 0)),
            pl.BlockSpec((tm, 1024), lambda i: (i, 0)),
            pl.BlockSpec((tm, 512), lambda i: (i, 0)),
        ],
        out_shape=[
            jax.ShapeDtypeStruct((m, 1024), jnp.bfloat16),
            jax.ShapeDtypeStruct((m, 1024), jnp.bfloat16),
            jax.ShapeDtypeStruct((m, 512), jnp.bfloat16),
        ],
        compiler_params=_cparams(("parallel",)),
        name="mla_up",
    )(proj, proj, proj, qn_w.reshape(1, 512), kvn_w.reshape(1, 512), w_uq_p, w_ukv_r, cos, sa, sb)


def _swa_kernel(sink_ref, q0_ref, q1_ref, kc_ref, kp_ref, vc_ref, vp_ref, gate_ref, o_ref, *, blocks):
    w = SW_WINDOW
    q_refs = (q0_ref, q1_ref)
    first = pl.program_id(1) == 0
    lane = lax.broadcasted_iota(jnp.int32, ((blocks + 1) * w, LANES), 1)
    lo = lane < 64

    kt = jnp.concatenate([kp_ref[...], kc_ref[...]], axis=0).astype(jnp.float32)
    kr = pltpu.roll(kt, 64, 1)
    k_halves = []
    for g in range(2):
        a, b = (kt, kr) if g == 0 else (kr, kt)
        k_halves.append((jnp.where(lo, a, 0.0).astype(jnp.bfloat16), jnp.where(lo, 0.0, b).astype(jnp.bfloat16)))

    vt = jnp.concatenate([vp_ref[...], vc_ref[...]], axis=0).astype(jnp.float32).T
    zeros = jnp.zeros((64, 2 * w), jnp.bfloat16)
    ones = jnp.ones((ONES_ROWS, 2 * w), jnp.bfloat16)
    no_ones = jnp.zeros((ONES_ROWS, 2 * w), jnp.bfloat16)

    key = lax.broadcasted_iota(jnp.int32, (2 * w, 4 * w), 0)
    qry = lax.broadcasted_iota(jnp.int32, (2 * w, 4 * w), 1) & (w - 1)
    rel = qry + w - key
    in_band = (rel >= 0) & (rel < w)
    first_key = jnp.where(first, w, 0)
    bias = jnp.where(in_band, 0.0, NEG_INF)
    bias_first = jnp.where(in_band & (key >= first_key), 0.0, NEG_INF)

    def scores(n, g):
        k_lo, k_hi = k_halves[g]
        rows = slice(n * w, (n + 2) * w)
        kk = jnp.concatenate([k_lo[rows], k_hi[rows]], axis=0)
        qs = jnp.concatenate(
            [q_refs[g][n * w:(n + 1) * w, s * LANES:(s + 1) * LANES] for s in range(4)], axis=0)
        st = lax.dot_general(kk, qs, _NT, preferred_element_type=jnp.float32)
        ps, sink_terms = [], []
        for e in range(2):
            blk = st[e * 2 * w:(e + 1) * 2 * w] + (bias_first if n == 0 else bias)
            sink = jnp.concatenate(
                [jnp.full((1, w), sink_ref[8 * g + 2 * s + e] * LOG2E, jnp.float32) for s in range(4)], axis=1)
            m = jnp.maximum(jnp.max(blk, axis=0, keepdims=True), sink)
            ps.append(jnp.exp2(blk - m).astype(jnp.bfloat16))
            sink_terms.append(jnp.exp2(sink - m))
        return jnp.concatenate(ps, axis=0), sink_terms

    def outputs(n, g, pt, sink_terms):
        vg = vt[64 * g:64 * (g + 1), n * w:(n + 2) * w].astype(jnp.bfloat16)
        lhs = jnp.concatenate([
            jnp.concatenate([vg, zeros], axis=1), jnp.concatenate([zeros, vg], axis=1),
            jnp.concatenate([ones, no_ones], axis=1), jnp.concatenate([no_ones, ones], axis=1)], axis=0)
        acc = jnp.dot(lhs, pt, preferred_element_type=jnp.float32)
        den_even = acc[128:129] + sink_terms[0]
        den_odd = acc[128 + ONES_ROWS:129 + ONES_ROWS] + sink_terms[1]
        ot = jnp.concatenate([acc[0:64] * (1.0 / den_even), acc[64:128] * (1.0 / den_odd)], axis=0)
        for s in range(4):
            slab = 4 * g + s
            gate = gate_ref[n * w:(n + 1) * w, slab * LANES:(slab + 1) * LANES]
            o = ot[:, s * w:(s + 1) * w].T
            o_ref[n * w:(n + 1) * w, slab * LANES:(slab + 1) * LANES] = (
                o * gate.astype(jnp.float32)).astype(o_ref.dtype)

    pending = None
    for item in [(n, g) for n in range(blocks) for g in range(2)] + [None]:
        computed = scores(*item) if item is not None else None
        if pending is not None:
            outputs(*pending[0], *pending[1])
        pending = (item, computed)


def _swa(proj, sinks, batch, seq):
    w = SW_WINDOW
    blocks = 4
    rows = blocks * w
    steps = seq // rows
    nb = seq // w
    cur = lambda slab: pl.BlockSpec((rows, LANES), lambda b, i: (b * steps + i, slab))
    prev = lambda slab: pl.BlockSpec(
        (w, LANES), lambda b, i: (jnp.maximum(b * nb + i * blocks - 1, 0), slab))
    wide = lambda width, slab: pl.BlockSpec((rows, width * LANES), lambda b, i: (b * steps + i, slab // width))
    return pl.pallas_call(
        functools.partial(_swa_kernel, blocks=blocks),
        grid=(batch, steps),
        in_specs=[
            pl.BlockSpec(memory_space=pltpu.SMEM),
            wide(4, SLAB_BQ_LO), wide(4, SLAB_BQ_HI),
            cur(SLAB_BK), prev(SLAB_BK), cur(SLAB_BV), prev(SLAB_BV),
            wide(8, SLAB_BG),
        ],
        out_specs=pl.BlockSpec((rows, 1024), lambda b, i: (b * steps + i, 0)),
        out_shape=jax.ShapeDtypeStruct((batch * seq, 1024), jnp.bfloat16),
        compiler_params=_cparams(("parallel", "arbitrary")),
        name="swa",
    )(sinks.astype(jnp.float32), proj, proj, proj, proj, proj, proj, proj)


OUT_SUB_ROWS = 256


def _out_proj_kernel(ya_ref, yb_ref, yc_ref, wf_ref, x_ref, postw_ref, o_ref, w_ref):
    @pl.when(pl.program_id(0) == 0)
    def _():
        w_ref[...] = wf_ref[...].astype(w_ref.dtype)

    for r in range(0, o_ref.shape[0], OUT_SUB_ROWS):
        rows = slice(r, r + OUT_SUB_ROWS)
        out = jnp.dot(ya_ref[rows, :], w_ref[0:512, :], preferred_element_type=jnp.float32)
        out += jnp.dot(yb_ref[rows, :], w_ref[512:1536, :], preferred_element_type=jnp.float32)
        out += jnp.dot(yc_ref[rows, :], w_ref[1536:2048, :], preferred_element_type=jnp.float32)
        ms = jnp.mean(out * out, axis=-1, keepdims=True)
        o_ref[rows, :] = x_ref[rows, :] + out * lax.rsqrt(ms + NORM_EPS) * postw_ref[...]


def _out_proj(ya, yb, yc, w_out, layer, x2, post_w):
    m = x2.shape[0]
    tm = 512
    row = lambda width: pl.BlockSpec((tm, width), lambda i: (i, 0))
    return pl.pallas_call(
        _out_proj_kernel,
        grid=(m // tm,),
        in_specs=[
            row(512), row(1024), row(512),
            pl.BlockSpec((None, D_MODEL, D_MODEL), lambda i: (layer, 0, 0), pipeline_mode=pl.Buffered(1)),
            row(D_MODEL),
            pl.BlockSpec((1, D_MODEL), lambda i: (0, 0)),
        ],
        out_specs=row(D_MODEL),
        out_shape=jax.ShapeDtypeStruct((m, D_MODEL), jnp.float32),
        scratch_shapes=[pltpu.VMEM((D_MODEL, D_MODEL), jnp.bfloat16)],
        compiler_params=_cparams(("arbitrary",)),
        name="out_proj",
    )(ya, yb, yc, w_out, x2, post_w.reshape(1, D_MODEL))


def kernel(x, positions, pre_norm_w, post_norm_w, w_in, diff_lambda_q1, diff_lambda_k1, diff_lambda_q2,
           diff_lambda_k2, diff_subln_w, sink_logits, mla_q_norm_w, mla_kv_norm_w, w_uq, w_ukv, w_out):
    batch, seq, d = x.shape
    depth = w_in.shape[0]
    cos, sa, sb = _rope_tables(positions)
    x2 = x.reshape(batch * seq, d)
    for layer in range(depth):
        lam_init = 0.8 - 0.6 * math.exp(-0.3 * layer)
        lam_rows = jnp.stack([diff_lambda_q1[layer], diff_lambda_k1[layer],
                              diff_lambda_q2[layer], diff_lambda_k2[layer]]).astype(jnp.float32)
        proj = _in_proj(x2, pre_norm_w[layer], w_in, layer, cos, sa, sb)
        ya = _diff_attn(proj, lam_rows, diff_subln_w[layer], batch, seq, lam_init)
        yb = _swa(proj, sink_logits[layer], batch, seq)
        qf, kf, vc = _mla_up(proj, mla_q_norm_w[layer], mla_kv_norm_w[layer],
                             _pad_w_uq(w_uq[layer]), _reorder_w_ukv(w_ukv[layer]), cos, sa, sb)
        yc = _mla_attn(qf, kf, vc, proj, batch, seq)
        x2 = _out_proj(ya, yb, yc, w_out, layer, x2, post_norm_w[layer])
    return x2.reshape(batch, seq, d)
```

```python
import functools
import math

import jax
import jax.numpy as jnp
import numpy as np
from jax import lax
from jax.experimental import pallas as pl
from jax.experimental.pallas import tpu as pltpu

D_MODEL = 2048
ROPE_THETA = 10000.0
NORM_EPS = 1e-6
NEG_INF = -1e30
LOG2E = math.log2(math.e)
DA_SUBLN_EPS = 1e-5
SW_WINDOW = 128
LANES = 128

IN_TILES = (
    ((0, 1, 2, 3, 8, 9, 10, 11), ("rope_q",) * 4 + ("plain",) * 4),
    ((4, 5, 6, 7, 34, 35, 36, 37), ("rope",) * 4 + ("plain",) * 4),
    ((16, 17, 18, 19, 38, 39, 40, 41), ("rope_q",) * 4 + ("plain",) * 4),
    ((20, 21, 22, 23, 12, 13, 14, 15), ("rope_q",) * 4 + ("silu",) * 4),
    ((26, 27, 28, 29, 30, 31, 32, 33), ("silu",) * 8),
    ((42, 43, 44, 45, 46, 24, 25, 25), ("kr",) + ("gate_c",) * 4 + ("rope", "plain", "skip")),
)
IN_TILE = 8 * LANES
N_SLABS = 8 * len(IN_TILES)
SLAB_AQ, SLAB_AV, SLAB_AK, SLAB_CQ = 0, 4, 8, 12
SLAB_BQ_LO, SLAB_CKV, SLAB_BQ_HI, SLAB_AG = 16, 20, 24, 28
SLAB_BG, SLAB_CKR, SLAB_CG, SLAB_BK, SLAB_BV = 32, 40, 41, 45, 46

VMEM_LIMIT = 56 * 1024 * 1024

_NT = (((1,), (1,)), ((), ()))


def _cparams(sem):
    return pltpu.CompilerParams(dimension_semantics=sem, vmem_limit_bytes=VMEM_LIMIT)


def _rope_table_kernel(pos_ref, freq_ref, cos_ref, sa_ref, sb_ref):
    ang = pos_ref[...].astype(jnp.float32) * freq_ref[...]
    c, s = jnp.cos(ang), jnp.sin(ang)
    lane = lax.broadcasted_iota(jnp.int32, ang.shape, 1)
    first_half = (lane % 64) < 32
    cos_ref[...] = c
    sa_ref[...] = jnp.where(first_half, -s, 0.0)
    sb_ref[...] = jnp.where(first_half, 0.0, s)


def _rope_tables(positions):
    m = positions.size
    pos_b = jnp.broadcast_to(positions.reshape(m, 1), (m, LANES))
    inv_freq = jnp.power(ROPE_THETA, -jnp.arange(0, 64, 2, dtype=jnp.float32) / 64)
    freq = jnp.tile(inv_freq, 4).reshape(1, LANES)
    tm = 1024
    spec = pl.BlockSpec((tm, LANES), lambda i: (i, 0))
    return pl.pallas_call(
        _rope_table_kernel,
        grid=(m // tm,),
        in_specs=[spec, pl.BlockSpec((1, LANES), lambda i: (0, 0))],
        out_specs=[spec, spec, spec],
        out_shape=[jax.ShapeDtypeStruct((m, LANES), jnp.float32)] * 3,
        compiler_params=_cparams(("parallel",)),
        name="rope_tables",
    )(pos_b, freq)


def _rope(a, cos, sa, sb):
    return a * cos + pltpu.roll(a, 96, 1) * sa + pltpu.roll(a, 32, 1) * sb


def _silu(a):
    return a * (0.5 + 0.5 * jnp.tanh(0.5 * a))


def _in_proj_kernel(x_ref, prew_ref, *rest):
    w_refs, (cos_ref, sa_ref, sb_ref, o_ref, h_ref) = rest[:8], rest[8:]
    j = pl.program_id(1)

    @pl.when(j == 0)
    def _():
        x = x_ref[...]
        ms = jnp.mean(x * x, axis=-1, keepdims=True)
        h_ref[...] = (x * lax.rsqrt(ms + NORM_EPS) * prew_ref[...]).astype(jnp.bfloat16)

    def epilogue(kinds):
        w = jnp.concatenate([w_ref[...] for w_ref in w_refs], axis=0).astype(jnp.bfloat16)
        acc = lax.dot_general(h_ref[...], w, _NT, preferred_element_type=jnp.float32)
        slab = lambda s: acc[:, s * LANES:(s + 1) * LANES]
        lane = lax.broadcasted_iota(jnp.int32, (acc.shape[0], LANES), 1)
        for s, kind in enumerate(kinds):
            a = slab(s)
            if kind == "skip":
                a = jnp.zeros_like(a)
            if kind in ("rope", "rope_q", "kr"):
                a = _rope(a, cos_ref[...], sa_ref[...], sb_ref[...])
                if kind == "rope_q":
                    a = a * (0.125 * LOG2E)
            elif kind == "silu":
                a = _silu(a)
            elif kind == "gate_c":
                a = jnp.where(lane < 64, pltpu.roll(_silu(slab(s - 1)), 64, 1), pltpu.roll(_silu(a), 64, 1))
            o_ref[:, s * LANES:(s + 1) * LANES] = a.astype(o_ref.dtype)

    for t, (_, kinds) in enumerate(IN_TILES):
        pl.when(j == t)(functools.partial(epilogue, kinds))


def _in_proj(x2, pre_w, w_in, layer, cos, sa, sb):
    m = x2.shape[0]
    tm = 1024
    tab = pl.BlockSpec((tm, LANES), lambda i, j: (i, 0))

    def slab_spec(position):
        sources = [slabs[position] for slabs, _ in IN_TILES]

        def index_map(i, j):
            source = sources[-1]
            for t in range(len(sources) - 2, -1, -1):
                source = jnp.where(j == t, sources[t], source)
            return layer, source, 0
        return pl.BlockSpec((None, LANES, D_MODEL), index_map)

    wt = jnp.swapaxes(w_in, 1, 2)
    return pl.pallas_call(
        _in_proj_kernel,
        grid=(m // tm, len(IN_TILES)),
        in_specs=[
            pl.BlockSpec((tm, D_MODEL), lambda i, j: (i, 0)),
            pl.BlockSpec((1, D_MODEL), lambda i, j: (0, 0)),
            *[slab_spec(position) for position in range(8)],
            tab, tab, tab,
        ],
        out_specs=pl.BlockSpec((tm, IN_TILE), lambda i, j: (i, j)),
        out_shape=jax.ShapeDtypeStruct((m, N_SLABS * LANES), jnp.bfloat16),
        scratch_shapes=[pltpu.VMEM((tm, D_MODEL), jnp.bfloat16)],
        compiler_params=_cparams(("parallel", "arbitrary")),
        name="in_proj",
    )(x2, pre_w.reshape(1, D_MODEL), *([wt] * 8), cos, sa, sb)


ATTN_TQ = 256


ONES_ROWS = 16


def _transpose_v(v_ref, vt_ref):
    dv = v_ref.shape[1]
    vt_ref[0:dv, :] = v_ref[...].astype(jnp.float32).T.astype(vt_ref.dtype)
    vt_ref[dv:, :] = jnp.ones((ONES_ROWS, vt_ref.shape[1]), vt_ref.dtype)


ATTN_STREAMS = 4


def _tile_streams(n_tiles):
    assert n_tiles == 2 * ATTN_STREAMS
    return [[n_tiles - 1 - r, r] for r in range(ATTN_STREAMS)]


def _causal_attention(streams, k_ref, vt_ref, tq, emit):
    key = lax.broadcasted_iota(jnp.int32, (tq, tq), 0)
    qry = lax.broadcasted_iota(jnp.int32, (tq, tq), 1)
    on_or_below_diag = key <= qry
    flat = [[(item, c) for item in stream for c in range(item[1] + 1)] for stream in streams]
    q_t = {}

    def scores(p, k):
        if k >= len(flat[p]):
            return None
        (load_q, i, tag), c = flat[p][k]
        if c == 0:
            q_t[tag] = load_q().astype(jnp.float32).T.astype(jnp.bfloat16)
        s = jnp.dot(k_ref[c * tq:(c + 1) * tq, :], q_t[tag], preferred_element_type=jnp.float32)
        if c == i:
            s = jnp.where(on_or_below_diag, s, NEG_INF)
        return s.reshape(tq // 8, 8, tq)

    m_run = [None] * len(streams)
    acc = [None] * len(streams)
    s_next = [scores(p, 0) for p in range(len(streams))]
    for k in range(max(len(f) for f in flat)):
        s_cur, s_next = s_next, [scores(p, k + 1) for p in range(len(streams))]
        for p in range(len(streams)):
            if s_cur[p] is None:
                continue
            (_, i, tag), c = flat[p][k]
            m = jnp.max(jnp.max(s_cur[p], axis=0), axis=0, keepdims=True)
            if c > 0:
                m = jnp.maximum(m_run[p], m)
            pt = jnp.exp2(s_cur[p] - jnp.broadcast_to(m, (8, tq))[None]).reshape(tq, tq).astype(jnp.bfloat16)
            d = jnp.dot(vt_ref[:, c * tq:(c + 1) * tq], pt, preferred_element_type=jnp.float32)
            acc[p] = d if c == 0 else acc[p] * jnp.exp2(m_run[p] - m) + d
            m_run[p] = m
            if c == i:
                emit(tag, acc[p])


def _attn_scratch(seq):
    return [pltpu.VMEM((LANES + ONES_ROWS, seq), jnp.bfloat16)]


def _diff_attn_kernel(q_ref, k_ref, v_ref, g_ref, lam_ref, subw_ref, o_ref, vt_ref, *, tq, lam_init):
    _transpose_v(v_ref, vt_ref)
    dv = v_ref.shape[1]
    t = lam_ref[...]
    lam = (jnp.exp(jnp.sum(t[0:1] * t[1:2], axis=-1, keepdims=True))
           - jnp.exp(jnp.sum(t[2:3] * t[3:4], axis=-1, keepdims=True)) + lam_init)
    lane = lax.broadcasted_iota(jnp.int32, (tq, LANES), 1)

    def component(i, second):
        def load_q():
            q = q_ref[i * tq:(i + 1) * tq, :]
            keep = (lane >= 64) if second else (lane < 64)
            return jnp.where(keep, q, jnp.zeros_like(q))
        return load_q

    streams = [[(component(i, second), i, (i, second)) for i in tiles for second in (False, True)]
               for tiles in _tile_streams(q_ref.shape[0] // tq)]
    first = {}

    def emit(tag, acc):
        i, second = tag
        o_n = acc[0:dv] * (1.0 / acc[dv:dv + 1])
        if not second:
            first[i] = o_n
            return
        rows = slice(i * tq, (i + 1) * tq)
        o = first.pop(i) - lam * o_n
        ms = jnp.mean(o * o, axis=0, keepdims=True)
        o = (o * lax.rsqrt(ms + DA_SUBLN_EPS)).T
        o = o * subw_ref[...] * (1.0 - lam_init)
        o_ref[rows, :] = (o * g_ref[rows, :].astype(jnp.float32)).astype(o_ref.dtype)

    _causal_attention(streams, k_ref, vt_ref, tq, emit)


def _diff_attn(proj, lam_rows, subln_w, batch, seq, lam_init):
    heads = 4
    slab = lambda first: pl.BlockSpec((seq, LANES), lambda b, h: (b, first + h))
    return pl.pallas_call(
        functools.partial(_diff_attn_kernel, tq=ATTN_TQ, lam_init=lam_init),
        grid=(batch, heads),
        in_specs=[
            slab(SLAB_AQ), slab(SLAB_AK), slab(SLAB_AV), slab(SLAB_AG),
            pl.BlockSpec((4, 64), lambda b, h: (0, 0)),
            pl.BlockSpec((1, LANES), lambda b, h: (0, 0)),
        ],
        out_specs=pl.BlockSpec((seq, LANES), lambda b, h: (b, h)),
        out_shape=jax.ShapeDtypeStruct((batch * seq, heads * LANES), jnp.bfloat16),
        scratch_shapes=_attn_scratch(seq),
        compiler_params=_cparams(("parallel", "parallel")),
        name="diff_attn",
    )(proj, proj, proj, proj, lam_rows, subln_w.reshape(1, LANES))


def _mla_attn_kernel(q_ref, k_ref, v_ref, g_ref, o_ref, vt_ref, *, tq):
    _transpose_v(v_ref, vt_ref)
    dv = v_ref.shape[1]
    streams = [[(functools.partial(lambda i: q_ref[i * tq:(i + 1) * tq, :], i), i, i) for i in tiles]
               for tiles in _tile_streams(q_ref.shape[0] // tq)]

    def emit(i, acc):
        rows = slice(i * tq, (i + 1) * tq)
        o = (acc[0:dv] * (1.0 / acc[dv:dv + 1])).T
        o_ref[rows, :] = (o * g_ref[rows, :].astype(jnp.float32)).astype(o_ref.dtype)

    _causal_attention(streams, k_ref, vt_ref, tq, emit)


def _mla_attn(qf, kf, vc, proj, batch, seq):
    heads = 4
    return pl.pallas_call(
        functools.partial(_mla_attn_kernel, tq=ATTN_TQ),
        grid=(batch, heads),
        in_specs=[
            pl.BlockSpec((seq, 2 * LANES), lambda b, h: (b, h)),
            pl.BlockSpec((seq, 2 * LANES), lambda b, h: (b, h)),
            pl.BlockSpec((seq, LANES), lambda b, h: (b, h)),
            pl.BlockSpec((seq, LANES), lambda b, h: (b, SLAB_CG + h)),
        ],
        out_specs=pl.BlockSpec((seq, LANES), lambda b, h: (b, h)),
        out_shape=jax.ShapeDtypeStruct((batch * seq, heads * LANES), jnp.bfloat16),
        scratch_shapes=_attn_scratch(seq),
        compiler_params=_cparams(("parallel", "parallel")),
        name="mla_attn",
    )(qf, kf, vc, proj)


def _pad_w_uq(w):
    z = jnp.zeros((w.shape[0], 64), w.dtype)
    pieces = []
    for h in range(4):
        pieces += [w[:, h * 192:(h + 1) * 192], z]
    return jnp.concatenate(pieces, axis=1).astype(jnp.bfloat16)


def _reorder_w_ukv(w):
    k = [w[:, h * 256:h * 256 + 128] for h in range(4)]
    v = [w[:, h * 256 + 128:(h + 1) * 256] for h in range(4)]
    return jnp.concatenate(k + v, axis=1).astype(jnp.bfloat16)


def _mla_up_kernel(cq_ref, ckv_ref, ckr_ref, qnw_ref, kvnw_ref, wuq_ref, wukv_ref,
                   cos_ref, sa_ref, sb_ref, qf_ref, kf_ref, vc_ref):
    def norm(ref, w_ref):
        c = ref[...].astype(jnp.float32)
        ms = jnp.mean(c * c, axis=-1, keepdims=True)
        return (c * lax.rsqrt(ms + NORM_EPS) * w_ref[...]).astype(jnp.bfloat16)

    scale = 192 ** -0.5 * LOG2E
    q = jnp.dot(norm(cq_ref, qnw_ref), wuq_ref[...], preferred_element_type=jnp.float32)
    kv = jnp.dot(norm(ckv_ref, kvnw_ref), wukv_ref[...], preferred_element_type=jnp.float32)
    kr = ckr_ref[...]
    for h in range(4):
        nope = q[:, 2 * h * LANES:(2 * h + 1) * LANES]
        rope = _rope(q[:, (2 * h + 1) * LANES:(2 * h + 2) * LANES], cos_ref[...], sa_ref[...], sb_ref[...])
        qf_ref[:, 2 * h * LANES:(2 * h + 1) * LANES] = (nope * scale).astype(qf_ref.dtype)
        qf_ref[:, (2 * h + 1) * LANES:(2 * h + 2) * LANES] = (rope * scale).astype(qf_ref.dtype)
        kf_ref[:, 2 * h * LANES:(2 * h + 1) * LANES] = kv[:, h * LANES:(h + 1) * LANES].astype(kf_ref.dtype)
        kf_ref[:, (2 * h + 1) * LANES:(2 * h + 2) * LANES] = kr
    vc_ref[...] = kv[:, 4 * LANES:].astype(vc_ref.dtype)


def _mla_up(proj, qn_w, kvn_w, w_uq_p, w_ukv_r, cos, sa, sb):
    m = proj.shape[0]
    tm = 512
    tab = pl.BlockSpec((tm, LANES), lambda i: (i, 0))
    full = lambda shape: pl.BlockSpec(shape, lambda i: (0, 0))
    return pl.pallas_call(
        _mla_up_kernel,
        grid=(m // tm,),
        in_specs=[
            pl.BlockSpec((tm, 4 * LANES), lambda i: (i, SLAB_CQ // 4)),
            pl.BlockSpec((tm, 4 * LANES), lambda i: (i, SLAB_CKV // 4)),
            pl.BlockSpec((tm, LANES), lambda i: (i, SLAB_CKR)),
            full((1, 512)), full((1, 512)), full((512, 1024)), full((512, 1024)),
            tab, tab, tab,
        ],
        out_specs=[
            pl.BlockSpec((tm, 1024), lambda i: (i, 0)),
            pl.BlockSpec((tm, 1024), lambda i: (i, 0)),
            pl.BlockSpec((tm, 512), lambda i: (i, 0)),
        ],
        out_shape=[
            jax.ShapeDtypeStruct((m, 1024), jnp.bfloat16),
            jax.ShapeDtypeStruct((m, 1024), jnp.bfloat16),
            jax.ShapeDtypeStruct((m, 512), jnp.bfloat16),
        ],
        compiler_params=_cparams(("parallel",)),
        name="mla_up",
    )(proj, proj, proj, qn_w.reshape(1, 512), kvn_w.reshape(1, 512), w_uq_p, w_ukv_r, cos, sa, sb)


def _swa_kernel(sink_ref, q0_ref, q1_ref, kc_ref, kp_ref, vc_ref, vp_ref, gate_ref, o_ref, *, blocks):
    w = SW_WINDOW
    q_refs = (q0_ref, q1_ref)
    first = pl.program_id(1) == 0
    lane = lax.broadcasted_iota(jnp.int32, ((blocks + 1) * w, LANES), 1)
    lo = lane < 64

    kt = jnp.concatenate([kp_ref[...], kc_ref[...]], axis=0).astype(jnp.float32)
    kr = pltpu.roll(kt, 64, 1)
    k_halves = []
    for g in range(2):
        a, b = (kt, kr) if g == 0 else (kr, kt)
        k_halves.append((jnp.where(lo, a, 0.0).astype(jnp.bfloat16), jnp.where(lo, 0.0, b).astype(jnp.bfloat16)))

    vt = jnp.concatenate([vp_ref[...], vc_ref[...]], axis=0).astype(jnp.float32).T
    zeros = jnp.zeros((64, 2 * w), jnp.bfloat16)
    ones = jnp.ones((ONES_ROWS, 2 * w), jnp.bfloat16)
    no_ones = jnp.zeros((ONES_ROWS, 2 * w), jnp.bfloat16)

    key = lax.broadcasted_iota(jnp.int32, (2 * w, 4 * w), 0)
    qry = lax.broadcasted_iota(jnp.int32, (2 * w, 4 * w), 1) & (w - 1)
    rel = qry + w - key
    in_band = (rel >= 0) & (rel < w)
    first_key = jnp.where(first, w, 0)
    bias = jnp.where(in_band, 0.0, NEG_INF)
    bias_first = jnp.where(in_band & (key >= first_key), 0.0, NEG_INF)

    def scores(n, g):
        k_lo, k_hi = k_halves[g]
        rows = slice(n * w, (n + 2) * w)
        kk = jnp.concatenate([k_lo[rows], k_hi[rows]], axis=0)
        qs = jnp.concatenate(
            [q_refs[g][n * w:(n + 1) * w, s * LANES:(s + 1) * LANES] for s in range(4)], axis=0)
        st = lax.dot_general(kk, qs, _NT, preferred_element_type=jnp.float32)
        ps, sink_terms = [], []
        for e in range(2):
            blk = st[e * 2 * w:(e + 1) * 2 * w] + (bias_first if n == 0 else bias)
            sink = jnp.concatenate(
                [jnp.full((1, w), sink_ref[8 * g + 2 * s + e] * LOG2E, jnp.float32) for s in range(4)], axis=1)
            m = jnp.maximum(jnp.max(blk, axis=0, keepdims=True), sink)
            ps.append(jnp.exp2(blk - m).astype(jnp.bfloat16))
            sink_terms.append(jnp.exp2(sink - m))
        return jnp.concatenate(ps, axis=0), sink_terms

    def outputs(n, g, pt, sink_terms):
        vg = vt[64 * g:64 * (g + 1), n * w:(n + 2) * w].astype(jnp.bfloat16)
        lhs = jnp.concatenate([
            jnp.concatenate([vg, zeros], axis=1), jnp.concatenate([zeros, vg], axis=1),
            jnp.concatenate([ones, no_ones], axis=1), jnp.concatenate([no_ones, ones], axis=1)], axis=0)
        acc = jnp.dot(lhs, pt, preferred_element_type=jnp.float32)
        den_even = acc[128:129] + sink_terms[0]
        den_odd = acc[128 + ONES_ROWS:129 + ONES_ROWS] + sink_terms[1]
        ot = jnp.concatenate([acc[0:64] * (1.0 / den_even), acc[64:128] * (1.0 / den_odd)], axis=0)
        for s in range(4):
            slab = 4 * g + s
            gate = gate_ref[n * w:(n + 1) * w, slab * LANES:(slab + 1) * LANES]
            o = ot[:, s * w:(s + 1) * w].T
            o_ref[n * w:(n + 1) * w, slab * LANES:(slab + 1) * LANES] = (
                o * gate.astype(jnp.float32)).astype(o_ref.dtype)

    pending = None
    for item in [(n, g) for n in range(blocks) for g in range(2)] + [None]:
        computed = scores(*item) if item is not None else None
        if pending is not None:
            outputs(*pending[0], *pending[1])
        pending = (item, computed)


def _swa(proj, sinks, batch, seq):
    w = SW_WINDOW
    blocks = 4
    rows = blocks * w
    steps = seq // rows
    nb = seq // w
    cur = lambda slab: pl.BlockSpec((rows, LANES), lambda b, i: (b * steps + i, slab))
    prev = lambda slab: pl.BlockSpec(
        (w, LANES), lambda b, i: (jnp.maximum(b * nb + i * blocks - 1, 0), slab))
    wide = lambda width, slab: pl.BlockSpec((rows, width * LANES), lambda b, i: (b * steps + i, slab // width))
    return pl.pallas_call(
        functools.partial(_swa_kernel, blocks=blocks),
        grid=(batch, steps),
        in_specs=[
            pl.BlockSpec(memory_space=pltpu.SMEM),
            wide(4, SLAB_BQ_LO), wide(4, SLAB_BQ_HI),
            cur(SLAB_BK), prev(SLAB_BK), cur(SLAB_BV), prev(SLAB_BV),
            wide(8, SLAB_BG),
        ],
        out_specs=pl.BlockSpec((rows, 1024), lambda b, i: (b * steps + i, 0)),
        out_shape=jax.ShapeDtypeStruct((batch * seq, 1024), jnp.bfloat16),
        compiler_params=_cparams(("parallel", "arbitrary")),
        name="swa",
    )(sinks.astype(jnp.float32), proj, proj, proj, proj, proj, proj, proj)


OUT_SUB_ROWS = 256


def _out_proj_kernel(ya_ref, yb_ref, yc_ref, wf_ref, x_ref, postw_ref, o_ref, w_ref):
    @pl.when(pl.program_id(0) == 0)
    def _():
        w_ref[...] = wf_ref[...].astype(w_ref.dtype)

    for r in range(0, o_ref.shape[0], OUT_SUB_ROWS):
        rows = slice(r, r + OUT_SUB_ROWS)
        out = jnp.dot(ya_ref[rows, :], w_ref[0:512, :], preferred_element_type=jnp.float32)
        out += jnp.dot(yb_ref[rows, :], w_ref[512:1536, :], preferred_element_type=jnp.float32)
        out += jnp.dot(yc_ref[rows, :], w_ref[1536:2048, :], preferred_element_type=jnp.float32)
        ms = jnp.mean(out * out, axis=-1, keepdims=True)
        o_ref[rows, :] = x_ref[rows, :] + out * lax.rsqrt(ms + NORM_EPS) * postw_ref[...]


def _out_proj(ya, yb, yc, w_out, layer, x2, post_w):
    m = x2.shape[0]
    tm = 512
    row = lambda width: pl.BlockSpec((tm, width), lambda i: (i, 0))
    return pl.pallas_call(
        _out_proj_kernel,
        grid=(m // tm,),
        in_specs=[
            row(512), row(1024), row(512),
            pl.BlockSpec((None, D_MODEL, D_MODEL), lambda i: (layer, 0, 0), pipeline_mode=pl.Buffered(1)),
            row(D_MODEL),
            pl.BlockSpec((1, D_MODEL), lambda i: (0, 0)),
        ],
        out_specs=row(D_MODEL),
        out_shape=jax.ShapeDtypeStruct((m, D_MODEL), jnp.float32),
        scratch_shapes=[pltpu.VMEM((D_MODEL, D_MODEL), jnp.bfloat16)],
        compiler_params=_cparams(("arbitrary",)),
        name="out_proj",
    )(ya, yb, yc, w_out, x2, post_w.reshape(1, D_MODEL))


def kernel(x, positions, pre_norm_w, post_norm_w, w_in, diff_lambda_q1, diff_lambda_k1, diff_lambda_q2,
           diff_lambda_k2, diff_subln_w, sink_logits, mla_q_norm_w, mla_kv_norm_w, w_uq, w_ukv, w_out):
    batch, seq, d = x.shape
    depth = w_in.shape[0]
    cos, sa, sb = _rope_tables(positions)
    x2 = x.reshape(batch * seq, d)
    for layer in range(depth):
        lam_init = 0.8 - 0.6 * math.exp(-0.3 * layer)
        lam_rows = jnp.stack([diff_lambda_q1[layer], diff_lambda_k1[layer],
                              diff_lambda_q2[layer], diff_lambda_k2[layer]]).astype(jnp.float32)
        proj = _in_proj(x2, pre_norm_w[layer], w_in, layer, cos, sa, sb)
        ya = _diff_attn(proj, lam_rows, diff_subln_w[layer], batch, seq, lam_init)
        yb = _swa(proj, sink_logits[layer], batch, seq)
        qf, kf, vc = _mla_up(proj, mla_q_norm_w[layer], mla_kv_norm_w[layer],
                             _pad_w_uq(w_uq[layer]), _reorder_w_ukv(w_ukv[layer]), cos, sa, sb)
        yc = _mla_attn(qf, kf, vc, proj, batch, seq)
        x2 = _out_proj(ya, yb, yc, w_out, layer, x2, post_norm_w[layer])
    return x2.reshape(batch, seq, d)
```

```python
import functools
import math

import jax
import jax.numpy as jnp
import numpy as np
from jax import lax
from jax.experimental import pallas as pl
from jax.experimental.pallas import tpu as pltpu

D_MODEL = 2048
ROPE_THETA = 10000.0
NORM_EPS = 1e-6
NEG_INF = -1e30
LOG2E = math.log2(math.e)
DA_SUBLN_EPS = 1e-5
SW_WINDOW = 128
LANES = 128

IN_TILES = (
    ((0, 1, 2, 3, 8, 9, 10, 11), ("rope_q",) * 4 + ("plain",) * 4),
    ((4, 5, 6, 7, 34, 35, 36, 37), ("rope",) * 4 + ("plain",) * 4),
    ((16, 17, 18, 19, 38, 39, 40, 41), ("rope_q",) * 4 + ("plain",) * 4),
    ((20, 21, 22, 23, 12, 13, 14, 15), ("rope_q",) * 4 + ("silu",) * 4),
    ((26, 27, 28, 29, 30, 31, 32, 33), ("silu",) * 8),
    ((42, 43, 44, 45, 46, 24, 25, 25), ("kr",) + ("gate_c",) * 4 + ("rope", "plain", "skip")),
)
IN_TILE = 8 * LANES
N_SLABS = 8 * len(IN_TILES)
SLAB_AQ, SLAB_AV, SLAB_AK, SLAB_CQ = 0, 4, 8, 12
SLAB_BQ_LO, SLAB_CKV, SLAB_BQ_HI, SLAB_AG = 16, 20, 24, 28
SLAB_BG, SLAB_CKR, SLAB_CG, SLAB_BK, SLAB_BV = 32, 40, 41, 45, 46

VMEM_LIMIT = 56 * 1024 * 1024

_NT = (((1,), (1,)), ((), ()))


def _cparams(sem):
    return pltpu.CompilerParams(dimension_semantics=sem, vmem_limit_bytes=VMEM_LIMIT)


def _rope_table_kernel(pos_ref, freq_ref, cos_ref, sa_ref, sb_ref):
    ang = pos_ref[...].astype(jnp.float32) * freq_ref[...]
    c, s = jnp.cos(ang), jnp.sin(ang)
    lane = lax.broadcasted_iota(jnp.int32, ang.shape, 1)
    first_half = (lane % 64) < 32
    cos_ref[...] = c
    sa_ref[...] = jnp.where(first_half, -s, 0.0)
    sb_ref[...] = jnp.where(first_half, 0.0, s)


def _rope_tables(positions):
    m = positions.size
    pos_b = jnp.broadcast_to(positions.reshape(m, 1), (m, LANES))
    inv_freq = jnp.power(ROPE_THETA, -jnp.arange(0, 64, 2, dtype=jnp.float32) / 64)
    freq = jnp.tile(inv_freq, 4).reshape(1, LANES)
    tm = 1024
    spec = pl.BlockSpec((tm, LANES), lambda i: (i, 0))
    return pl.pallas_call(
        _rope_table_kernel,
        grid=(m // tm,),
        in_specs=[spec, pl.BlockSpec((1, LANES), lambda i: (0, 0))],
        out_specs=[spec, spec, spec],
        out_shape=[jax.ShapeDtypeStruct((m, LANES), jnp.float32)] * 3,
        compiler_params=_cparams(("parallel",)),
        name="rope_tables",
    )(pos_b, freq)


def _rope(a, cos, sa, sb):
    return a * cos + pltpu.roll(a, 96, 1) * sa + pltpu.roll(a, 32, 1) * sb


def _silu(a):
    return a * (0.5 + 0.5 * jnp.tanh(0.5 * a))


def _in_proj_kernel(x_ref, prew_ref, *rest):
    w_refs, (cos_ref, sa_ref, sb_ref, o_ref, h_ref) = rest[:8], rest[8:]
    j = pl.program_id(1)

    @pl.when(j == 0)
    def _():
        x = x_ref[...]
        ms = jnp.mean(x * x, axis=-1, keepdims=True)
        h_ref[...] = (x * lax.rsqrt(ms + NORM_EPS) * prew_ref[...]).astype(jnp.bfloat16)

    def epilogue(kinds):
        w = jnp.concatenate([w_ref[...] for w_ref in w_refs], axis=0).astype(jnp.bfloat16)
        acc = lax.dot_general(h_ref[...], w, _NT, preferred_element_type=jnp.float32)
        slab = lambda s: acc[:, s * LANES:(s + 1) * LANES]
        lane = lax.broadcasted_iota(jnp.int32, (acc.shape[0], LANES), 1)
        for s, kind in enumerate(kinds):
            a = slab(s)
            if kind == "skip":
                a = jnp.zeros_like(a)
            if kind in ("rope", "rope_q", "kr"):
                a = _rope(a, cos_ref[...], sa_ref[...], sb_ref[...])
                if kind == "rope_q":
                    a = a * (0.125 * LOG2E)
            elif kind == "silu":
                a = _silu(a)
            elif kind == "gate_c":
                a = jnp.where(lane < 64, pltpu.roll(_silu(slab(s - 1)), 64, 1), pltpu.roll(_silu(a), 64, 1))
            o_ref[:, s * LANES:(s + 1) * LANES] = a.astype(o_ref.dtype)

    for t, (_, kinds) in enumerate(IN_TILES):
        pl.when(j == t)(functools.partial(epilogue, kinds))


def _in_proj(x2, pre_w, w_in, layer, cos, sa, sb):
    m = x2.shape[0]
    tm = 1024
    tab = pl.BlockSpec((tm, LANES), lambda i, j: (i, 0))

    def slab_spec(position):
        sources = [slabs[position] for slabs, _ in IN_TILES]

        def index_map(i, j):
            source = sources[-1]
            for t in range(len(sources) - 2, -1, -1):
                source = jnp.where(j == t, sources[t], source)
            return layer, source, 0
        return pl.BlockSpec((None, LANES, D_MODEL), index_map)

    wt = jnp.swapaxes(w_in, 1, 2)
    return pl.pallas_call(
        _in_proj_kernel,
        grid=(m // tm, len(IN_TILES)),
        in_specs=[
            pl.BlockSpec((tm, D_MODEL), lambda i, j: (i, 0)),
            pl.BlockSpec((1, D_MODEL), lambda i, j: (0, 0)),
            *[slab_spec(position) for position in range(8)],
            tab, tab, tab,
        ],
        out_specs=pl.BlockSpec((tm, IN_TILE), lambda i, j: (i, j)),
        out_shape=jax.ShapeDtypeStruct((m, N_SLABS * LANES), jnp.bfloat16),
        scratch_shapes=[pltpu.VMEM((tm, D_MODEL), jnp.bfloat16)],
        compiler_params=_cparams(("parallel", "arbitrary")),
        name="in_proj",
    )(x2, pre_w.reshape(1, D_MODEL), *([wt] * 8), cos, sa, sb)


ATTN_TQ = 256


ONES_ROWS = 16


def _transpose_v(v_ref, vt_ref):
    for h in range(vt_ref.shape[0]):
        vt_ref[h, 0:LANES, :] = v_ref[:, h * LANES:(h + 1) * LANES].astype(jnp.float32).T.astype(vt_ref.dtype)
        vt_ref[h, LANES:, :] = jnp.ones((ONES_ROWS, vt_ref.shape[2]), vt_ref.dtype)


ATTN_STREAMS = 4
ATTN_HEADS_PER_STEP = 2


def _tile_streams(n_tiles):
    assert n_tiles == 2 * ATTN_STREAMS
    return [[n_tiles - 1 - r, r] for r in range(ATTN_STREAMS)]


def _causal_attention(streams, tq, emit):
    key = lax.broadcasted_iota(jnp.int32, (tq, tq), 0)
    qry = lax.broadcasted_iota(jnp.int32, (tq, tq), 1)
    on_or_below_diag = key <= qry
    flat = [[(item, c) for item in stream for c in range(item[3] + 1)] for stream in streams]
    q_t = {}

    def scores(p, k):
        if k >= len(flat[p]):
            return None
        (load_q, load_k, _, i, tag), c = flat[p][k]
        if c == 0:
            q_t[tag] = load_q().astype(jnp.float32).T.astype(jnp.bfloat16)
        s = jnp.dot(load_k(c), q_t[tag], preferred_element_type=jnp.float32)
        if c == i:
            s = jnp.where(on_or_below_diag, s, NEG_INF)
        return s.reshape(tq // 8, 8, tq)

    m_run = [None] * len(streams)
    acc = [None] * len(streams)
    s_next = [scores(p, 0) for p in range(len(streams))]
    for k in range(max(len(f) for f in flat)):
        s_cur, s_next = s_next, [scores(p, k + 1) for p in range(len(streams))]
        for p in range(len(streams)):
            if s_cur[p] is None:
                continue
            (_, _, load_vt, i, tag), c = flat[p][k]
            m = jnp.max(jnp.max(s_cur[p], axis=0), axis=0, keepdims=True)
            if c > 0:
                m = jnp.maximum(m_run[p], m)
            pt = jnp.exp2(s_cur[p] - jnp.broadcast_to(m, (8, tq))[None]).reshape(tq, tq).astype(jnp.bfloat16)
            d = jnp.dot(load_vt(c), pt, preferred_element_type=jnp.float32)
            acc[p] = d if c == 0 else acc[p] * jnp.exp2(m_run[p] - m) + d
            m_run[p] = m
            if c == i:
                emit(tag, acc[p])


def _attn_scratch(seq):
    return [pltpu.VMEM((ATTN_HEADS_PER_STEP, LANES + ONES_ROWS, seq), jnp.bfloat16)]


def _diff_attn_kernel(q_ref, k_ref, v_ref, g_ref, lam_ref, subw_ref, o_ref, vt_ref, *, tq, lam_init):
    _transpose_v(v_ref, vt_ref)
    dv = LANES
    t = lam_ref[...]
    lam = (jnp.exp(jnp.sum(t[0:1] * t[1:2], axis=-1, keepdims=True))
           - jnp.exp(jnp.sum(t[2:3] * t[3:4], axis=-1, keepdims=True)) + lam_init)
    lane = lax.broadcasted_iota(jnp.int32, (tq, LANES), 1)

    def item(h, i, second):
        cols = slice(h * LANES, (h + 1) * LANES)

        def load_q():
            q = q_ref[i * tq:(i + 1) * tq, cols]
            keep = (lane >= 64) if second else (lane < 64)
            return jnp.where(keep, q, jnp.zeros_like(q))
        return (load_q, lambda c: k_ref[c * tq:(c + 1) * tq, cols], lambda c: vt_ref[h, :, c * tq:(c + 1) * tq],
                i, (h, i, second))

    streams = [[item(h, i, second) for h in range(ATTN_HEADS_PER_STEP) for i in tiles for second in (False, True)]
               for tiles in _tile_streams(q_ref.shape[0] // tq)]
    first = {}

    def emit(tag, acc):
        h, i, second = tag
        o_n = acc[0:dv] * (1.0 / acc[dv:dv + 1])
        if not second:
            first[h, i] = o_n
            return
        rows, cols = slice(i * tq, (i + 1) * tq), slice(h * LANES, (h + 1) * LANES)
        o = first.pop((h, i)) - lam * o_n
        ms = jnp.mean(o * o, axis=0, keepdims=True)
        o = (o * lax.rsqrt(ms + DA_SUBLN_EPS)).T
        o = o * subw_ref[...] * (1.0 - lam_init)
        o_ref[rows, cols] = (o * g_ref[rows, cols].astype(jnp.float32)).astype(o_ref.dtype)

    _causal_attention(streams, tq, emit)


def _diff_attn(proj, lam_rows, subln_w, batch, seq, lam_init):
    heads, hps = 4, ATTN_HEADS_PER_STEP
    slab = lambda first: pl.BlockSpec((seq, hps * LANES), lambda b, h: (b, first // hps + h))
    return pl.pallas_call(
        functools.partial(_diff_attn_kernel, tq=ATTN_TQ, lam_init=lam_init),
        grid=(batch, heads // hps),
        in_specs=[
            slab(SLAB_AQ), slab(SLAB_AK), slab(SLAB_AV), slab(SLAB_AG),
            pl.BlockSpec((4, 64), lambda b, h: (0, 0)),
            pl.BlockSpec((1, LANES), lambda b, h: (0, 0)),
        ],
        out_specs=pl.BlockSpec((seq, hps * LANES), lambda b, h: (b, h)),
        out_shape=jax.ShapeDtypeStruct((batch * seq, heads * LANES), jnp.bfloat16),
        scratch_shapes=_attn_scratch(seq),
        compiler_params=_cparams(("parallel", "parallel")),
        name="diff_attn",
    )(proj, proj, proj, proj, lam_rows, subln_w.reshape(1, LANES))


def _mla_attn_kernel(q_ref, k_ref, v_ref, *rest, tq):
    g_refs, (o_ref, vt_ref) = rest[:ATTN_HEADS_PER_STEP], rest[ATTN_HEADS_PER_STEP:]
    _transpose_v(v_ref, vt_ref)
    dv = LANES

    def item(h, i):
        cols = slice(h * 2 * LANES, (h + 1) * 2 * LANES)
        return (lambda: q_ref[i * tq:(i + 1) * tq, cols], lambda c: k_ref[c * tq:(c + 1) * tq, cols],
                lambda c: vt_ref[h, :, c * tq:(c + 1) * tq], i, (h, i))

    streams = [[item(h, i) for h in range(ATTN_HEADS_PER_STEP) for i in tiles]
               for tiles in _tile_streams(q_ref.shape[0] // tq)]

    def emit(tag, acc):
        h, i = tag
        rows = slice(i * tq, (i + 1) * tq)
        o = (acc[0:dv] * (1.0 / acc[dv:dv + 1])).T
        o_ref[rows, h * LANES:(h + 1) * LANES] = (o * g_refs[h][rows, :].astype(jnp.float32)).astype(o_ref.dtype)

    _causal_attention(streams, tq, emit)


def _mla_attn(qf, kf, vc, proj, batch, seq):
    heads, hps = 4, ATTN_HEADS_PER_STEP
    gate = lambda r: pl.BlockSpec((seq, LANES), lambda b, h: (b, SLAB_CG + hps * h + r))
    return pl.pallas_call(
        functools.partial(_mla_attn_kernel, tq=ATTN_TQ),
        grid=(batch, heads // hps),
        in_specs=[
            pl.BlockSpec((seq, hps * 2 * LANES), lambda b, h: (b, h)),
            pl.BlockSpec((seq, hps * 2 * LANES), lambda b, h: (b, h)),
            pl.BlockSpec((seq, hps * LANES), lambda b, h: (b, h)),
            *[gate(r) for r in range(hps)],
        ],
        out_specs=pl.BlockSpec((seq, hps * LANES), lambda b, h: (b, h)),
        out_shape=jax.ShapeDtypeStruct((batch * seq, heads * LANES), jnp.bfloat16),
        scratch_shapes=_attn_scratch(seq),
        compiler_params=_cparams(("parallel", "parallel")),
        name="mla_attn",
    )(qf, kf, vc, *([proj] * hps))


def _pad_w_uq(w):
    z = jnp.zeros((w.shape[0], 64), w.dtype)
    pieces = []
    for h in range(4):
        pieces += [w[:, h * 192:(h + 1) * 192], z]
    return jnp.concatenate(pieces, axis=1).astype(jnp.bfloat16)


def _reorder_w_ukv(w):
    k = [w[:, h * 256:h * 256 + 128] for h in range(4)]
    v = [w[:, h * 256 + 128:(h + 1) * 256] for h in range(4)]
    return jnp.concatenate(k + v, axis=1).astype(jnp.bfloat16)


def _mla_up_kernel(cq_ref, ckv_ref, ckr_ref, qnw_ref, kvnw_ref, wuq_ref, wukv_ref,
                   cos_ref, sa_ref, sb_ref, qf_ref, kf_ref, vc_ref):
    def norm(ref, w_ref):
        c = ref[...].astype(jnp.float32)
        ms = jnp.mean(c * c, axis=-1, keepdims=True)
        return (c * lax.rsqrt(ms + NORM_EPS) * w_ref[...]).astype(jnp.bfloat16)

    scale = 192 ** -0.5 * LOG2E
    q = jnp.dot(norm(cq_ref, qnw_ref), wuq_ref[...], preferred_element_type=jnp.float32)
    kv = jnp.dot(norm(ckv_ref, kvnw_ref), wukv_ref[...], preferred_element_type=jnp.float32)
    kr = ckr_ref[...]
    for h in range(4):
        nope = q[:, 2 * h * LANES:(2 * h + 1) * LANES]
        rope = _rope(q[:, (2 * h + 1) * LANES:(2 * h + 2) * LANES], cos_ref[...], sa_ref[...], sb_ref[...])
        qf_ref[:, 2 * h * LANES:(2 * h + 1) * LANES] = (nope * scale).astype(qf_ref.dtype)
        qf_ref[:, (2 * h + 1) * LANES:(2 * h + 2) * LANES] = (rope * scale).astype(qf_ref.dtype)
        kf_ref[:, 2 * h * LANES:(2 * h + 1) * LANES] = kv[:, h * LANES:(h + 1) * LANES].astype(kf_ref.dtype)
        kf_ref[:, (2 * h + 1) * LANES:(2 * h + 2) * LANES] = kr
    vc_ref[...] = kv[:, 4 * LANES:].astype(vc_ref.dtype)


def _mla_up(proj, qn_w, kvn_w, w_uq_p, w_ukv_r, cos, sa, sb):
    m = proj.shape[0]
    tm = 1024
    tab = pl.BlockSpec((tm, LANES), lambda i: (i, 0))
    full = lambda shape: pl.BlockSpec(shape, lambda i: (0, 0))
    return pl.pallas_call(
        _mla_up_kernel,
        grid=(m // tm,),
        in_specs=[
            pl.BlockSpec((tm, 4 * LANES), lambda i: (i, SLAB_CQ // 4)),
            pl.BlockSpec((tm, 4 * LANES), lambda i: (i, SLAB_CKV // 4)),
            pl.BlockSpec((tm, LANES), lambda i: (i, SLAB_CKR)),
            full((1, 512)), full((1, 512)), full((512, 1024)), full((512, 1024)),
            tab, tab, tab,
        ],
        out_specs=[
            pl.BlockSpec((tm, 1024), lambda i: (i, 0)),
            pl.BlockSpec((tm, 1024), lambda i: (i, 0)),
            pl.BlockSpec((tm, 512), lambda i: (i, 0)),
        ],
        out_shape=[
            jax.ShapeDtypeStruct((m, 1024), jnp.bfloat16),
            jax.ShapeDtypeStruct((m, 1024), jnp.bfloat16),
            jax.ShapeDtypeStruct((m, 512), jnp.bfloat16),
        ],
        compiler_params=_cparams(("parallel",)),
        name="mla_up",
    )(proj, proj, proj, qn_w.reshape(1, 512), kvn_w.reshape(1, 512), w_uq_p, w_ukv_r, cos, sa, sb)


def _swa_kernel(sink_ref, q0_ref, q1_ref, kc_ref, kp_ref, vc_ref, vp_ref, gate_ref, o_ref, *, blocks):
    w = SW_WINDOW
    q_refs = (q0_ref, q1_ref)
    first = pl.program_id(1) == 0
    lane = lax.broadcasted_iota(jnp.int32, ((blocks + 1) * w, LANES), 1)
    lo = lane < 64

    kt = jnp.concatenate([kp_ref[...], kc_ref[...]], axis=0).astype(jnp.float32)
    kr = pltpu.roll(kt, 64, 1)
    k_halves = []
    for g in range(2):
        a, b = (kt, kr) if g == 0 else (kr, kt)
        k_halves.append((jnp.where(lo, a, 0.0).astype(jnp.bfloat16), jnp.where(lo, 0.0, b).astype(jnp.bfloat16)))

    vt = jnp.concatenate([vp_ref[...], vc_ref[...]], axis=0).astype(jnp.float32).T
    zeros = jnp.zeros((64, 2 * w), jnp.bfloat16)
    ones = jnp.ones((ONES_ROWS, 2 * w), jnp.bfloat16)
    no_ones = jnp.zeros((ONES_ROWS, 2 * w), jnp.bfloat16)

    key = lax.broadcasted_iota(jnp.int32, (2 * w, 4 * w), 0)
    qry = lax.broadcasted_iota(jnp.int32, (2 * w, 4 * w), 1) & (w - 1)
    rel = qry + w - key
    in_band = (rel >= 0) & (rel < w)
    first_key = jnp.where(first, w, 0)
    bias = jnp.where(in_band, 0.0, NEG_INF)
    bias_first = jnp.where(in_band & (key >= first_key), 0.0, NEG_INF)

    def scores(n, g):
        k_lo, k_hi = k_halves[g]
        rows = slice(n * w, (n + 2) * w)
        kk = jnp.concatenate([k_lo[rows], k_hi[rows]], axis=0)
        qs = jnp.concatenate(
            [q_refs[g][n * w:(n + 1) * w, s * LANES:(s + 1) * LANES] for s in range(4)], axis=0)
        st = lax.dot_general(kk, qs, _NT, preferred_element_type=jnp.float32)
        ps, sink_terms = [], []
        for e in range(2):
            blk = st[e * 2 * w:(e + 1) * 2 * w] + (bias_first if n == 0 else bias)
            sink = jnp.concatenate(
                [jnp.full((1, w), sink_ref[8 * g + 2 * s + e] * LOG2E, jnp.float32) for s in range(4)], axis=1)
            m = jnp.maximum(jnp.max(blk, axis=0, keepdims=True), sink)
            ps.append(jnp.exp2(blk - m).astype(jnp.bfloat16))
            sink_terms.append(jnp.exp2(sink - m))
        return jnp.concatenate(ps, axis=0), sink_terms

    def outputs(n, g, pt, sink_terms):
        vg = vt[64 * g:64 * (g + 1), n * w:(n + 2) * w].astype(jnp.bfloat16)
        lhs = jnp.concatenate([
            jnp.concatenate([vg, zeros], axis=1), jnp.concatenate([zeros, vg], axis=1),
            jnp.concatenate([ones, no_ones], axis=1), jnp.concatenate([no_ones, ones], axis=1)], axis=0)
        acc = jnp.dot(lhs, pt, preferred_element_type=jnp.float32)
        den_even = acc[128:129] + sink_terms[0]
        den_odd = acc[128 + ONES_ROWS:129 + ONES_ROWS] + sink_terms[1]
        ot = jnp.concatenate([acc[0:64] * (1.0 / den_even), acc[64:128] * (1.0 / den_odd)], axis=0)
        for s in range(4):
            slab = 4 * g + s
            gate = gate_ref[n * w:(n + 1) * w, slab * LANES:(slab + 1) * LANES]
            o = ot[:, s * w:(s + 1) * w].T
            o_ref[n * w:(n + 1) * w, slab * LANES:(slab + 1) * LANES] = (
                o * gate.astype(jnp.float32)).astype(o_ref.dtype)

    pending = None
    for item in [(n, g) for n in range(blocks) for g in range(2)] + [None]:
        computed = scores(*item) if item is not None else None
        if pending is not None:
            outputs(*pending[0], *pending[1])
        pending = (item, computed)


def _swa(proj, sinks, batch, seq):
    w = SW_WINDOW
    blocks = 8
    rows = blocks * w
    steps = seq // rows
    nb = seq // w
    cur = lambda slab: pl.BlockSpec((rows, LANES), lambda b, i: (b * steps + i, slab))
    prev = lambda slab: pl.BlockSpec(
        (w, LANES), lambda b, i: (jnp.maximum(b * nb + i * blocks - 1, 0), slab))
    wide = lambda width, slab: pl.BlockSpec((rows, width * LANES), lambda b, i: (b * steps + i, slab // width))
    return pl.pallas_call(
        functools.partial(_swa_kernel, blocks=blocks),
        grid=(batch, steps),
        in_specs=[
            pl.BlockSpec(memory_space=pltpu.SMEM),
            wide(4, SLAB_BQ_LO), wide(4, SLAB_BQ_HI),
            cur(SLAB_BK), prev(SLAB_BK), cur(SLAB_BV), prev(SLAB_BV),
            wide(8, SLAB_BG),
        ],
        out_specs=pl.BlockSpec((rows, 1024), lambda b, i: (b * steps + i, 0)),
        out_shape=jax.ShapeDtypeStruct((batch * seq, 1024), jnp.bfloat16),
        compiler_params=_cparams(("parallel", "arbitrary")),
        name="swa",
    )(sinks.astype(jnp.float32), proj, proj, proj, proj, proj, proj, proj)


OUT_SUB_ROWS = 256


def _out_proj_kernel(ya_ref, yb_ref, yc_ref, wf_ref, x_ref, postw_ref, o_ref, w_ref):
    @pl.when(pl.program_id(0) == 0)
    def _():
        w_ref[...] = wf_ref[...].astype(w_ref.dtype)

    for r in range(0, o_ref.shape[0], OUT_SUB_ROWS):
        rows = slice(r, r + OUT_SUB_ROWS)
        out = jnp.dot(ya_ref[rows, :], w_ref[0:512, :], preferred_element_type=jnp.float32)
        out += jnp.dot(yb_ref[rows, :], w_ref[512:1536, :], preferred_element_type=jnp.float32)
        out += jnp.dot(yc_ref[rows, :], w_ref[1536:2048, :], preferred_element_type=jnp.float32)
        ms = jnp.mean(out * out, axis=-1, keepdims=True)
        o_ref[rows, :] = x_ref[rows, :] + out * lax.rsqrt(ms + NORM_EPS) * postw_ref[...]


def _out_proj(ya, yb, yc, w_out, layer, x2, post_w):
    m = x2.shape[0]
    tm = 512
    row = lambda width: pl.BlockSpec((tm, width), lambda i: (i, 0))
    return pl.pallas_call(
        _out_proj_kernel,
        grid=(m // tm,),
        in_specs=[
            row(512), row(1024), row(512),
            pl.BlockSpec((None, D_MODEL, D_MODEL), lambda i: (layer, 0, 0), pipeline_mode=pl.Buffered(1)),
            row(D_MODEL),
            pl.BlockSpec((1, D_MODEL), lambda i: (0, 0)),
        ],
        out_specs=row(D_MODEL),
        out_shape=jax.ShapeDtypeStruct((m, D_MODEL), jnp.float32),
        scratch_shapes=[pltpu.VMEM((D_MODEL, D_MODEL), jnp.bfloat16)],
        compiler_params=_cparams(("arbitrary",)),
        name="out_proj",
    )(ya, yb, yc, w_out, x2, post_w.reshape(1, D_MODEL))


def kernel(x, positions, pre_norm_w, post_norm_w, w_in, diff_lambda_q1, diff_lambda_k1, diff_lambda_q2,
           diff_lambda_k2, diff_subln_w, sink_logits, mla_q_norm_w, mla_kv_norm_w, w_uq, w_ukv, w_out):
    batch, seq, d = x.shape
    depth = w_in.shape[0]
    cos, sa, sb = _rope_tables(positions)
    x2 = x.reshape(batch * seq, d)
    for layer in range(depth):
        lam_init = 0.8 - 0.6 * math.exp(-0.3 * layer)
        lam_rows = jnp.stack([diff_lambda_q1[layer], diff_lambda_k1[layer],
                              diff_lambda_q2[layer], diff_lambda_k2[layer]]).astype(jnp.float32)
        proj = _in_proj(x2, pre_norm_w[layer], w_in, layer, cos, sa, sb)
        ya = _diff_attn(proj, lam_rows, diff_subln_w[layer], batch, seq, lam_init)
        yb = _swa(proj, sink_logits[layer], batch, seq)
        qf, kf, vc = _mla_up(proj, mla_q_norm_w[layer], mla_kv_norm_w[layer],
                             _pad_w_uq(w_uq[layer]), _reorder_w_ukv(w_ukv[layer]), cos, sa, sb)
        yc = _mla_attn(qf, kf, vc, proj, batch, seq)
        x2 = _out_proj(ya, yb, yc, w_out, layer, x2, post_norm_w[layer])
    return x2.reshape(batch, seq, d)
```

```python
import functools
import math

import jax
import jax.numpy as jnp
import numpy as np
from jax import lax
from jax.experimental import pallas as pl
from jax.experimental.pallas import tpu as pltpu

D_MODEL = 2048
ROPE_THETA = 10000.0
NORM_EPS = 1e-6
NEG_INF = -1e30
LOG2E = math.log2(math.e)
DA_SUBLN_EPS = 1e-5
SW_WINDOW = 128
LANES = 128

IN_TILES = (
    ((0, 1, 2, 3, 8, 9, 10, 11), ("rope_q",) * 4 + ("plain",) * 4),
    ((4, 5, 6, 7, 34, 35, 36, 37), ("rope",) * 4 + ("plain",) * 4),
    ((16, 17, 18, 19, 38, 39, 40, 41), ("rope_q",) * 4 + ("plain",) * 4),
    ((20, 21, 22, 23, 12, 13, 14, 15), ("rope_q",) * 4 + ("silu",) * 4),
    ((26, 27, 28, 29, 30, 31, 32, 33), ("silu",) * 8),
    ((42, 43, 44, 45, 46, 24, 25, 25), ("kr",) + ("gate_c",) * 4 + ("rope", "plain", "skip")),
)
IN_TILE = 8 * LANES
N_SLABS = 8 * len(IN_TILES)
SLAB_AQ, SLAB_AV, SLAB_AK, SLAB_CQ = 0, 4, 8, 12
SLAB_BQ_LO, SLAB_CKV, SLAB_BQ_HI, SLAB_AG = 16, 20, 24, 28
SLAB_BG, SLAB_CKR, SLAB_CG, SLAB_BK, SLAB_BV = 32, 40, 41, 45, 46

VMEM_LIMIT = 56 * 1024 * 1024

_NT = (((1,), (1,)), ((), ()))


def _cparams(sem):
    return pltpu.CompilerParams(dimension_semantics=sem, vmem_limit_bytes=VMEM_LIMIT)


def _rope_table_kernel(pos_ref, freq_ref, cos_ref, sa_ref, sb_ref):
    ang = pos_ref[...].astype(jnp.float32) * freq_ref[...]
    c, s = jnp.cos(ang), jnp.sin(ang)
    lane = lax.broadcasted_iota(jnp.int32, ang.shape, 1)
    first_half = (lane % 64) < 32
    cos_ref[...] = c
    sa_ref[...] = jnp.where(first_half, -s, 0.0)
    sb_ref[...] = jnp.where(first_half, 0.0, s)


def _rope_tables(positions):
    m = positions.size
    pos_b = jnp.broadcast_to(positions.reshape(m, 1), (m, LANES))
    inv_freq = jnp.power(ROPE_THETA, -jnp.arange(0, 64, 2, dtype=jnp.float32) / 64)
    freq = jnp.tile(inv_freq, 4).reshape(1, LANES)
    tm = 1024
    spec = pl.BlockSpec((tm, LANES), lambda i: (i, 0))
    return pl.pallas_call(
        _rope_table_kernel,
        grid=(m // tm,),
        in_specs=[spec, pl.BlockSpec((1, LANES), lambda i: (0, 0))],
        out_specs=[spec, spec, spec],
        out_shape=[jax.ShapeDtypeStruct((m, LANES), jnp.float32)] * 3,
        compiler_params=_cparams(("parallel",)),
        name="rope_tables",
    )(pos_b, freq)


def _rope(a, cos, sa, sb):
    return a * cos + pltpu.roll(a, 96, 1) * sa + pltpu.roll(a, 32, 1) * sb


def _silu(a):
    return a * (0.5 + 0.5 * jnp.tanh(0.5 * a))


def _in_proj_kernel(x_ref, prew_ref, *rest):
    w_refs, (cos_ref, sa_ref, sb_ref, o_ref, h_ref) = rest[:8], rest[8:]
    j, r = pl.program_id(1), pl.program_id(2)
    tm = x_ref.shape[0]

    @pl.when(j == 0)
    def _():
        x = x_ref[...]
        ms = jnp.mean(x * x, axis=-1, keepdims=True)
        h_ref[r] = (x * lax.rsqrt(ms + NORM_EPS) * prew_ref[...]).astype(jnp.bfloat16)

    def epilogue(kinds):
        w = jnp.concatenate([w_ref[...] for w_ref in w_refs], axis=0).astype(jnp.bfloat16)
        acc = lax.dot_general(h_ref[r], w, _NT, preferred_element_type=jnp.float32)
        slab = lambda s: acc[:, s * LANES:(s + 1) * LANES]
        lane = lax.broadcasted_iota(jnp.int32, (acc.shape[0], LANES), 1)
        rows = pl.ds(pl.multiple_of(r * tm, tm), tm)
        for s, kind in enumerate(kinds):
            a = slab(s)
            if kind == "skip":
                a = jnp.zeros_like(a)
            if kind in ("rope", "rope_q", "kr"):
                a = _rope(a, cos_ref[rows, :], sa_ref[rows, :], sb_ref[rows, :])
                if kind == "rope_q":
                    a = a * (0.125 * LOG2E)
            elif kind == "silu":
                a = _silu(a)
            elif kind == "gate_c":
                a = jnp.where(lane < 64, pltpu.roll(_silu(slab(s - 1)), 64, 1), pltpu.roll(_silu(a), 64, 1))
            o_ref[:, s * LANES:(s + 1) * LANES] = a.astype(o_ref.dtype)

    for t, (_, kinds) in enumerate(IN_TILES):
        pl.when(j == t)(functools.partial(epilogue, kinds))


def _in_proj(x2, pre_w, w_in, layer, cos, sa, sb):
    m = x2.shape[0]
    tm = 1024
    tab = pl.BlockSpec((2 * tm, LANES), lambda p, j, r: (p, 0))

    def slab_spec(position):
        sources = [slabs[position] for slabs, _ in IN_TILES]

        def index_map(p, j, r):
            source = sources[-1]
            for t in range(len(sources) - 2, -1, -1):
                source = jnp.where(j == t, sources[t], source)
            return layer, source, 0
        return pl.BlockSpec((None, LANES, D_MODEL), index_map)

    wt = jnp.swapaxes(w_in, 1, 2)
    return pl.pallas_call(
        _in_proj_kernel,
        grid=(m // (2 * tm), len(IN_TILES), 2),
        in_specs=[
            pl.BlockSpec((tm, D_MODEL), lambda p, j, r: (2 * p + jnp.where(j == 0, r, 1), 0)),
            pl.BlockSpec((1, D_MODEL), lambda p, j, r: (0, 0)),
            *[slab_spec(position) for position in range(8)],
            tab, tab, tab,
        ],
        out_specs=pl.BlockSpec((tm, IN_TILE), lambda p, j, r: (2 * p + r, j)),
        out_shape=jax.ShapeDtypeStruct((m, N_SLABS * LANES), jnp.bfloat16),
        scratch_shapes=[pltpu.VMEM((2, tm, D_MODEL), jnp.bfloat16)],
        compiler_params=_cparams(("parallel", "arbitrary", "arbitrary")),
        name="in_proj",
    )(x2, pre_w.reshape(1, D_MODEL), *([wt] * 8), cos, sa, sb)


ATTN_TQ = 256


ONES_ROWS = 16


def _transpose_v(v_ref, vt_ref):
    for h in range(vt_ref.shape[0]):
        vt_ref[h, 0:LANES, :] = v_ref[:, h * LANES:(h + 1) * LANES].astype(jnp.float32).T.astype(vt_ref.dtype)
        vt_ref[h, LANES:, :] = jnp.ones((ONES_ROWS, vt_ref.shape[2]), vt_ref.dtype)


ATTN_STREAMS = 4
ATTN_HEADS_PER_STEP = 2


def _tile_streams(n_tiles):
    assert n_tiles == 2 * ATTN_STREAMS
    return [[n_tiles - 1 - r, r] for r in range(ATTN_STREAMS)]


def _causal_attention(streams, tq, emit):
    key = lax.broadcasted_iota(jnp.int32, (tq, tq), 0)
    qry = lax.broadcasted_iota(jnp.int32, (tq, tq), 1)
    on_or_below_diag = key <= qry
    flat = [[(item, c) for item in stream for c in range(item[3] + 1)] for stream in streams]
    q_t = {}

    def scores(p, k):
        if k >= len(flat[p]):
            return None
        (load_q, load_k, _, i, tag), c = flat[p][k]
        if c == 0:
            q_t[tag] = load_q().astype(jnp.float32).T.astype(jnp.bfloat16)
        s = jnp.dot(load_k(c), q_t[tag], preferred_element_type=jnp.float32)
        if c == i:
            s = jnp.where(on_or_below_diag, s, NEG_INF)
        return s.reshape(tq // 8, 8, tq)

    m_run = [None] * len(streams)
    acc = [None] * len(streams)
    s_next = [scores(p, 0) for p in range(len(streams))]
    for k in range(max(len(f) for f in flat)):
        s_cur, s_next = s_next, [scores(p, k + 1) for p in range(len(streams))]
        for p in range(len(streams)):
            if s_cur[p] is None:
                continue
            (_, _, load_vt, i, tag), c = flat[p][k]
            m = jnp.max(jnp.max(s_cur[p], axis=0), axis=0, keepdims=True)
            if c > 0:
                m = jnp.maximum(m_run[p], m)
            pt = jnp.exp2(s_cur[p] - jnp.broadcast_to(m, (8, tq))[None]).reshape(tq, tq).astype(jnp.bfloat16)
            d = jnp.dot(load_vt(c), pt, preferred_element_type=jnp.float32)
            acc[p] = d if c == 0 else acc[p] * jnp.exp2(m_run[p] - m) + d
            m_run[p] = m
            if c == i:
                emit(tag, acc[p])


def _attn_scratch(seq):
    return [pltpu.VMEM((ATTN_HEADS_PER_STEP, LANES + ONES_ROWS, seq), jnp.bfloat16)]


def _diff_attn_kernel(q_ref, k_ref, v_ref, g_ref, lam_ref, subw_ref, o_ref, vt_ref, *, tq, lam_init):
    _transpose_v(v_ref, vt_ref)
    dv = LANES
    t = lam_ref[...]
    lam = (jnp.exp(jnp.sum(t[0:1] * t[1:2], axis=-1, keepdims=True))
           - jnp.exp(jnp.sum(t[2:3] * t[3:4], axis=-1, keepdims=True)) + lam_init)
    lane = lax.broadcasted_iota(jnp.int32, (tq, LANES), 1)

    def item(h, i, second):
        cols = slice(h * LANES, (h + 1) * LANES)

        def load_q():
            q = q_ref[i * tq:(i + 1) * tq, cols]
            keep = (lane >= 64) if second else (lane < 64)
            return jnp.where(keep, q, jnp.zeros_like(q))
        return (load_q, lambda c: k_ref[c * tq:(c + 1) * tq, cols], lambda c: vt_ref[h, :, c * tq:(c + 1) * tq],
                i, (h, i, second))

    streams = [[item(h, i, second) for h in range(ATTN_HEADS_PER_STEP) for i in tiles for second in (False, True)]
               for tiles in _tile_streams(q_ref.shape[0] // tq)]
    first = {}

    def emit(tag, acc):
        h, i, second = tag
        o_n = acc[0:dv] * (1.0 / acc[dv:dv + 1])
        if not second:
            first[h, i] = o_n
            return
        rows, cols = slice(i * tq, (i + 1) * tq), slice(h * LANES, (h + 1) * LANES)
        o = first.pop((h, i)) - lam * o_n
        ms = jnp.mean(o * o, axis=0, keepdims=True)
        o = (o * lax.rsqrt(ms + DA_SUBLN_EPS)).T
        o = o * subw_ref[...] * (1.0 - lam_init)
        o_ref[rows, cols] = (o * g_ref[rows, cols].astype(jnp.float32)).astype(o_ref.dtype)

    _causal_attention(streams, tq, emit)


def _diff_attn(proj, lam_rows, subln_w, batch, seq, lam_init):
    heads, hps = 4, ATTN_HEADS_PER_STEP
    slab = lambda first: pl.BlockSpec((seq, hps * LANES), lambda b, h: (b, first // hps + h))
    return pl.pallas_call(
        functools.partial(_diff_attn_kernel, tq=ATTN_TQ, lam_init=lam_init),
        grid=(batch, heads // hps),
        in_specs=[
            slab(SLAB_AQ), slab(SLAB_AK), slab(SLAB_AV), slab(SLAB_AG),
            pl.BlockSpec((4, 64), lambda b, h: (0, 0)),
            pl.BlockSpec((1, LANES), lambda b, h: (0, 0)),
        ],
        out_specs=pl.BlockSpec((seq, hps * LANES), lambda b, h: (b, h)),
        out_shape=jax.ShapeDtypeStruct((batch * seq, heads * LANES), jnp.bfloat16),
        scratch_shapes=_attn_scratch(seq),
        compiler_params=_cparams(("parallel", "parallel")),
        name="diff_attn",
    )(proj, proj, proj, proj, lam_rows, subln_w.reshape(1, LANES))


def _mla_attn_kernel(q_ref, k_ref, v_ref, *rest, tq):
    g_refs, (o_ref, vt_ref) = rest[:ATTN_HEADS_PER_STEP], rest[ATTN_HEADS_PER_STEP:]
    _transpose_v(v_ref, vt_ref)
    dv = LANES

    def item(h, i):
        cols = slice(h * 2 * LANES, (h + 1) * 2 * LANES)
        return (lambda: q_ref[i * tq:(i + 1) * tq, cols], lambda c: k_ref[c * tq:(c + 1) * tq, cols],
                lambda c: vt_ref[h, :, c * tq:(c + 1) * tq], i, (h, i))

    streams = [[item(h, i) for h in range(ATTN_HEADS_PER_STEP) for i in tiles]
               for tiles in _tile_streams(q_ref.shape[0] // tq)]

    def emit(tag, acc):
        h, i = tag
        rows = slice(i * tq, (i + 1) * tq)
        o = (acc[0:dv] * (1.0 / acc[dv:dv + 1])).T
        o_ref[rows, h * LANES:(h + 1) * LANES] = (o * g_refs[h][rows, :].astype(jnp.float32)).astype(o_ref.dtype)

    _causal_attention(streams, tq, emit)


def _mla_attn(qf, kf, vc, proj, batch, seq):
    heads, hps = 4, ATTN_HEADS_PER_STEP
    gate = lambda r: pl.BlockSpec((seq, LANES), lambda b, h: (b, SLAB_CG + hps * h + r))
    return pl.pallas_call(
        functools.partial(_mla_attn_kernel, tq=ATTN_TQ),
        grid=(batch, heads // hps),
        in_specs=[
            pl.BlockSpec((seq, hps * 2 * LANES), lambda b, h: (b, h)),
            pl.BlockSpec((seq, hps * 2 * LANES), lambda b, h: (b, h)),
            pl.BlockSpec((seq, hps * LANES), lambda b, h: (b, h)),
            *[gate(r) for r in range(hps)],
        ],
        out_specs=pl.BlockSpec((seq, hps * LANES), lambda b, h: (b, h)),
        out_shape=jax.ShapeDtypeStruct((batch * seq, heads * LANES), jnp.bfloat16),
        scratch_shapes=_attn_scratch(seq),
        compiler_params=_cparams(("parallel", "parallel")),
        name="mla_attn",
    )(qf, kf, vc, *([proj] * hps))


def _pad_w_uq(w):
    z = jnp.zeros((w.shape[0], 64), w.dtype)
    pieces = []
    for h in range(4):
        pieces += [w[:, h * 192:(h + 1) * 192], z]
    return jnp.concatenate(pieces, axis=1).astype(jnp.bfloat16)


def _reorder_w_ukv(w):
    k = [w[:, h * 256:h * 256 + 128] for h in range(4)]
    v = [w[:, h * 256 + 128:(h + 1) * 256] for h in range(4)]
    return jnp.concatenate(k + v, axis=1).astype(jnp.bfloat16)


def _mla_up_kernel(cq_ref, ckv_ref, ckr_ref, qnw_ref, kvnw_ref, wuq_ref, wukv_ref,
                   cos_ref, sa_ref, sb_ref, qf_ref, kf_ref, vc_ref):
    def norm(ref, w_ref):
        c = ref[...].astype(jnp.float32)
        ms = jnp.mean(c * c, axis=-1, keepdims=True)
        return (c * lax.rsqrt(ms + NORM_EPS) * w_ref[...]).astype(jnp.bfloat16)

    scale = 192 ** -0.5 * LOG2E
    q = jnp.dot(norm(cq_ref, qnw_ref), wuq_ref[...], preferred_element_type=jnp.float32)
    kv = jnp.dot(norm(ckv_ref, kvnw_ref), wukv_ref[...], preferred_element_type=jnp.float32)
    kr = ckr_ref[...]
    for h in range(4):
        nope = q[:, 2 * h * LANES:(2 * h + 1) * LANES]
        rope = _rope(q[:, (2 * h + 1) * LANES:(2 * h + 2) * LANES], cos_ref[...], sa_ref[...], sb_ref[...])
        qf_ref[:, 2 * h * LANES:(2 * h + 1) * LANES] = (nope * scale).astype(qf_ref.dtype)
        qf_ref[:, (2 * h + 1) * LANES:(2 * h + 2) * LANES] = (rope * scale).astype(qf_ref.dtype)
        kf_ref[:, 2 * h * LANES:(2 * h + 1) * LANES] = kv[:, h * LANES:(h + 1) * LANES].astype(kf_ref.dtype)
        kf_ref[:, (2 * h + 1) * LANES:(2 * h + 2) * LANES] = kr
    vc_ref[...] = kv[:, 4 * LANES:].astype(vc_ref.dtype)


def _mla_up(proj, qn_w, kvn_w, w_uq_p, w_ukv_r, cos, sa, sb):
    m = proj.shape[0]
    tm = 1024
    tab = pl.BlockSpec((tm, LANES), lambda i: (i, 0))
    full = lambda shape: pl.BlockSpec(shape, lambda i: (0, 0))
    return pl.pallas_call(
        _mla_up_kernel,
        grid=(m // tm,),
        in_specs=[
            pl.BlockSpec((tm, 4 * LANES), lambda i: (i, SLAB_CQ // 4)),
            pl.BlockSpec((tm, 4 * LANES), lambda i: (i, SLAB_CKV // 4)),
            pl.BlockSpec((tm, LANES), lambda i: (i, SLAB_CKR)),
            full((1, 512)), full((1, 512)), full((512, 1024)), full((512, 1024)),
            tab, tab, tab,
        ],
        out_specs=[
            pl.BlockSpec((tm, 1024), lambda i: (i, 0)),
            pl.BlockSpec((tm, 1024), lambda i: (i, 0)),
            pl.BlockSpec((tm, 512), lambda i: (i, 0)),
        ],
        out_shape=[
            jax.ShapeDtypeStruct((m, 1024), jnp.bfloat16),
            jax.ShapeDtypeStruct((m, 1024), jnp.bfloat16),
            jax.ShapeDtypeStruct((m, 512), jnp.bfloat16),
        ],
        compiler_params=_cparams(("parallel",)),
        name="mla_up",
    )(proj, proj, proj, qn_w.reshape(1, 512), kvn_w.reshape(1, 512), w_uq_p, w_ukv_r, cos, sa, sb)


def _swa_kernel(sink_ref, q0_ref, q1_ref, kc_ref, kp_ref, vc_ref, vp_ref, gate_ref, o_ref, *, blocks):
    w = SW_WINDOW
    q_refs = (q0_ref, q1_ref)
    first = pl.program_id(1) == 0
    lane = lax.broadcasted_iota(jnp.int32, ((blocks + 1) * w, LANES), 1)
    lo = lane < 64

    kt = jnp.concatenate([kp_ref[...], kc_ref[...]], axis=0).astype(jnp.float32)
    kr = pltpu.roll(kt, 64, 1)
    k_halves = []
    for g in range(2):
        a, b = (kt, kr) if g == 0 else (kr, kt)
        k_halves.append((jnp.where(lo, a, 0.0).astype(jnp.bfloat16), jnp.where(lo, 0.0, b).astype(jnp.bfloat16)))

    vt = jnp.concatenate([vp_ref[...], vc_ref[...]], axis=0).astype(jnp.float32).T
    zeros = jnp.zeros((64, 2 * w), jnp.bfloat16)
    ones = jnp.ones((ONES_ROWS, 2 * w), jnp.bfloat16)
    no_ones = jnp.zeros((ONES_ROWS, 2 * w), jnp.bfloat16)

    key = lax.broadcasted_iota(jnp.int32, (2 * w, 4 * w), 0)
    qry = lax.broadcasted_iota(jnp.int32, (2 * w, 4 * w), 1) & (w - 1)
    rel = qry + w - key
    in_band = (rel >= 0) & (rel < w)
    first_key = jnp.where(first, w, 0)
    bias = jnp.where(in_band, 0.0, NEG_INF)
    bias_first = jnp.where(in_band & (key >= first_key), 0.0, NEG_INF)

    def scores(n, g):
        k_lo, k_hi = k_halves[g]
        rows = slice(n * w, (n + 2) * w)
        kk = jnp.concatenate([k_lo[rows], k_hi[rows]], axis=0)
        qs = jnp.concatenate(
            [q_refs[g][n * w:(n + 1) * w, s * LANES:(s + 1) * LANES] for s in range(4)], axis=0)
        st = lax.dot_general(kk, qs, _NT, preferred_element_type=jnp.float32)
        ps, sink_terms = [], []
        for e in range(2):
            blk = st[e * 2 * w:(e + 1) * 2 * w] + (bias_first if n == 0 else bias)
            sink = jnp.concatenate(
                [jnp.full((1, w), sink_ref[8 * g + 2 * s + e] * LOG2E, jnp.float32) for s in range(4)], axis=1)
            m = jnp.maximum(jnp.max(blk, axis=0, keepdims=True), sink)
            ps.append(jnp.exp2(blk - m).astype(jnp.bfloat16))
            sink_terms.append(jnp.exp2(sink - m))
        return jnp.concatenate(ps, axis=0), sink_terms

    def outputs(n, g, pt, sink_terms):
        vg = vt[64 * g:64 * (g + 1), n * w:(n + 2) * w].astype(jnp.bfloat16)
        lhs = jnp.concatenate([
            jnp.concatenate([vg, zeros], axis=1), jnp.concatenate([zeros, vg], axis=1),
            jnp.concatenate([ones, no_ones], axis=1), jnp.concatenate([no_ones, ones], axis=1)], axis=0)
        acc = jnp.dot(lhs, pt, preferred_element_type=jnp.float32)
        den_even = acc[128:129] + sink_terms[0]
        den_odd = acc[128 + ONES_ROWS:129 + ONES_ROWS] + sink_terms[1]
        ot = jnp.concatenate([acc[0:64] * (1.0 / den_even), acc[64:128] * (1.0 / den_odd)], axis=0)
        for s in range(4):
            slab = 4 * g + s
            gate = gate_ref[n * w:(n + 1) * w, slab * LANES:(slab + 1) * LANES]
            o = ot[:, s * w:(s + 1) * w].T
            o_ref[n * w:(n + 1) * w, slab * LANES:(slab + 1) * LANES] = (
                o * gate.astype(jnp.float32)).astype(o_ref.dtype)

    pending = None
    for item in [(n, g) for n in range(blocks) for g in range(2)] + [None]:
        computed = scores(*item) if item is not None else None
        if pending is not None:
            outputs(*pending[0], *pending[1])
        pending = (item, computed)


def _swa(proj, sinks, batch, seq):
    w = SW_WINDOW
    blocks = 8
    rows = blocks * w
    steps = seq // rows
    nb = seq // w
    cur = lambda slab: pl.BlockSpec((rows, LANES), lambda b, i: (b * steps + i, slab))
    prev = lambda slab: pl.BlockSpec(
        (w, LANES), lambda b, i: (jnp.maximum(b * nb + i * blocks - 1, 0), slab))
    wide = lambda width, slab: pl.BlockSpec((rows, width * LANES), lambda b, i: (b * steps + i, slab // width))
    return pl.pallas_call(
        functools.partial(_swa_kernel, blocks=blocks),
        grid=(batch, steps),
        in_specs=[
            pl.BlockSpec(memory_space=pltpu.SMEM),
            wide(4, SLAB_BQ_LO), wide(4, SLAB_BQ_HI),
            cur(SLAB_BK), prev(SLAB_BK), cur(SLAB_BV), prev(SLAB_BV),
            wide(8, SLAB_BG),
        ],
        out_specs=pl.BlockSpec((rows, 1024), lambda b, i: (b * steps + i, 0)),
        out_shape=jax.ShapeDtypeStruct((batch * seq, 1024), jnp.bfloat16),
        compiler_params=_cparams(("parallel", "arbitrary")),
        name="swa",
    )(sinks.astype(jnp.float32), proj, proj, proj, proj, proj, proj, proj)


OUT_SUB_ROWS = 256


def _out_proj_kernel(ya_ref, yb_ref, yc_ref, wf_ref, x_ref, postw_ref, o_ref, w_ref):
    @pl.when(pl.program_id(0) == 0)
    def _():
        w_ref[...] = wf_ref[...].astype(w_ref.dtype)

    for r in range(0, o_ref.shape[0], OUT_SUB_ROWS):
        rows = slice(r, r + OUT_SUB_ROWS)
        out = jnp.dot(ya_ref[rows, :], w_ref[0:512, :], preferred_element_type=jnp.float32)
        out += jnp.dot(yb_ref[rows, :], w_ref[512:1536, :], preferred_element_type=jnp.float32)
        out += jnp.dot(yc_ref[rows, :], w_ref[1536:2048, :], preferred_element_type=jnp.float32)
        ms = jnp.mean(out * out, axis=-1, keepdims=True)
        o_ref[rows, :] = x_ref[rows, :] + out * lax.rsqrt(ms + NORM_EPS) * postw_ref[...]


def _out_proj(ya, yb, yc, w_out, layer, x2, post_w):
    m = x2.shape[0]
    tm = 512
    row = lambda width: pl.BlockSpec((tm, width), lambda i: (i, 0))
    return pl.pallas_call(
        _out_proj_kernel,
        grid=(m // tm,),
        in_specs=[
            row(512), row(1024), row(512),
            pl.BlockSpec((None, D_MODEL, D_MODEL), lambda i: (layer, 0, 0), pipeline_mode=pl.Buffered(1)),
            row(D_MODEL),
            pl.BlockSpec((1, D_MODEL), lambda i: (0, 0)),
        ],
        out_specs=row(D_MODEL),
        out_shape=jax.ShapeDtypeStruct((m, D_MODEL), jnp.float32),
        scratch_shapes=[pltpu.VMEM((D_MODEL, D_MODEL), jnp.bfloat16)],
        compiler_params=_cparams(("arbitrary",)),
        name="out_proj",
    )(ya, yb, yc, w_out, x2, post_w.reshape(1, D_MODEL))


def kernel(x, positions, pre_norm_w, post_norm_w, w_in, diff_lambda_q1, diff_lambda_k1, diff_lambda_q2,
           diff_lambda_k2, diff_subln_w, sink_logits, mla_q_norm_w, mla_kv_norm_w, w_uq, w_ukv, w_out):
    batch, seq, d = x.shape
    depth = w_in.shape[0]
    cos, sa, sb = _rope_tables(positions)
    x2 = x.reshape(batch * seq, d)
    for layer in range(depth):
        lam_init = 0.8 - 0.6 * math.exp(-0.3 * layer)
        lam_rows = jnp.stack([diff_lambda_q1[layer], diff_lambda_k1[layer],
                              diff_lambda_q2[layer], diff_lambda_k2[layer]]).astype(jnp.float32)
        proj = _in_proj(x2, pre_norm_w[layer], w_in, layer, cos, sa, sb)
        ya = _diff_attn(proj, lam_rows, diff_subln_w[layer], batch, seq, lam_init)
        yb = _swa(proj, sink_logits[layer], batch, seq)
        qf, kf, vc = _mla_up(proj, mla_q_norm_w[layer], mla_kv_norm_w[layer],
                             _pad_w_uq(w_uq[layer]), _reorder_w_ukv(w_ukv[layer]), cos, sa, sb)
        yc = _mla_attn(qf, kf, vc, proj, batch, seq)
        x2 = _out_proj(ya, yb, yc, w_out, layer, x2, post_norm_w[layer])
    return x2.reshape(batch, seq, d)
```

```python
import functools
import math

import jax
import jax.numpy as jnp
import numpy as np
from jax import lax
from jax.experimental import pallas as pl
from jax.experimental.pallas import tpu as pltpu

D_MODEL = 2048
ROPE_THETA = 10000.0
NORM_EPS = 1e-6
NEG_INF = -1e30
LOG2E = math.log2(math.e)
DA_SUBLN_EPS = 1e-5
SW_WINDOW = 128
LANES = 128

IN_TILES = (
    ((0, 1, 2, 3, 8, 9, 10, 11), ("rope_q",) * 4 + ("plain",) * 4),
    ((4, 5, 6, 7, 34, 35, 36, 37), ("rope",) * 4 + ("plain",) * 4),
    ((16, 17, 18, 19, 38, 39, 40, 41), ("rope_q",) * 4 + ("plain",) * 4),
    ((20, 21, 22, 23, 12, 13, 14, 15), ("rope_q",) * 4 + ("silu",) * 4),
    ((26, 27, 28, 29, 30, 31, 32, 33), ("silu",) * 8),
    ((42, 43, 44, 45, 46, 24, 25, 25), ("kr",) + ("gate_c",) * 4 + ("rope", "plain", "skip")),
)
IN_TILE = 8 * LANES
N_SLABS = 8 * len(IN_TILES)
SLAB_AQ, SLAB_AV, SLAB_AK, SLAB_CQ = 0, 4, 8, 12
SLAB_BQ_LO, SLAB_CKV, SLAB_BQ_HI, SLAB_AG = 16, 20, 24, 28
SLAB_BG, SLAB_CKR, SLAB_CG, SLAB_BK, SLAB_BV = 32, 40, 41, 45, 46

VMEM_LIMIT = 56 * 1024 * 1024

_NT = (((1,), (1,)), ((), ()))


def _cparams(sem):
    return pltpu.CompilerParams(dimension_semantics=sem, vmem_limit_bytes=VMEM_LIMIT)


def _rope_table_kernel(pos_ref, freq_ref, cos_ref, sa_ref, sb_ref):
    ang = pos_ref[...].astype(jnp.float32) * freq_ref[...]
    c, s = jnp.cos(ang), jnp.sin(ang)
    quarter = ang.shape[0]
    lane = lax.broadcasted_iota(jnp.int32, ang.shape, 1)
    first_half = (lane % 64) < 32
    for r in range(4):
        rows = slice(r * quarter, (r + 1) * quarter)
        cr = jnp.tile(c[:, 32 * r:32 * (r + 1)], (1, 4))
        sr = jnp.tile(s[:, 32 * r:32 * (r + 1)], (1, 4))
        cos_ref[rows, :] = cr
        sa_ref[rows, :] = jnp.where(first_half, -sr, 0.0)
        sb_ref[rows, :] = jnp.where(first_half, 0.0, sr)


def _rope_tables(positions):
    m = positions.size
    tm = 1024
    pos_p = jnp.repeat(positions.reshape(m // tm, 4, tm // 4).transpose(0, 2, 1), 32, axis=2).reshape(m // 4, LANES)
    inv_freq = jnp.power(ROPE_THETA, -jnp.arange(0, 64, 2, dtype=jnp.float32) / 64)
    freq = jnp.tile(inv_freq, 4).reshape(1, LANES)
    spec = pl.BlockSpec((tm, LANES), lambda i: (i, 0))
    return pl.pallas_call(
        _rope_table_kernel,
        grid=(m // tm,),
        in_specs=[pl.BlockSpec((tm // 4, LANES), lambda i: (i, 0)), pl.BlockSpec((1, LANES), lambda i: (0, 0))],
        out_specs=[spec, spec, spec],
        out_shape=[jax.ShapeDtypeStruct((m, LANES), jnp.float32)] * 3,
        compiler_params=_cparams(("parallel",)),
        name="rope_tables",
    )(pos_p, freq)


def _rope(a, cos, sa, sb):
    return a * cos + pltpu.roll(a, 96, 1) * sa + pltpu.roll(a, 32, 1) * sb


def _silu(a):
    return a * (0.5 + 0.5 * jnp.tanh(0.5 * a))


def _in_proj_kernel(x_ref, prew_ref, *rest):
    w_refs, (cos_ref, sa_ref, sb_ref, o_ref, h_ref) = rest[:8], rest[8:]
    j, r = pl.program_id(1), pl.program_id(2)
    tm = x_ref.shape[0]

    @pl.when(j == 0)
    def _():
        x = x_ref[...]
        ms = jnp.mean(x * x, axis=-1, keepdims=True)
        h_ref[r] = (x * lax.rsqrt(ms + NORM_EPS) * prew_ref[...]).astype(jnp.bfloat16)

    def epilogue(kinds):
        w = jnp.concatenate([w_ref[...] for w_ref in w_refs], axis=0).astype(jnp.bfloat16)
        acc = lax.dot_general(h_ref[r], w, _NT, preferred_element_type=jnp.float32)
        slab = lambda s: acc[:, s * LANES:(s + 1) * LANES]
        lane = lax.broadcasted_iota(jnp.int32, (acc.shape[0], LANES), 1)
        rows = pl.ds(pl.multiple_of(r * tm, tm), tm)
        for s, kind in enumerate(kinds):
            a = slab(s)
            if kind == "skip":
                a = jnp.zeros_like(a)
            if kind in ("rope", "rope_q", "kr"):
                a = _rope(a, cos_ref[rows, :], sa_ref[rows, :], sb_ref[rows, :])
                if kind == "rope_q":
                    a = a * (0.125 * LOG2E)
            elif kind == "silu":
                a = _silu(a)
            elif kind == "gate_c":
                a = jnp.where(lane < 64, pltpu.roll(_silu(slab(s - 1)), 64, 1), pltpu.roll(_silu(a), 64, 1))
            o_ref[:, s * LANES:(s + 1) * LANES] = a.astype(o_ref.dtype)

    for t, (_, kinds) in enumerate(IN_TILES):
        pl.when(j == t)(functools.partial(epilogue, kinds))


def _in_proj(x2, pre_w, w_in, layer, cos, sa, sb):
    m = x2.shape[0]
    tm = 1024
    tab = pl.BlockSpec((2 * tm, LANES), lambda p, j, r: (p, 0))

    def slab_spec(position):
        sources = [slabs[position] for slabs, _ in IN_TILES]

        def index_map(p, j, r):
            source = sources[-1]
            for t in range(len(sources) - 2, -1, -1):
                source = jnp.where(j == t, sources[t], source)
            return layer, source, 0
        return pl.BlockSpec((None, LANES, D_MODEL), index_map)

    wt = jnp.swapaxes(w_in, 1, 2)
    return pl.pallas_call(
        _in_proj_kernel,
        grid=(m // (2 * tm), len(IN_TILES), 2),
        in_specs=[
            pl.BlockSpec((tm, D_MODEL), lambda p, j, r: (2 * p + jnp.where(j == 0, r, 1), 0)),
            pl.BlockSpec((1, D_MODEL), lambda p, j, r: (0, 0)),
            *[slab_spec(position) for position in range(8)],
            tab, tab, tab,
        ],
        out_specs=pl.BlockSpec((tm, IN_TILE), lambda p, j, r: (2 * p + r, j)),
        out_shape=jax.ShapeDtypeStruct((m, N_SLABS * LANES), jnp.bfloat16),
        scratch_shapes=[pltpu.VMEM((2, tm, D_MODEL), jnp.bfloat16)],
        compiler_params=_cparams(("parallel", "arbitrary", "arbitrary")),
        name="in_proj",
    )(x2, pre_w.reshape(1, D_MODEL), *([wt] * 8), cos, sa, sb)


ATTN_TQ = 256


ONES_ROWS = 16


def _transpose_v(v_ref, vt_ref):
    for h in range(vt_ref.shape[0]):
        vt_ref[h, 0:LANES, :] = v_ref[:, h * LANES:(h + 1) * LANES].astype(jnp.float32).T.astype(vt_ref.dtype)
        vt_ref[h, LANES:, :] = jnp.ones((ONES_ROWS, vt_ref.shape[2]), vt_ref.dtype)


ATTN_STREAMS = 4
ATTN_HEADS_PER_STEP = 2


def _tile_streams(n_tiles):
    assert n_tiles == 2 * ATTN_STREAMS
    return [[n_tiles - 1 - r, r] for r in range(ATTN_STREAMS)]


def _causal_attention(streams, tq, emit):
    key = lax.broadcasted_iota(jnp.int32, (tq, tq), 0)
    qry = lax.broadcasted_iota(jnp.int32, (tq, tq), 1)
    on_or_below_diag = key <= qry
    flat = [[(item, c) for item in stream for c in range(item[3] + 1)] for stream in streams]
    q_t = {}

    def scores(p, k):
        if k >= len(flat[p]):
            return None
        (load_q, load_k, _, i, tag), c = flat[p][k]
        if c == 0:
            q_t[tag] = load_q().astype(jnp.float32).T.astype(jnp.bfloat16)
        s = jnp.dot(load_k(c), q_t[tag], preferred_element_type=jnp.float32)
        if c == i:
            s = jnp.where(on_or_below_diag, s, NEG_INF)
        return s.reshape(tq // 8, 8, tq)

    m_run = [None] * len(streams)
    acc = [None] * len(streams)
    s_next = [scores(p, 0) for p in range(len(streams))]
    for k in range(max(len(f) for f in flat)):
        s_cur, s_next = s_next, [scores(p, k + 1) for p in range(len(streams))]
        for p in range(len(streams)):
            if s_cur[p] is None:
                continue
            (_, _, load_vt, i, tag), c = flat[p][k]
            m = jnp.max(jnp.max(s_cur[p], axis=0), axis=0, keepdims=True)
            if c > 0:
                m = jnp.maximum(m_run[p], m)
            pt = jnp.exp2(s_cur[p] - jnp.broadcast_to(m, (8, tq))[None]).reshape(tq, tq).astype(jnp.bfloat16)
            d = jnp.dot(load_vt(c), pt, preferred_element_type=jnp.float32)
            acc[p] = d if c == 0 else acc[p] * jnp.exp2(m_run[p] - m) + d
            m_run[p] = m
            if c == i:
                emit(tag, acc[p])


def _attn_scratch(seq):
    return [pltpu.VMEM((ATTN_HEADS_PER_STEP, LANES + ONES_ROWS, seq), jnp.bfloat16)]


def _diff_attn_kernel(q_ref, k_ref, v_ref, g_ref, lam_ref, subw_ref, o_ref, vt_ref, *, tq, lam_init):
    _transpose_v(v_ref, vt_ref)
    dv = LANES
    t = lam_ref[...]
    lam = (jnp.exp(jnp.sum(t[0:1] * t[1:2], axis=-1, keepdims=True))
           - jnp.exp(jnp.sum(t[2:3] * t[3:4], axis=-1, keepdims=True)) + lam_init)
    lane = lax.broadcasted_iota(jnp.int32, (tq, LANES), 1)

    def item(h, i, second):
        cols = slice(h * LANES, (h + 1) * LANES)

        def load_q():
            q = q_ref[i * tq:(i + 1) * tq, cols]
            keep = (lane >= 64) if second else (lane < 64)
            return jnp.where(keep, q, jnp.zeros_like(q))
        return (load_q, lambda c: k_ref[c * tq:(c + 1) * tq, cols], lambda c: vt_ref[h, :, c * tq:(c + 1) * tq],
                i, (h, i, second))

    streams = [[item(h, i, second) for i in tiles for second in (False, True)]
               for h in range(ATTN_HEADS_PER_STEP) for tiles in _tile_streams(q_ref.shape[0] // tq)]
    first = {}

    def emit(tag, acc):
        h, i, second = tag
        o_n = acc[0:dv] * (1.0 / acc[dv:dv + 1])
        if not second:
            first[h, i] = o_n
            return
        rows, cols = slice(i * tq, (i + 1) * tq), slice(h * LANES, (h + 1) * LANES)
        o = first.pop((h, i)) - lam * o_n
        ms = jnp.mean(o * o, axis=0, keepdims=True)
        o = (o * lax.rsqrt(ms + DA_SUBLN_EPS)).T
        o = o * subw_ref[...] * (1.0 - lam_init)
        o_ref[rows, cols] = (o * g_ref[rows, cols].astype(jnp.float32)).astype(o_ref.dtype)

    _causal_attention(streams, tq, emit)


def _diff_attn(proj, lam_rows, subln_w, batch, seq, lam_init):
    heads, hps = 4, ATTN_HEADS_PER_STEP
    slab = lambda first: pl.BlockSpec((seq, hps * LANES), lambda b, h: (b, first // hps + h))
    return pl.pallas_call(
        functools.partial(_diff_attn_kernel, tq=ATTN_TQ, lam_init=lam_init),
        grid=(batch, heads // hps),
        in_specs=[
            slab(SLAB_AQ), slab(SLAB_AK), slab(SLAB_AV), slab(SLAB_AG),
            pl.BlockSpec((4, 64), lambda b, h: (0, 0)),
            pl.BlockSpec((1, LANES), lambda b, h: (0, 0)),
        ],
        out_specs=pl.BlockSpec((seq, hps * LANES), lambda b, h: (b, h)),
        out_shape=jax.ShapeDtypeStruct((batch * seq, heads * LANES), jnp.bfloat16),
        scratch_shapes=_attn_scratch(seq),
        compiler_params=_cparams(("parallel", "parallel")),
        name="diff_attn",
    )(proj, proj, proj, proj, lam_rows, subln_w.reshape(1, LANES))


def _mla_attn_kernel(q_ref, k_ref, v_ref, *rest, tq):
    g_refs, (o_ref, vt_ref) = rest[:ATTN_HEADS_PER_STEP], rest[ATTN_HEADS_PER_STEP:]
    _transpose_v(v_ref, vt_ref)
    dv = LANES

    def item(h, i):
        cols = slice(h * 2 * LANES, (h + 1) * 2 * LANES)
        return (lambda: q_ref[i * tq:(i + 1) * tq, cols], lambda c: k_ref[c * tq:(c + 1) * tq, cols],
                lambda c: vt_ref[h, :, c * tq:(c + 1) * tq], i, (h, i))

    streams = [[item(h, i) for i in tiles]
               for h in range(ATTN_HEADS_PER_STEP) for tiles in _tile_streams(q_ref.shape[0] // tq)]

    def emit(tag, acc):
        h, i = tag
        rows = slice(i * tq, (i + 1) * tq)
        o = (acc[0:dv] * (1.0 / acc[dv:dv + 1])).T
        o_ref[rows, h * LANES:(h + 1) * LANES] = (o * g_refs[h][rows, :].astype(jnp.float32)).astype(o_ref.dtype)

    _causal_attention(streams, tq, emit)


def _mla_attn(qf, kf, vc, proj, batch, seq):
    heads, hps = 4, ATTN_HEADS_PER_STEP
    gate = lambda r: pl.BlockSpec((seq, LANES), lambda b, h: (b, SLAB_CG + hps * h + r))
    return pl.pallas_call(
        functools.partial(_mla_attn_kernel, tq=ATTN_TQ),
        grid=(batch, heads // hps),
        in_specs=[
            pl.BlockSpec((seq, hps * 2 * LANES), lambda b, h: (b, h)),
            pl.BlockSpec((seq, hps * 2 * LANES), lambda b, h: (b, h)),
            pl.BlockSpec((seq, hps * LANES), lambda b, h: (b, h)),
            *[gate(r) for r in range(hps)],
        ],
        out_specs=pl.BlockSpec((seq, hps * LANES), lambda b, h: (b, h)),
        out_shape=jax.ShapeDtypeStruct((batch * seq, heads * LANES), jnp.bfloat16),
        scratch_shapes=_attn_scratch(seq),
        compiler_params=_cparams(("parallel", "parallel")),
        name="mla_attn",
    )(qf, kf, vc, *([proj] * hps))


def _pad_w_uq(w):
    z = jnp.zeros((w.shape[0], 64), w.dtype)
    pieces = []
    for h in range(4):
        pieces += [w[:, h * 192:(h + 1) * 192], z]
    return jnp.concatenate(pieces, axis=1).astype(jnp.bfloat16)


def _reorder_w_ukv(w):
    k = [w[:, h * 256:h * 256 + 128] for h in range(4)]
    v = [w[:, h * 256 + 128:(h + 1) * 256] for h in range(4)]
    return jnp.concatenate(k + v, axis=1).astype(jnp.bfloat16)


def _mla_up_kernel(cq_ref, ckv_ref, ckr_ref, qnw_ref, kvnw_ref, wuq_ref, wukv_ref,
                   cos_ref, sa_ref, sb_ref, qf_ref, kf_ref, vc_ref):
    def norm(ref, w_ref):
        c = ref[...].astype(jnp.float32)
        ms = jnp.mean(c * c, axis=-1, keepdims=True)
        return (c * lax.rsqrt(ms + NORM_EPS) * w_ref[...]).astype(jnp.bfloat16)

    scale = 192 ** -0.5 * LOG2E
    q = jnp.dot(norm(cq_ref, qnw_ref), wuq_ref[...], preferred_element_type=jnp.float32)
    kv = jnp.dot(norm(ckv_ref, kvnw_ref), wukv_ref[...], preferred_element_type=jnp.float32)
    kr = ckr_ref[...]
    for h in range(4):
        nope = q[:, 2 * h * LANES:(2 * h + 1) * LANES]
        rope = _rope(q[:, (2 * h + 1) * LANES:(2 * h + 2) * LANES], cos_ref[...], sa_ref[...], sb_ref[...])
        qf_ref[:, 2 * h * LANES:(2 * h + 1) * LANES] = (nope * scale).astype(qf_ref.dtype)
        qf_ref[:, (2 * h + 1) * LANES:(2 * h + 2) * LANES] = (rope * scale).astype(qf_ref.dtype)
        kf_ref[:, 2 * h * LANES:(2 * h + 1) * LANES] = kv[:, h * LANES:(h + 1) * LANES].astype(kf_ref.dtype)
        kf_ref[:, (2 * h + 1) * LANES:(2 * h + 2) * LANES] = kr
    vc_ref[...] = kv[:, 4 * LANES:].astype(vc_ref.dtype)


def _mla_up(proj, qn_w, kvn_w, w_uq_p, w_ukv_r, cos, sa, sb):
    m = proj.shape[0]
    tm = 1024
    tab = pl.BlockSpec((tm, LANES), lambda i: (i, 0))
    full = lambda shape: pl.BlockSpec(shape, lambda i: (0, 0))
    return pl.pallas_call(
        _mla_up_kernel,
        grid=(m // tm,),
        in_specs=[
            pl.BlockSpec((tm, 4 * LANES), lambda i: (i, SLAB_CQ // 4)),
            pl.BlockSpec((tm, 4 * LANES), lambda i: (i, SLAB_CKV // 4)),
            pl.BlockSpec((tm, LANES), lambda i: (i, SLAB_CKR)),
            full((1, 512)), full((1, 512)), full((512, 1024)), full((512, 1024)),
            tab, tab, tab,
        ],
        out_specs=[
            pl.BlockSpec((tm, 1024), lambda i: (i, 0)),
            pl.BlockSpec((tm, 1024), lambda i: (i, 0)),
            pl.BlockSpec((tm, 512), lambda i: (i, 0)),
        ],
        out_shape=[
            jax.ShapeDtypeStruct((m, 1024), jnp.bfloat16),
            jax.ShapeDtypeStruct((m, 1024), jnp.bfloat16),
            jax.ShapeDtypeStruct((m, 512), jnp.bfloat16),
        ],
        compiler_params=_cparams(("parallel",)),
        name="mla_up",
    )(proj, proj, proj, qn_w.reshape(1, 512), kvn_w.reshape(1, 512), w_uq_p, w_ukv_r, cos, sa, sb)


def _swa_kernel(sink_ref, q0_ref, q1_ref, kc_ref, kp_ref, vc_ref, vp_ref, gate_ref, o_ref, *, blocks):
    w = SW_WINDOW
    q_refs = (q0_ref, q1_ref)
    first = pl.program_id(1) == 0
    lane = lax.broadcasted_iota(jnp.int32, ((blocks + 1) * w, LANES), 1)
    lo = lane < 64

    kt = jnp.concatenate([kp_ref[...], kc_ref[...]], axis=0).astype(jnp.float32)
    kr = pltpu.roll(kt, 64, 1)
    k_halves = []
    for g in range(2):
        a, b = (kt, kr) if g == 0 else (kr, kt)
        k_halves.append((jnp.where(lo, a, 0.0).astype(jnp.bfloat16), jnp.where(lo, 0.0, b).astype(jnp.bfloat16)))

    vt = jnp.concatenate([vp_ref[...], vc_ref[...]], axis=0).astype(jnp.float32).T
    zeros = jnp.zeros((64, 2 * w), jnp.bfloat16)
    ones = jnp.ones((ONES_ROWS, 2 * w), jnp.bfloat16)
    no_ones = jnp.zeros((ONES_ROWS, 2 * w), jnp.bfloat16)

    key = lax.broadcasted_iota(jnp.int32, (2 * w, 4 * w), 0)
    qry = lax.broadcasted_iota(jnp.int32, (2 * w, 4 * w), 1) & (w - 1)
    rel = qry + w - key
    in_band = (rel >= 0) & (rel < w)
    first_key = jnp.where(first, w, 0)
    bias = jnp.where(in_band, 0.0, NEG_INF)
    bias_first = jnp.where(in_band & (key >= first_key), 0.0, NEG_INF)

    def scores(n, g):
        k_lo, k_hi = k_halves[g]
        rows = slice(n * w, (n + 2) * w)
        kk = jnp.concatenate([k_lo[rows], k_hi[rows]], axis=0)
        qs = jnp.concatenate(
            [q_refs[g][n * w:(n + 1) * w, s * LANES:(s + 1) * LANES] for s in range(4)], axis=0)
        st = lax.dot_general(kk, qs, _NT, preferred_element_type=jnp.float32)
        ps, sink_terms = [], []
        for e in range(2):
            blk = st[e * 2 * w:(e + 1) * 2 * w] + (bias_first if n == 0 else bias)
            sink = jnp.concatenate(
                [jnp.full((1, w), sink_ref[8 * g + 2 * s + e] * LOG2E, jnp.float32) for s in range(4)], axis=1)
            m = jnp.maximum(jnp.max(blk, axis=0, keepdims=True), sink)
            ps.append(jnp.exp2(blk - m).astype(jnp.bfloat16))
            sink_terms.append(jnp.exp2(sink - m))
        return jnp.concatenate(ps, axis=0), sink_terms

    def outputs(n, g, pt, sink_terms):
        vg = vt[64 * g:64 * (g + 1), n * w:(n + 2) * w].astype(jnp.bfloat16)
        lhs = jnp.concatenate([
            jnp.concatenate([vg, zeros], axis=1), jnp.concatenate([zeros, vg], axis=1),
            jnp.concatenate([ones, no_ones], axis=1), jnp.concatenate([no_ones, ones], axis=1)], axis=0)
        acc = jnp.dot(lhs, pt, preferred_element_type=jnp.float32)
        den_even = acc[128:129] + sink_terms[0]
        den_odd = acc[128 + ONES_ROWS:129 + ONES_ROWS] + sink_terms[1]
        ot = jnp.concatenate([acc[0:64] * (1.0 / den_even), acc[64:128] * (1.0 / den_odd)], axis=0)
        for s in range(4):
            slab = 4 * g + s
            gate = gate_ref[n * w:(n + 1) * w, slab * LANES:(slab + 1) * LANES]
            o = ot[:, s * w:(s + 1) * w].T
            o_ref[n * w:(n + 1) * w, slab * LANES:(slab + 1) * LANES] = (
                o * gate.astype(jnp.float32)).astype(o_ref.dtype)

    pending = None
    for item in [(n, g) for n in range(blocks) for g in range(2)] + [None]:
        computed = scores(*item) if item is not None else None
        if pending is not None:
            outputs(*pending[0], *pending[1])
        pending = (item, computed)


def _swa(proj, sinks, batch, seq):
    w = SW_WINDOW
    blocks = 8
    rows = blocks * w
    steps = seq // rows
    nb = seq // w
    cur = lambda slab: pl.BlockSpec((rows, LANES), lambda b, i: (b * steps + i, slab))
    prev = lambda slab: pl.BlockSpec(
        (w, LANES), lambda b, i: (jnp.maximum(b * nb + i * blocks - 1, 0), slab))
    wide = lambda width, slab: pl.BlockSpec((rows, width * LANES), lambda b, i: (b * steps + i, slab // width))
    return pl.pallas_call(
        functools.partial(_swa_kernel, blocks=blocks),
        grid=(batch, steps),
        in_specs=[
            pl.BlockSpec(memory_space=pltpu.SMEM),
            wide(4, SLAB_BQ_LO), wide(4, SLAB_BQ_HI),
            cur(SLAB_BK), prev(SLAB_BK), cur(SLAB_BV), prev(SLAB_BV),
            wide(8, SLAB_BG),
        ],
        out_specs=pl.BlockSpec((rows, 1024), lambda b, i: (b * steps + i, 0)),
        out_shape=jax.ShapeDtypeStruct((batch * seq, 1024), jnp.bfloat16),
        compiler_params=_cparams(("parallel", "arbitrary")),
        name="swa",
    )(sinks.astype(jnp.float32), proj, proj, proj, proj, proj, proj, proj)


OUT_SUB_ROWS = 256


def _out_proj_kernel(ya_ref, yb_ref, yc_ref, wf_ref, x_ref, postw_ref, o_ref, w_ref):
    @pl.when(pl.program_id(0) == 0)
    def _():
        w_ref[...] = wf_ref[...].astype(w_ref.dtype)

    for r in range(0, o_ref.shape[0], OUT_SUB_ROWS):
        rows = slice(r, r + OUT_SUB_ROWS)
        out = jnp.dot(ya_ref[rows, :], w_ref[0:512, :], preferred_element_type=jnp.float32)
        out += jnp.dot(yb_ref[rows, :], w_ref[512:1536, :], preferred_element_type=jnp.float32)
        out += jnp.dot(yc_ref[rows, :], w_ref[1536:2048, :], preferred_element_type=jnp.float32)
        ms = jnp.mean(out * out, axis=-1, keepdims=True)
        o_ref[rows, :] = x_ref[rows, :] + out * lax.rsqrt(ms + NORM_EPS) * postw_ref[...]


def _out_proj(ya, yb, yc, w_out, layer, x2, post_w):
    m = x2.shape[0]
    tm = 512
    row = lambda width: pl.BlockSpec((tm, width), lambda i: (i, 0))
    return pl.pallas_call(
        _out_proj_kernel,
        grid=(m // tm,),
        in_specs=[
            row(512), row(1024), row(512),
            pl.BlockSpec((None, D_MODEL, D_MODEL), lambda i: (layer, 0, 0), pipeline_mode=pl.Buffered(1)),
            row(D_MODEL),
            pl.BlockSpec((1, D_MODEL), lambda i: (0, 0)),
        ],
        out_specs=row(D_MODEL),
        out_shape=jax.ShapeDtypeStruct((m, D_MODEL), jnp.float32),
        scratch_shapes=[pltpu.VMEM((D_MODEL, D_MODEL), jnp.bfloat16)],
        compiler_params=_cparams(("arbitrary",)),
        name="out_proj",
    )(ya, yb, yc, w_out, x2, post_w.reshape(1, D_MODEL))


def kernel(x, positions, pre_norm_w, post_norm_w, w_in, diff_lambda_q1, diff_lambda_k1, diff_lambda_q2,
           diff_lambda_k2, diff_subln_w, sink_logits, mla_q_norm_w, mla_kv_norm_w, w_uq, w_ukv, w_out):
    batch, seq, d = x.shape
    depth = w_in.shape[0]
    cos, sa, sb = _rope_tables(positions)
    x2 = x.reshape(batch * seq, d)
    for layer in range(depth):
        lam_init = 0.8 - 0.6 * math.exp(-0.3 * layer)
        lam_rows = jnp.stack([diff_lambda_q1[layer], diff_lambda_k1[layer],
                              diff_lambda_q2[layer], diff_lambda_k2[layer]]).astype(jnp.float32)
        proj = _in_proj(x2, pre_norm_w[layer], w_in, layer, cos, sa, sb)
        ya = _diff_attn(proj, lam_rows, diff_subln_w[layer], batch, seq, lam_init)
        yb = _swa(proj, sink_logits[layer], batch, seq)
        qf, kf, vc = _mla_up(proj, mla_q_norm_w[layer], mla_kv_norm_w[layer],
                             _pad_w_uq(w_uq[layer]), _reorder_w_ukv(w_ukv[layer]), cos, sa, sb)
        yc = _mla_attn(qf, kf, vc, proj, batch, seq)
        x2 = _out_proj(ya, yb, yc, w_out, layer, x2, post_norm_w[layer])
    return x2.reshape(batch, seq, d)
```

```python
import functools
import math

import jax
import jax.numpy as jnp
import numpy as np
from jax import lax
from jax.experimental import pallas as pl
from jax.experimental.pallas import tpu as pltpu

D_MODEL = 2048
ROPE_THETA = 10000.0
NORM_EPS = 1e-6
NEG_INF = -1e30
LOG2E = math.log2(math.e)
DA_SUBLN_EPS = 1e-5
SW_WINDOW = 128
LANES = 128

IN_TILES = (
    ((0, 1, 2, 3, 8, 9, 10, 11), ("rope_q",) * 4 + ("plain",) * 4),
    ((4, 5, 6, 7, 34, 35, 36, 37), ("rope",) * 4 + ("plain",) * 4),
    ((16, 17, 18, 19, 38, 39, 40, 41), ("rope_q",) * 4 + ("plain",) * 4),
    ((20, 21, 22, 23, 12, 13, 14, 15), ("rope_q",) * 4 + ("silu",) * 4),
    ((26, 27, 28, 29, 30, 31, 32, 33), ("silu",) * 8),
    ((42, 43, 44, 45, 46, 24, 25, 25), ("kr",) + ("gate_c",) * 4 + ("rope", "plain", "skip")),
)
IN_TILE = 8 * LANES
N_SLABS = 8 * len(IN_TILES)
SLAB_AQ, SLAB_AV, SLAB_AK, SLAB_CQ = 0, 4, 8, 12
SLAB_BQ_LO, SLAB_CKV, SLAB_BQ_HI, SLAB_AG = 16, 20, 24, 28
SLAB_BG, SLAB_CKR, SLAB_CG, SLAB_BK, SLAB_BV = 32, 40, 41, 45, 46

VMEM_LIMIT = 56 * 1024 * 1024

_NT = (((1,), (1,)), ((), ()))


def _cparams(sem):
    return pltpu.CompilerParams(dimension_semantics=sem, vmem_limit_bytes=VMEM_LIMIT)


def _rope_table_kernel(pos_ref, freq_ref, cos_ref, sa_ref, sb_ref):
    ang = pos_ref[...].astype(jnp.float32) * freq_ref[...]
    c, s = jnp.cos(ang), jnp.sin(ang)
    quarter = ang.shape[0]
    lane = lax.broadcasted_iota(jnp.int32, ang.shape, 1)
    first_half = (lane % 64) < 32
    for r in range(4):
        rows = slice(r * quarter, (r + 1) * quarter)
        cr = jnp.tile(c[:, 32 * r:32 * (r + 1)], (1, 4))
        sr = jnp.tile(s[:, 32 * r:32 * (r + 1)], (1, 4))
        cos_ref[rows, :] = cr
        sa_ref[rows, :] = jnp.where(first_half, -sr, 0.0)
        sb_ref[rows, :] = jnp.where(first_half, 0.0, sr)


def _rope_tables(positions):
    m = positions.size
    tm = 1024
    pos_p = jnp.repeat(positions.reshape(m // tm, 4, tm // 4).transpose(0, 2, 1), 32, axis=2).reshape(m // 4, LANES)
    inv_freq = jnp.power(ROPE_THETA, -jnp.arange(0, 64, 2, dtype=jnp.float32) / 64)
    freq = jnp.tile(inv_freq, 4).reshape(1, LANES)
    spec = pl.BlockSpec((tm, LANES), lambda i: (i, 0))
    return pl.pallas_call(
        _rope_table_kernel,
        grid=(m // tm,),
        in_specs=[pl.BlockSpec((tm // 4, LANES), lambda i: (i, 0)), pl.BlockSpec((1, LANES), lambda i: (0, 0))],
        out_specs=[spec, spec, spec],
        out_shape=[jax.ShapeDtypeStruct((m, LANES), jnp.float32)] * 3,
        compiler_params=_cparams(("parallel",)),
        name="rope_tables",
    )(pos_p, freq)


def _rope(a, cos, sa, sb):
    return a * cos + pltpu.roll(a, 96, 1) * sa + pltpu.roll(a, 32, 1) * sb


def _silu(a):
    return a * (0.5 + 0.5 * jnp.tanh(0.5 * a))


def _in_proj_kernel(x_ref, prew_ref, *rest):
    w_refs, (cos_ref, sa_ref, sb_ref, o_ref, h_ref) = rest[:8], rest[8:]
    j, r = pl.program_id(1), pl.program_id(2)
    tm = x_ref.shape[0]

    @pl.when(j == 0)
    def _():
        x = x_ref[...]
        ms = jnp.mean(x * x, axis=-1, keepdims=True)
        h_ref[r] = (x * lax.rsqrt(ms + NORM_EPS) * prew_ref[...]).astype(jnp.bfloat16)

    def epilogue(kinds):
        w = jnp.concatenate([w_ref[...] for w_ref in w_refs], axis=0).astype(jnp.bfloat16)
        acc = lax.dot_general(h_ref[r], w, _NT, preferred_element_type=jnp.float32)
        slab = lambda s: acc[:, s * LANES:(s + 1) * LANES]
        lane = lax.broadcasted_iota(jnp.int32, (acc.shape[0], LANES), 1)
        rows = pl.ds(pl.multiple_of(r * tm, tm), tm)
        for s, kind in enumerate(kinds):
            a = slab(s)
            if kind == "skip":
                a = jnp.zeros_like(a)
            if kind in ("rope", "rope_q", "kr"):
                a = _rope(a, cos_ref[rows, :], sa_ref[rows, :], sb_ref[rows, :])
                if kind == "rope_q":
                    a = a * (0.125 * LOG2E)
            elif kind == "silu":
                a = _silu(a)
            elif kind == "gate_c":
                a = jnp.where(lane < 64, pltpu.roll(_silu(slab(s - 1)), 64, 1), pltpu.roll(_silu(a), 64, 1))
            o_ref[:, s * LANES:(s + 1) * LANES] = a.astype(o_ref.dtype)

    for t, (_, kinds) in enumerate(IN_TILES):
        pl.when(j == t)(functools.partial(epilogue, kinds))


def _in_proj(x2, pre_w, w_in, layer, cos, sa, sb):
    m = x2.shape[0]
    tm = 1024
    tab = pl.BlockSpec((2 * tm, LANES), lambda p, j, r: (p, 0))

    def slab_spec(position):
        sources = [slabs[position] for slabs, _ in IN_TILES]

        def index_map(p, j, r):
            source = sources[-1]
            for t in range(len(sources) - 2, -1, -1):
                source = jnp.where(j == t, sources[t], source)
            return layer, source, 0
        return pl.BlockSpec((None, LANES, D_MODEL), index_map)

    wt = jnp.swapaxes(w_in, 1, 2)
    return pl.pallas_call(
        _in_proj_kernel,
        grid=(m // (2 * tm), len(IN_TILES), 2),
        in_specs=[
            pl.BlockSpec((tm, D_MODEL), lambda p, j, r: (2 * p + jnp.where(j == 0, r, 1), 0)),
            pl.BlockSpec((1, D_MODEL), lambda p, j, r: (0, 0)),
            *[slab_spec(position) for position in range(8)],
            tab, tab, tab,
        ],
        out_specs=pl.BlockSpec((tm, IN_TILE), lambda p, j, r: (2 * p + r, j)),
        out_shape=jax.ShapeDtypeStruct((m, N_SLABS * LANES), jnp.bfloat16),
        scratch_shapes=[pltpu.VMEM((2, tm, D_MODEL), jnp.bfloat16)],
        compiler_params=_cparams(("parallel", "arbitrary", "arbitrary")),
        name="in_proj",
    )(x2, pre_w.reshape(1, D_MODEL), *([wt] * 8), cos, sa, sb)


ATTN_TQ = 256


ONES_ROWS = 16


def _transpose_v(v_ref, vt_ref):
    for h in range(vt_ref.shape[0]):
        vt_ref[h, 0:LANES, :] = v_ref[:, h * LANES:(h + 1) * LANES].astype(jnp.float32).T.astype(vt_ref.dtype)
        vt_ref[h, LANES:, :] = jnp.ones((ONES_ROWS, vt_ref.shape[2]), vt_ref.dtype)


ATTN_STREAMS = 4
DIFF_HEADS_PER_STEP = 2
MLA_HEADS_PER_STEP = 2


def _tile_streams(n_tiles):
    assert n_tiles == 2 * ATTN_STREAMS
    return [[n_tiles - 1 - r, r] for r in range(ATTN_STREAMS)]


def _causal_attention(streams, tq, emit):
    key = lax.broadcasted_iota(jnp.int32, (tq, tq), 0)
    qry = lax.broadcasted_iota(jnp.int32, (tq, tq), 1)
    on_or_below_diag = key <= qry
    flat = [[(item, c) for item in stream for c in range(item[3] + 1)] for stream in streams]
    q_t = {}

    def scores(p, k):
        if k >= len(flat[p]):
            return None
        (load_q, load_k, _, i, tag), c = flat[p][k]
        if c == 0:
            q_t[tag] = load_q().astype(jnp.float32).T.astype(jnp.bfloat16)
        s = jnp.dot(load_k(c), q_t[tag], preferred_element_type=jnp.float32)
        if c == i:
            s = jnp.where(on_or_below_diag, s, NEG_INF)
        return s.reshape(tq // 8, 8, tq)

    m_run = [None] * len(streams)
    acc = [None] * len(streams)
    s_next = [scores(p, 0) for p in range(len(streams))]
    for k in range(max(len(f) for f in flat)):
        s_cur, s_next = s_next, [scores(p, k + 1) for p in range(len(streams))]
        for p in range(len(streams)):
            if s_cur[p] is None:
                continue
            (_, _, load_vt, i, tag), c = flat[p][k]
            m = jnp.max(jnp.max(s_cur[p], axis=0), axis=0, keepdims=True)
            if c > 0:
                m = jnp.maximum(m_run[p], m)
            pt = jnp.exp2(s_cur[p] - jnp.broadcast_to(m, (8, tq))[None]).reshape(tq, tq).astype(jnp.bfloat16)
            d = jnp.dot(load_vt(c), pt, preferred_element_type=jnp.float32)
            acc[p] = d if c == 0 else acc[p] * jnp.exp2(m_run[p] - m) + d
            m_run[p] = m
            if c == i:
                emit(tag, acc[p])


def _attn_scratch(seq, heads_per_step):
    return [pltpu.VMEM((heads_per_step, LANES + ONES_ROWS, seq), jnp.bfloat16)]


def _diff_attn_kernel(q_ref, k_ref, v_ref, g_ref, lam_ref, subw_ref, o_ref, vt_ref, *, tq, lam_init):
    _transpose_v(v_ref, vt_ref)
    dv = LANES
    t = lam_ref[...]
    lam = (jnp.exp(jnp.sum(t[0:1] * t[1:2], axis=-1, keepdims=True))
           - jnp.exp(jnp.sum(t[2:3] * t[3:4], axis=-1, keepdims=True)) + lam_init)
    lane = lax.broadcasted_iota(jnp.int32, (tq, LANES), 1)

    def item(h, i, second):
        cols = slice(h * LANES, (h + 1) * LANES)

        def load_q():
            q = q_ref[i * tq:(i + 1) * tq, cols]
            keep = (lane >= 64) if second else (lane < 64)
            return jnp.where(keep, q, jnp.zeros_like(q))
        return (load_q, lambda c: k_ref[c * tq:(c + 1) * tq, cols], lambda c: vt_ref[h, :, c * tq:(c + 1) * tq],
                i, (h, i, second))

    streams = [[item(h, i, second) for i in tiles for second in (False, True)]
               for h in range(DIFF_HEADS_PER_STEP) for tiles in _tile_streams(q_ref.shape[0] // tq)]
    first = {}

    def emit(tag, acc):
        h, i, second = tag
        o_n = acc[0:dv] * (1.0 / acc[dv:dv + 1])
        if not second:
            first[h, i] = o_n
            return
        rows, cols = slice(i * tq, (i + 1) * tq), slice(h * LANES, (h + 1) * LANES)
        o = first.pop((h, i)) - lam * o_n
        ms = jnp.mean(o * o, axis=0, keepdims=True)
        o = (o * lax.rsqrt(ms + DA_SUBLN_EPS)).T
        o = o * subw_ref[...] * (1.0 - lam_init)
        o_ref[rows, cols] = (o * g_ref[rows, cols].astype(jnp.float32)).astype(o_ref.dtype)

    _causal_attention(streams, tq, emit)


def _diff_attn(proj, lam_rows, subln_w, batch, seq, lam_init):
    heads, hps = 4, DIFF_HEADS_PER_STEP
    slab = lambda first: pl.BlockSpec((seq, hps * LANES), lambda b, h: (b, first // hps + h))
    return pl.pallas_call(
        functools.partial(_diff_attn_kernel, tq=ATTN_TQ, lam_init=lam_init),
        grid=(batch, heads // hps),
        in_specs=[
            slab(SLAB_AQ), slab(SLAB_AK), slab(SLAB_AV), slab(SLAB_AG),
            pl.BlockSpec((4, 64), lambda b, h: (0, 0)),
            pl.BlockSpec((1, LANES), lambda b, h: (0, 0)),
        ],
        out_specs=pl.BlockSpec((seq, hps * LANES), lambda b, h: (b, h)),
        out_shape=jax.ShapeDtypeStruct((batch * seq, heads * LANES), jnp.bfloat16),
        scratch_shapes=_attn_scratch(seq, hps),
        compiler_params=_cparams(("parallel", "parallel")),
        name="diff_attn",
    )(proj, proj, proj, proj, lam_rows, subln_w.reshape(1, LANES))


def _mla_attn_kernel(q_ref, k_ref, v_ref, *rest, tq):
    g_refs, (o_ref, vt_ref) = rest[:MLA_HEADS_PER_STEP], rest[MLA_HEADS_PER_STEP:]
    _transpose_v(v_ref, vt_ref)
    dv = LANES

    def item(h, i):
        cols = slice(h * 2 * LANES, (h + 1) * 2 * LANES)
        return (lambda: q_ref[i * tq:(i + 1) * tq, cols], lambda c: k_ref[c * tq:(c + 1) * tq, cols],
                lambda c: vt_ref[h, :, c * tq:(c + 1) * tq], i, (h, i))

    streams = [[item(h, i) for i in tiles]
               for h in range(MLA_HEADS_PER_STEP) for tiles in _tile_streams(q_ref.shape[0] // tq)]

    def emit(tag, acc):
        h, i = tag
        rows = slice(i * tq, (i + 1) * tq)
        o = (acc[0:dv] * (1.0 / acc[dv:dv + 1])).T
        o_ref[rows, h * LANES:(h + 1) * LANES] = (o * g_refs[h][rows, :].astype(jnp.float32)).astype(o_ref.dtype)

    _causal_attention(streams, tq, emit)


def _mla_attn(qf, kf, vc, proj, batch, seq):
    heads, hps = 4, MLA_HEADS_PER_STEP
    gate = lambda r: pl.BlockSpec((seq, LANES), lambda b, h: (b, SLAB_CG + hps * h + r))
    return pl.pallas_call(
        functools.partial(_mla_attn_kernel, tq=ATTN_TQ),
        grid=(batch, heads // hps),
        in_specs=[
            pl.BlockSpec((seq, hps * 2 * LANES), lambda b, h: (b, h)),
            pl.BlockSpec((seq, hps * 2 * LANES), lambda b, h: (b, h)),
            pl.BlockSpec((seq, hps * LANES), lambda b, h: (b, h)),
            *[gate(r) for r in range(hps)],
        ],
        out_specs=pl.BlockSpec((seq, hps * LANES), lambda b, h: (b, h)),
        out_shape=jax.ShapeDtypeStruct((batch * seq, heads * LANES), jnp.bfloat16),
        scratch_shapes=_attn_scratch(seq, hps),
        compiler_params=_cparams(("parallel", "parallel")),
        name="mla_attn",
    )(qf, kf, vc, *([proj] * hps))


def _pad_w_uq(w):
    z = jnp.zeros(w.shape[:-1] + (64,), w.dtype)
    pieces = []
    for h in range(4):
        pieces += [w[..., h * 192:(h + 1) * 192], z]
    return jnp.concatenate(pieces, axis=-1).astype(jnp.bfloat16)


def _reorder_w_ukv(w):
    k = [w[..., h * 256:h * 256 + 128] for h in range(4)]
    v = [w[..., h * 256 + 128:(h + 1) * 256] for h in range(4)]
    return jnp.concatenate(k + v, axis=-1).astype(jnp.bfloat16)


def _mla_up_kernel(cq_ref, ckv_ref, ckr_ref, qnw_ref, kvnw_ref, wuq_ref, wukv_ref,
                   cos_ref, sa_ref, sb_ref, qf_ref, kf_ref, vc_ref):
    def norm(ref, w_ref):
        c = ref[...].astype(jnp.float32)
        ms = jnp.mean(c * c, axis=-1, keepdims=True)
        return (c * lax.rsqrt(ms + NORM_EPS) * w_ref[...]).astype(jnp.bfloat16)

    scale = 192 ** -0.5 * LOG2E
    q = jnp.dot(norm(cq_ref, qnw_ref), wuq_ref[...], preferred_element_type=jnp.float32)
    kv = jnp.dot(norm(ckv_ref, kvnw_ref), wukv_ref[...], preferred_element_type=jnp.float32)
    kr = ckr_ref[...]
    for h in range(4):
        nope = q[:, 2 * h * LANES:(2 * h + 1) * LANES]
        rope = _rope(q[:, (2 * h + 1) * LANES:(2 * h + 2) * LANES], cos_ref[...], sa_ref[...], sb_ref[...])
        qf_ref[:, 2 * h * LANES:(2 * h + 1) * LANES] = (nope * scale).astype(qf_ref.dtype)
        qf_ref[:, (2 * h + 1) * LANES:(2 * h + 2) * LANES] = (rope * scale).astype(qf_ref.dtype)
        kf_ref[:, 2 * h * LANES:(2 * h + 1) * LANES] = kv[:, h * LANES:(h + 1) * LANES].astype(kf_ref.dtype)
        kf_ref[:, (2 * h + 1) * LANES:(2 * h + 2) * LANES] = kr
    vc_ref[...] = kv[:, 4 * LANES:].astype(vc_ref.dtype)


def _mla_up(proj, qn_w, kvn_w, w_uq_p, w_ukv_r, layer, cos, sa, sb):
    m = proj.shape[0]
    tm = 1024
    tab = pl.BlockSpec((tm, LANES), lambda i: (i, 0))
    full = lambda shape: pl.BlockSpec(shape, lambda i: (0, 0))
    weight = pl.BlockSpec((None, 512, 1024), lambda i: (layer, 0, 0))
    return pl.pallas_call(
        _mla_up_kernel,
        grid=(m // tm,),
        in_specs=[
            pl.BlockSpec((tm, 4 * LANES), lambda i: (i, SLAB_CQ // 4)),
            pl.BlockSpec((tm, 4 * LANES), lambda i: (i, SLAB_CKV // 4)),
            pl.BlockSpec((tm, LANES), lambda i: (i, SLAB_CKR)),
            full((1, 512)), full((1, 512)), weight, weight,
            tab, tab, tab,
        ],
        out_specs=[
            pl.BlockSpec((tm, 1024), lambda i: (i, 0)),
            pl.BlockSpec((tm, 1024), lambda i: (i, 0)),
            pl.BlockSpec((tm, 512), lambda i: (i, 0)),
        ],
        out_shape=[
            jax.ShapeDtypeStruct((m, 1024), jnp.bfloat16),
            jax.ShapeDtypeStruct((m, 1024), jnp.bfloat16),
            jax.ShapeDtypeStruct((m, 512), jnp.bfloat16),
        ],
        compiler_params=_cparams(("parallel",)),
        name="mla_up",
    )(proj, proj, proj, qn_w.reshape(1, 512), kvn_w.reshape(1, 512), w_uq_p, w_ukv_r, cos, sa, sb)


def _swa_kernel(sink_ref, q0_ref, q1_ref, kc_ref, kp_ref, vc_ref, vp_ref, gate_ref, o_ref, *, blocks):
    w = SW_WINDOW
    q_refs = (q0_ref, q1_ref)
    first = pl.program_id(1) == 0
    lane = lax.broadcasted_iota(jnp.int32, ((blocks + 1) * w, LANES), 1)
    lo = lane < 64

    kt = jnp.concatenate([kp_ref[...], kc_ref[...]], axis=0).astype(jnp.float32)
    kr = pltpu.roll(kt, 64, 1)
    k_halves = []
    for g in range(2):
        a, b = (kt, kr) if g == 0 else (kr, kt)
        k_halves.append((jnp.where(lo, a, 0.0).astype(jnp.bfloat16), jnp.where(lo, 0.0, b).astype(jnp.bfloat16)))

    vt = jnp.concatenate([vp_ref[...], vc_ref[...]], axis=0).astype(jnp.float32).T
    zeros = jnp.zeros((64, 2 * w), jnp.bfloat16)
    ones = jnp.ones((ONES_ROWS, 2 * w), jnp.bfloat16)
    no_ones = jnp.zeros((ONES_ROWS, 2 * w), jnp.bfloat16)

    key = lax.broadcasted_iota(jnp.int32, (2 * w, 4 * w), 0)
    qry = lax.broadcasted_iota(jnp.int32, (2 * w, 4 * w), 1) & (w - 1)
    rel = qry + w - key
    in_band = (rel >= 0) & (rel < w)
    first_key = jnp.where(first, w, 0)
    bias = jnp.where(in_band, 0.0, NEG_INF)
    bias_first = jnp.where(in_band & (key >= first_key), 0.0, NEG_INF)

    def scores(n, g):
        k_lo, k_hi = k_halves[g]
        rows = slice(n * w, (n + 2) * w)
        kk = jnp.concatenate([k_lo[rows], k_hi[rows]], axis=0)
        qs = jnp.concatenate(
            [q_refs[g][n * w:(n + 1) * w, s * LANES:(s + 1) * LANES] for s in range(4)], axis=0)
        st = lax.dot_general(kk, qs, _NT, preferred_element_type=jnp.float32)
        ps, sink_terms = [], []
        for e in range(2):
            blk = st[e * 2 * w:(e + 1) * 2 * w] + (bias_first if n == 0 else bias)
            sink = jnp.concatenate(
                [jnp.full((1, w), sink_ref[8 * g + 2 * s + e] * LOG2E, jnp.float32) for s in range(4)], axis=1)
            m = jnp.maximum(jnp.max(blk, axis=0, keepdims=True), sink)
            ps.append(jnp.exp2(blk - m).astype(jnp.bfloat16))
            sink_terms.append(jnp.exp2(sink - m))
        return jnp.concatenate(ps, axis=0), sink_terms

    def outputs(n, g, pt, sink_terms):
        vg = vt[64 * g:64 * (g + 1), n * w:(n + 2) * w].astype(jnp.bfloat16)
        lhs = jnp.concatenate([
            jnp.concatenate([vg, zeros], axis=1), jnp.concatenate([zeros, vg], axis=1),
            jnp.concatenate([ones, no_ones], axis=1), jnp.concatenate([no_ones, ones], axis=1)], axis=0)
        acc = jnp.dot(lhs, pt, preferred_element_type=jnp.float32)
        den_even = acc[128:129] + sink_terms[0]
        den_odd = acc[128 + ONES_ROWS:129 + ONES_ROWS] + sink_terms[1]
        ot = jnp.concatenate([acc[0:64] * (1.0 / den_even), acc[64:128] * (1.0 / den_odd)], axis=0)
        for s in range(4):
            slab = 4 * g + s
            gate = gate_ref[n * w:(n + 1) * w, slab * LANES:(slab + 1) * LANES]
            o = ot[:, s * w:(s + 1) * w].T
            o_ref[n * w:(n + 1) * w, slab * LANES:(slab + 1) * LANES] = (
                o * gate.astype(jnp.float32)).astype(o_ref.dtype)

    pending = None
    for item in [(n, g) for n in range(blocks) for g in range(2)] + [None]:
        computed = scores(*item) if item is not None else None
        if pending is not None:
            outputs(*pending[0], *pending[1])
        pending = (item, computed)


def _swa(proj, sinks, batch, seq):
    w = SW_WINDOW
    blocks = 8
    rows = blocks * w
    steps = seq // rows
    nb = seq // w
    cur = lambda slab: pl.BlockSpec((rows, LANES), lambda b, i: (b * steps + i, slab))
    prev = lambda slab: pl.BlockSpec(
        (w, LANES), lambda b, i: (jnp.maximum(b * nb + i * blocks - 1, 0), slab))
    wide = lambda width, slab: pl.BlockSpec((rows, width * LANES), lambda b, i: (b * steps + i, slab // width))
    return pl.pallas_call(
        functools.partial(_swa_kernel, blocks=blocks),
        grid=(batch, steps),
        in_specs=[
            pl.BlockSpec(memory_space=pltpu.SMEM),
            wide(4, SLAB_BQ_LO), wide(4, SLAB_BQ_HI),
            cur(SLAB_BK), prev(SLAB_BK), cur(SLAB_BV), prev(SLAB_BV),
            wide(8, SLAB_BG),
        ],
        out_specs=pl.BlockSpec((rows, 1024), lambda b, i: (b * steps + i, 0)),
        out_shape=jax.ShapeDtypeStruct((batch * seq, 1024), jnp.bfloat16),
        compiler_params=_cparams(("parallel", "arbitrary")),
        name="swa",
    )(sinks.astype(jnp.float32), proj, proj, proj, proj, proj, proj, proj)


OUT_SUB_ROWS = 256


def _out_proj_kernel(ya_ref, yb_ref, yc_ref, wf_ref, x_ref, postw_ref, o_ref, w_ref):
    @pl.when(pl.program_id(0) == 0)
    def _():
        w_ref[...] = wf_ref[...].astype(w_ref.dtype)

    for r in range(0, o_ref.shape[0], OUT_SUB_ROWS):
        rows = slice(r, r + OUT_SUB_ROWS)
        out = jnp.dot(ya_ref[rows, :], w_ref[0:512, :], preferred_element_type=jnp.float32)
        out += jnp.dot(yb_ref[rows, :], w_ref[512:1536, :], preferred_element_type=jnp.float32)
        out += jnp.dot(yc_ref[rows, :], w_ref[1536:2048, :], preferred_element_type=jnp.float32)
        ms = jnp.mean(out * out, axis=-1, keepdims=True)
        o_ref[rows, :] = x_ref[rows, :] + out * lax.rsqrt(ms + NORM_EPS) * postw_ref[...]


def _out_proj(ya, yb, yc, w_out, layer, x2, post_w):
    m = x2.shape[0]
    tm = 512
    row = lambda width: pl.BlockSpec((tm, width), lambda i: (i, 0))
    return pl.pallas_call(
        _out_proj_kernel,
        grid=(m // tm,),
        in_specs=[
            row(512), row(1024), row(512),
            pl.BlockSpec((None, D_MODEL, D_MODEL), lambda i: (layer, 0, 0), pipeline_mode=pl.Buffered(1)),
            row(D_MODEL),
            pl.BlockSpec((1, D_MODEL), lambda i: (0, 0)),
        ],
        out_specs=row(D_MODEL),
        out_shape=jax.ShapeDtypeStruct((m, D_MODEL), jnp.float32),
        scratch_shapes=[pltpu.VMEM((D_MODEL, D_MODEL), jnp.bfloat16)],
        compiler_params=_cparams(("arbitrary",)),
        name="out_proj",
    )(ya, yb, yc, w_out, x2, post_w.reshape(1, D_MODEL))


def kernel(x, positions, pre_norm_w, post_norm_w, w_in, diff_lambda_q1, diff_lambda_k1, diff_lambda_q2,
           diff_lambda_k2, diff_subln_w, sink_logits, mla_q_norm_w, mla_kv_norm_w, w_uq, w_ukv, w_out):
    batch, seq, d = x.shape
    depth = w_in.shape[0]
    cos, sa, sb = _rope_tables(positions)
    x2 = x.reshape(batch * seq, d)
    w_uq_p, w_ukv_r = _pad_w_uq(w_uq), _reorder_w_ukv(w_ukv)
    for layer in range(depth):
        lam_init = 0.8 - 0.6 * math.exp(-0.3 * layer)
        lam_rows = jnp.stack([diff_lambda_q1[layer], diff_lambda_k1[layer],
                              diff_lambda_q2[layer], diff_lambda_k2[layer]]).astype(jnp.float32)
        proj = _in_proj(x2, pre_norm_w[layer], w_in, layer, cos, sa, sb)
        ya = _diff_attn(proj, lam_rows, diff_subln_w[layer], batch, seq, lam_init)
        yb = _swa(proj, sink_logits[layer], batch, seq)
        qf, kf, vc = _mla_up(proj, mla_q_norm_w[layer], mla_kv_norm_w[layer],
                             w_uq_p, w_ukv_r, layer, cos, sa, sb)
        yc = _mla_attn(qf, kf, vc, proj, batch, seq)
        x2 = _out_proj(ya, yb, yc, w_out, layer, x2, post_norm_w[layer])
    return x2.reshape(batch, seq, d)
```

```python
import functools
import math

import jax
import jax.numpy as jnp
import numpy as np
from jax import lax
from jax.experimental import pallas as pl
from jax.experimental.pallas import tpu as pltpu

D_MODEL = 2048
ROPE_THETA = 10000.0
NORM_EPS = 1e-6
NEG_INF = -1e30
LOG2E = math.log2(math.e)
DA_SUBLN_EPS = 1e-5
SW_WINDOW = 128
LANES = 128

IN_TILES = (
    ((0, 1, 2, 3, 8, 9, 10, 11), ("rope_q",) * 4 + ("plain",) * 4),
    ((4, 5, 6, 7, 34, 35, 36, 37), ("rope",) * 4 + ("plain",) * 4),
    ((16, 17, 18, 19, 38, 39, 40, 41), ("rope_q",) * 4 + ("plain",) * 4),
    ((20, 21, 22, 23, 12, 13, 14, 15), ("rope_q",) * 4 + ("silu",) * 4),
    ((26, 27, 28, 29, 30, 31, 32, 33), ("silu",) * 8),
    ((42, 43, 44, 45, 46, 24, 25, 25), ("kr",) + ("gate_c",) * 4 + ("rope", "plain", "skip")),
)
IN_TILE = 8 * LANES
N_SLABS = 8 * len(IN_TILES)
SLAB_AQ, SLAB_AV, SLAB_AK, SLAB_CQ = 0, 4, 8, 12
SLAB_BQ_LO, SLAB_CKV, SLAB_BQ_HI, SLAB_AG = 16, 20, 24, 28
SLAB_BG, SLAB_CKR, SLAB_CG, SLAB_BK, SLAB_BV = 32, 40, 41, 45, 46

VMEM_LIMIT = 56 * 1024 * 1024

_NT = (((1,), (1,)), ((), ()))


def _cparams(sem):
    return pltpu.CompilerParams(dimension_semantics=sem, vmem_limit_bytes=VMEM_LIMIT)


def _rope_table_kernel(pos_ref, freq_ref, cos_ref, sa_ref, sb_ref):
    ang = pos_ref[...].astype(jnp.float32) * freq_ref[...]
    c, s = jnp.cos(ang), jnp.sin(ang)
    quarter = ang.shape[0]
    lane = lax.broadcasted_iota(jnp.int32, ang.shape, 1)
    first_half = (lane % 64) < 32
    for r in range(4):
        rows = slice(r * quarter, (r + 1) * quarter)
        cr = jnp.tile(c[:, 32 * r:32 * (r + 1)], (1, 4))
        sr = jnp.tile(s[:, 32 * r:32 * (r + 1)], (1, 4))
        cos_ref[rows, :] = cr
        sa_ref[rows, :] = jnp.where(first_half, -sr, 0.0)
        sb_ref[rows, :] = jnp.where(first_half, 0.0, sr)


def _rope_tables(positions):
    m = positions.size
    tm = 1024
    pos_p = jnp.repeat(positions.reshape(m // tm, 4, tm // 4).transpose(0, 2, 1), 32, axis=2).reshape(m // 4, LANES)
    inv_freq = jnp.power(ROPE_THETA, -jnp.arange(0, 64, 2, dtype=jnp.float32) / 64)
    freq = jnp.tile(inv_freq, 4).reshape(1, LANES)
    spec = pl.BlockSpec((tm, LANES), lambda i: (i, 0))
    return pl.pallas_call(
        _rope_table_kernel,
        grid=(m // tm,),
        in_specs=[pl.BlockSpec((tm // 4, LANES), lambda i: (i, 0)), pl.BlockSpec((1, LANES), lambda i: (0, 0))],
        out_specs=[spec, spec, spec],
        out_shape=[jax.ShapeDtypeStruct((m, LANES), jnp.float32)] * 3,
        compiler_params=_cparams(("parallel",)),
        name="rope_tables",
    )(pos_p, freq)


def _rope(a, cos, sa, sb):
    return a * cos + pltpu.roll(a, 96, 1) * sa + pltpu.roll(a, 32, 1) * sb


def _silu(a):
    return a * (0.5 + 0.5 * jnp.tanh(0.5 * a))


def _in_proj_kernel(x_ref, prew_ref, *rest):
    w_refs, (cos_ref, sa_ref, sb_ref, o_ref, h_ref) = rest[:8], rest[8:]
    j, r = pl.program_id(1), pl.program_id(2)
    tm = x_ref.shape[0]

    @pl.when(j == 0)
    def _():
        x = x_ref[...]
        ms = jnp.mean(x * x, axis=-1, keepdims=True)
        h_ref[r] = (x * lax.rsqrt(ms + NORM_EPS) * prew_ref[...]).astype(jnp.bfloat16)

    def epilogue(kinds):
        w = jnp.concatenate([w_ref[...] for w_ref in w_refs], axis=0).astype(jnp.bfloat16)
        acc = lax.dot_general(h_ref[r], w, _NT, preferred_element_type=jnp.float32)
        slab = lambda s: acc[:, s * LANES:(s + 1) * LANES]
        lane = lax.broadcasted_iota(jnp.int32, (acc.shape[0], LANES), 1)
        rows = pl.ds(pl.multiple_of(r * tm, tm), tm)
        for s, kind in enumerate(kinds):
            a = slab(s)
            if kind == "skip":
                a = jnp.zeros_like(a)
            if kind in ("rope", "rope_q", "kr"):
                a = _rope(a, cos_ref[rows, :], sa_ref[rows, :], sb_ref[rows, :])
                if kind == "rope_q":
                    a = a * (0.125 * LOG2E)
            elif kind == "silu":
                a = _silu(a)
            elif kind == "gate_c":
                a = jnp.where(lane < 64, pltpu.roll(_silu(slab(s - 1)), 64, 1), pltpu.roll(_silu(a), 64, 1))
            o_ref[:, s * LANES:(s + 1) * LANES] = a.astype(o_ref.dtype)

    for t, (_, kinds) in enumerate(IN_TILES):
        pl.when(j == t)(functools.partial(epilogue, kinds))


def _in_proj(x2, pre_w, w_in, layer, cos, sa, sb):
    m = x2.shape[0]
    tm = 1024
    tab = pl.BlockSpec((2 * tm, LANES), lambda p, j, r: (p, 0))

    def slab_spec(position):
        sources = [slabs[position] for slabs, _ in IN_TILES]

        def index_map(p, j, r):
            source = sources[-1]
            for t in range(len(sources) - 2, -1, -1):
                source = jnp.where(j == t, sources[t], source)
            return layer, source, 0
        return pl.BlockSpec((None, LANES, D_MODEL), index_map)

    wt = jnp.swapaxes(w_in, 1, 2)
    return pl.pallas_call(
        _in_proj_kernel,
        grid=(m // (2 * tm), len(IN_TILES), 2),
        in_specs=[
            pl.BlockSpec((tm, D_MODEL), lambda p, j, r: (2 * p + jnp.where(j == 0, r, 1), 0)),
            pl.BlockSpec((1, D_MODEL), lambda p, j, r: (0, 0)),
            *[slab_spec(position) for position in range(8)],
            tab, tab, tab,
        ],
        out_specs=pl.BlockSpec((tm, IN_TILE), lambda p, j, r: (2 * p + r, j)),
        out_shape=jax.ShapeDtypeStruct((m, N_SLABS * LANES), jnp.bfloat16),
        scratch_shapes=[pltpu.VMEM((2, tm, D_MODEL), jnp.bfloat16)],
        compiler_params=_cparams(("parallel", "arbitrary", "arbitrary")),
        name="in_proj",
    )(x2, pre_w.reshape(1, D_MODEL), *([wt] * 8), cos, sa, sb)


ATTN_TQ = 256


ONES_ROWS = 16


def _transpose_v(v_ref, vt_ref):
    for h in range(vt_ref.shape[0]):
        vt_ref[h, 0:LANES, :] = v_ref[:, h * LANES:(h + 1) * LANES].astype(jnp.float32).T.astype(vt_ref.dtype)
        vt_ref[h, LANES:, :] = jnp.ones((ONES_ROWS, vt_ref.shape[2]), vt_ref.dtype)


ATTN_STREAMS = 4
DIFF_HEADS_PER_STEP = 2
MLA_HEADS_PER_STEP = 2


def _tile_streams(n_tiles):
    assert n_tiles == 2 * ATTN_STREAMS
    return [[n_tiles - 1 - r, r] for r in range(ATTN_STREAMS)]


def _causal_attention(streams, tq, emit):
    key = lax.broadcasted_iota(jnp.int32, (tq, tq), 0)
    qry = lax.broadcasted_iota(jnp.int32, (tq, tq), 1)
    on_or_below_diag = key <= qry
    flat = [[(item, c) for item in stream for c in range(item[3] + 1)] for stream in streams]
    q_t = {}

    def scores(p, k):
        if k >= len(flat[p]):
            return None
        (load_q, load_k, _, i, tag), c = flat[p][k]
        if c == 0:
            q_t[tag] = load_q().astype(jnp.float32).T.astype(jnp.bfloat16)
        s = jnp.dot(load_k(c), q_t[tag], preferred_element_type=jnp.float32)
        if c == i:
            s = jnp.where(on_or_below_diag, s, NEG_INF)
        return s.reshape(tq // 8, 8, tq)

    m_run = [None] * len(streams)
    acc = [None] * len(streams)
    s_next = [scores(p, 0) for p in range(len(streams))]
    for k in range(max(len(f) for f in flat)):
        s_cur, s_next = s_next, [scores(p, k + 1) for p in range(len(streams))]
        for p in range(len(streams)):
            if s_cur[p] is None:
                continue
            (_, _, load_vt, i, tag), c = flat[p][k]
            m = jnp.max(jnp.max(s_cur[p], axis=0), axis=0, keepdims=True)
            if c > 0:
                m = jnp.maximum(m_run[p], m)
            pt = jnp.exp2(s_cur[p] - jnp.broadcast_to(m, (8, tq))[None]).reshape(tq, tq).astype(jnp.bfloat16)
            d = jnp.dot(load_vt(c), pt, preferred_element_type=jnp.float32)
            acc[p] = d if c == 0 else acc[p] * jnp.exp2(m_run[p] - m) + d
            m_run[p] = m
            if c == i:
                emit(tag, acc[p])


def _attn_scratch(seq, heads_per_step):
    return [pltpu.VMEM((heads_per_step, LANES + ONES_ROWS, seq), jnp.bfloat16)]


def _diff_attn_kernel(q_ref, k_ref, v_ref, g_ref, lam_ref, subw_ref, o_ref, vt_ref, *, tq, lam_init):
    _transpose_v(v_ref, vt_ref)
    dv = LANES
    t = lam_ref[...]
    lam = (jnp.exp(jnp.sum(t[0:1] * t[1:2], axis=-1, keepdims=True))
           - jnp.exp(jnp.sum(t[2:3] * t[3:4], axis=-1, keepdims=True)) + lam_init)
    lane = lax.broadcasted_iota(jnp.int32, (tq, LANES), 1)

    def item(h, i, second):
        cols = slice(h * LANES, (h + 1) * LANES)

        def load_q():
            q = q_ref[i * tq:(i + 1) * tq, cols]
            keep = (lane >= 64) if second else (lane < 64)
            return jnp.where(keep, q, jnp.zeros_like(q))
        return (load_q, lambda c: k_ref[c * tq:(c + 1) * tq, cols], lambda c: vt_ref[h, :, c * tq:(c + 1) * tq],
                i, (h, i, second))

    streams = [[item(h, i, second) for i in tiles for second in (False, True)]
               for h in range(DIFF_HEADS_PER_STEP) for tiles in _tile_streams(q_ref.shape[0] // tq)]
    first = {}

    def emit(tag, acc):
        h, i, second = tag
        o_n = acc[0:dv] * (1.0 / acc[dv:dv + 1])
        if not second:
            first[h, i] = o_n
            return
        rows, cols = slice(i * tq, (i + 1) * tq), slice(h * LANES, (h + 1) * LANES)
        o = first.pop((h, i)) - lam * o_n
        ms = jnp.mean(o * o, axis=0, keepdims=True)
        o = (o * lax.rsqrt(ms + DA_SUBLN_EPS)).T
        o = o * subw_ref[...] * (1.0 - lam_init)
        o_ref[rows, cols] = (o * g_ref[rows, cols].astype(jnp.float32)).astype(o_ref.dtype)

    _causal_attention(streams, tq, emit)


def _diff_attn(proj, lam_rows, subln_w, batch, seq, lam_init):
    heads, hps = 4, DIFF_HEADS_PER_STEP
    slab = lambda first: pl.BlockSpec((seq, hps * LANES), lambda b, h: (b, first // hps + h))
    return pl.pallas_call(
        functools.partial(_diff_attn_kernel, tq=ATTN_TQ, lam_init=lam_init),
        grid=(batch, heads // hps),
        in_specs=[
            slab(SLAB_AQ), slab(SLAB_AK), slab(SLAB_AV), slab(SLAB_AG),
            pl.BlockSpec((4, 64), lambda b, h: (0, 0)),
            pl.BlockSpec((1, LANES), lambda b, h: (0, 0)),
        ],
        out_specs=pl.BlockSpec((seq, hps * LANES), lambda b, h: (b, h)),
        out_shape=jax.ShapeDtypeStruct((batch * seq, heads * LANES), jnp.bfloat16),
        scratch_shapes=_attn_scratch(seq, hps),
        compiler_params=_cparams(("parallel", "parallel")),
        name="diff_attn",
    )(proj, proj, proj, proj, lam_rows, subln_w.reshape(1, LANES))


def _mla_attn_kernel(q_ref, k_ref, v_ref, *rest, tq):
    g_refs, (o_ref, vt_ref) = rest[:MLA_HEADS_PER_STEP], rest[MLA_HEADS_PER_STEP:]
    _transpose_v(v_ref, vt_ref)
    dv = LANES

    def item(h, i):
        cols = slice(h * 2 * LANES, (h + 1) * 2 * LANES)
        return (lambda: q_ref[i * tq:(i + 1) * tq, cols], lambda c: k_ref[c * tq:(c + 1) * tq, cols],
                lambda c: vt_ref[h, :, c * tq:(c + 1) * tq], i, (h, i))

    streams = [[item(h, i) for i in tiles]
               for h in range(MLA_HEADS_PER_STEP) for tiles in _tile_streams(q_ref.shape[0] // tq)]

    def emit(tag, acc):
        h, i = tag
        rows = slice(i * tq, (i + 1) * tq)
        o = (acc[0:dv] * (1.0 / acc[dv:dv + 1])).T
        o_ref[rows, h * LANES:(h + 1) * LANES] = (o * g_refs[h][rows, :].astype(jnp.float32)).astype(o_ref.dtype)

    _causal_attention(streams, tq, emit)


def _mla_attn(qf, kf, vc, proj, batch, seq):
    heads, hps = 4, MLA_HEADS_PER_STEP
    gate = lambda r: pl.BlockSpec((seq, LANES), lambda b, h: (b, SLAB_CG + hps * h + r))
    return pl.pallas_call(
        functools.partial(_mla_attn_kernel, tq=ATTN_TQ),
        grid=(batch, heads // hps),
        in_specs=[
            pl.BlockSpec((seq, hps * 2 * LANES), lambda b, h: (b, h)),
            pl.BlockSpec((seq, hps * 2 * LANES), lambda b, h: (b, h)),
            pl.BlockSpec((seq, hps * LANES), lambda b, h: (b, h)),
            *[gate(r) for r in range(hps)],
        ],
        out_specs=pl.BlockSpec((seq, hps * LANES), lambda b, h: (b, h)),
        out_shape=jax.ShapeDtypeStruct((batch * seq, heads * LANES), jnp.bfloat16),
        scratch_shapes=_attn_scratch(seq, hps),
        compiler_params=_cparams(("parallel", "parallel")),
        name="mla_attn",
    )(qf, kf, vc, *([proj] * hps))


def _mla_up_kernel(cq_ref, ckv_ref, ckr_ref, qnw_ref, kvnw_ref, wuq_ref, wukv_ref,
                   cos_ref, sa_ref, sb_ref, qf_ref, kf_ref, vc_ref):
    def norm(ref, w_ref):
        c = ref[...].astype(jnp.float32)
        ms = jnp.mean(c * c, axis=-1, keepdims=True)
        return (c * lax.rsqrt(ms + NORM_EPS) * w_ref[...]).astype(jnp.bfloat16)

    scale = 192 ** -0.5 * LOG2E
    wq = wuq_ref[...]
    zero = jnp.zeros((wq.shape[0], 64), wq.dtype)
    wq = jnp.concatenate([piece for h in range(4) for piece in (wq[:, 192 * h:192 * (h + 1)], zero)], axis=1)
    q = jnp.dot(norm(cq_ref, qnw_ref), wq.astype(jnp.bfloat16), preferred_element_type=jnp.float32)
    kv = jnp.dot(norm(ckv_ref, kvnw_ref), wukv_ref[...].astype(jnp.bfloat16), preferred_element_type=jnp.float32)
    kr = ckr_ref[...]
    for h in range(4):
        nope = q[:, 2 * h * LANES:(2 * h + 1) * LANES]
        rope = _rope(q[:, (2 * h + 1) * LANES:(2 * h + 2) * LANES], cos_ref[...], sa_ref[...], sb_ref[...])
        qf_ref[:, 2 * h * LANES:(2 * h + 1) * LANES] = (nope * scale).astype(qf_ref.dtype)
        qf_ref[:, (2 * h + 1) * LANES:(2 * h + 2) * LANES] = (rope * scale).astype(qf_ref.dtype)
        kf_ref[:, 2 * h * LANES:(2 * h + 1) * LANES] = kv[:, 2 * h * LANES:(2 * h + 1) * LANES].astype(kf_ref.dtype)
        kf_ref[:, (2 * h + 1) * LANES:(2 * h + 2) * LANES] = kr
        vc_ref[:, h * LANES:(h + 1) * LANES] = kv[:, (2 * h + 1) * LANES:(2 * h + 2) * LANES].astype(vc_ref.dtype)


def _mla_up(proj, qn_w, kvn_w, w_uq, w_ukv, layer, cos, sa, sb):
    m = proj.shape[0]
    tm = 1024
    tab = pl.BlockSpec((tm, LANES), lambda i: (i, 0))
    full = lambda shape: pl.BlockSpec(shape, lambda i: (0, 0))
    weight = lambda w: pl.BlockSpec((None,) + w.shape[1:], lambda i: (layer, 0, 0))
    return pl.pallas_call(
        _mla_up_kernel,
        grid=(m // tm,),
        in_specs=[
            pl.BlockSpec((tm, 4 * LANES), lambda i: (i, SLAB_CQ // 4)),
            pl.BlockSpec((tm, 4 * LANES), lambda i: (i, SLAB_CKV // 4)),
            pl.BlockSpec((tm, LANES), lambda i: (i, SLAB_CKR)),
            full((1, 512)), full((1, 512)), weight(w_uq), weight(w_ukv),
            tab, tab, tab,
        ],
        out_specs=[
            pl.BlockSpec((tm, 1024), lambda i: (i, 0)),
            pl.BlockSpec((tm, 1024), lambda i: (i, 0)),
            pl.BlockSpec((tm, 512), lambda i: (i, 0)),
        ],
        out_shape=[
            jax.ShapeDtypeStruct((m, 1024), jnp.bfloat16),
            jax.ShapeDtypeStruct((m, 1024), jnp.bfloat16),
            jax.ShapeDtypeStruct((m, 512), jnp.bfloat16),
        ],
        compiler_params=_cparams(("parallel",)),
        name="mla_up",
    )(proj, proj, proj, qn_w.reshape(1, 512), kvn_w.reshape(1, 512), w_uq, w_ukv, cos, sa, sb)


def _swa_kernel(sink_ref, q0_ref, q1_ref, kc_ref, kp_ref, vc_ref, vp_ref, gate_ref, o_ref, *, blocks):
    w = SW_WINDOW
    q_refs = (q0_ref, q1_ref)
    first = pl.program_id(1) == 0
    lane = lax.broadcasted_iota(jnp.int32, ((blocks + 1) * w, LANES), 1)
    lo = lane < 64

    kt = jnp.concatenate([kp_ref[...], kc_ref[...]], axis=0).astype(jnp.float32)
    kr = pltpu.roll(kt, 64, 1)
    k_halves = []
    for g in range(2):
        a, b = (kt, kr) if g == 0 else (kr, kt)
        k_halves.append((jnp.where(lo, a, 0.0).astype(jnp.bfloat16), jnp.where(lo, 0.0, b).astype(jnp.bfloat16)))

    vt = jnp.concatenate([vp_ref[...], vc_ref[...]], axis=0).astype(jnp.float32).T
    zeros = jnp.zeros((64, 2 * w), jnp.bfloat16)
    ones = jnp.ones((ONES_ROWS, 2 * w), jnp.bfloat16)
    no_ones = jnp.zeros((ONES_ROWS, 2 * w), jnp.bfloat16)

    key = lax.broadcasted_iota(jnp.int32, (2 * w, 4 * w), 0)
    qry = lax.broadcasted_iota(jnp.int32, (2 * w, 4 * w), 1) & (w - 1)
    rel = qry + w - key
    in_band = (rel >= 0) & (rel < w)
    first_key = jnp.where(first, w, 0)
    bias = jnp.where(in_band, 0.0, NEG_INF)
    bias_first = jnp.where(in_band & (key >= first_key), 0.0, NEG_INF)

    def scores(n, g):
        k_lo, k_hi = k_halves[g]
        rows = slice(n * w, (n + 2) * w)
        kk = jnp.concatenate([k_lo[rows], k_hi[rows]], axis=0)
        qs = jnp.concatenate(
            [q_refs[g][n * w:(n + 1) * w, s * LANES:(s + 1) * LANES] for s in range(4)], axis=0)
        st = lax.dot_general(kk, qs, _NT, preferred_element_type=jnp.float32)
        ps, sink_terms = [], []
        for e in range(2):
            blk = st[e * 2 * w:(e + 1) * 2 * w] + (bias_first if n == 0 else bias)
            sink = jnp.concatenate(
                [jnp.full((1, w), sink_ref[8 * g + 2 * s + e] * LOG2E, jnp.float32) for s in range(4)], axis=1)
            m = jnp.maximum(jnp.max(blk, axis=0, keepdims=True), sink)
            ps.append(jnp.exp2(blk - m).astype(jnp.bfloat16))
            sink_terms.append(jnp.exp2(sink - m))
        return jnp.concatenate(ps, axis=0), sink_terms

    def outputs(n, g, pt, sink_terms):
        vg = vt[64 * g:64 * (g + 1), n * w:(n + 2) * w].astype(jnp.bfloat16)
        lhs = jnp.concatenate([
            jnp.concatenate([vg, zeros], axis=1), jnp.concatenate([zeros, vg], axis=1),
            jnp.concatenate([ones, no_ones], axis=1), jnp.concatenate([no_ones, ones], axis=1)], axis=0)
        acc = jnp.dot(lhs, pt, preferred_element_type=jnp.float32)
        den_even = acc[128:129] + sink_terms[0]
        den_odd = acc[128 + ONES_ROWS:129 + ONES_ROWS] + sink_terms[1]
        ot = jnp.concatenate([acc[0:64] * (1.0 / den_even), acc[64:128] * (1.0 / den_odd)], axis=0)
        for s in range(4):
            slab = 4 * g + s
            gate = gate_ref[n * w:(n + 1) * w, slab * LANES:(slab + 1) * LANES]
            o = ot[:, s * w:(s + 1) * w].T
            o_ref[n * w:(n + 1) * w, slab * LANES:(slab + 1) * LANES] = (
                o * gate.astype(jnp.float32)).astype(o_ref.dtype)

    pending = None
    for item in [(n, g) for n in range(blocks) for g in range(2)] + [None]:
        computed = scores(*item) if item is not None else None
        if pending is not None:
            outputs(*pending[0], *pending[1])
        pending = (item, computed)


def _swa(proj, sinks, batch, seq):
    w = SW_WINDOW
    blocks = 8
    rows = blocks * w
    steps = seq // rows
    nb = seq // w
    cur = lambda slab: pl.BlockSpec((rows, LANES), lambda b, i: (b * steps + i, slab))
    prev = lambda slab: pl.BlockSpec(
        (w, LANES), lambda b, i: (jnp.maximum(b * nb + i * blocks - 1, 0), slab))
    wide = lambda width, slab: pl.BlockSpec((rows, width * LANES), lambda b, i: (b * steps + i, slab // width))
    return pl.pallas_call(
        functools.partial(_swa_kernel, blocks=blocks),
        grid=(batch, steps),
        in_specs=[
            pl.BlockSpec(memory_space=pltpu.SMEM),
            wide(4, SLAB_BQ_LO), wide(4, SLAB_BQ_HI),
            cur(SLAB_BK), prev(SLAB_BK), cur(SLAB_BV), prev(SLAB_BV),
            wide(8, SLAB_BG),
        ],
        out_specs=pl.BlockSpec((rows, 1024), lambda b, i: (b * steps + i, 0)),
        out_shape=jax.ShapeDtypeStruct((batch * seq, 1024), jnp.bfloat16),
        compiler_params=_cparams(("parallel", "arbitrary")),
        name="swa",
    )(sinks.astype(jnp.float32), proj, proj, proj, proj, proj, proj, proj)


OUT_SUB_ROWS = 256


def _out_proj_kernel(ya_ref, yb_ref, yc_ref, wf_ref, x_ref, postw_ref, o_ref, w_ref):
    @pl.when(pl.program_id(0) == 0)
    def _():
        w_ref[...] = wf_ref[...].astype(w_ref.dtype)

    for r in range(0, o_ref.shape[0], OUT_SUB_ROWS):
        rows = slice(r, r + OUT_SUB_ROWS)
        out = jnp.dot(ya_ref[rows, :], w_ref[0:512, :], preferred_element_type=jnp.float32)
        out += jnp.dot(yb_ref[rows, :], w_ref[512:1536, :], preferred_element_type=jnp.float32)
        out += jnp.dot(yc_ref[rows, :], w_ref[1536:2048, :], preferred_element_type=jnp.float32)
        ms = jnp.mean(out * out, axis=-1, keepdims=True)
        o_ref[rows, :] = x_ref[rows, :] + out * lax.rsqrt(ms + NORM_EPS) * postw_ref[...]


def _out_proj(ya, yb, yc, w_out, layer, x2, post_w):
    m = x2.shape[0]
    tm = 512
    row = lambda width: pl.BlockSpec((tm, width), lambda i: (i, 0))
    return pl.pallas_call(
        _out_proj_kernel,
        grid=(m // tm,),
        in_specs=[
            row(512), row(1024), row(512),
            pl.BlockSpec((None, D_MODEL, D_MODEL), lambda i: (layer, 0, 0), pipeline_mode=pl.Buffered(1)),
            row(D_MODEL),
            pl.BlockSpec((1, D_MODEL), lambda i: (0, 0)),
        ],
        out_specs=row(D_MODEL),
        out_shape=jax.ShapeDtypeStruct((m, D_MODEL), jnp.float32),
        scratch_shapes=[pltpu.VMEM((D_MODEL, D_MODEL), jnp.bfloat16)],
        compiler_params=_cparams(("arbitrary",)),
        name="out_proj",
    )(ya, yb, yc, w_out, x2, post_w.reshape(1, D_MODEL))


def kernel(x, positions, pre_norm_w, post_norm_w, w_in, diff_lambda_q1, diff_lambda_k1, diff_lambda_q2,
           diff_lambda_k2, diff_subln_w, sink_logits, mla_q_norm_w, mla_kv_norm_w, w_uq, w_ukv, w_out):
    batch, seq, d = x.shape
    depth = w_in.shape[0]
    cos, sa, sb = _rope_tables(positions)
    x2 = x.reshape(batch * seq, d)
    for layer in range(depth):
        lam_init = 0.8 - 0.6 * math.exp(-0.3 * layer)
        lam_rows = jnp.stack([diff_lambda_q1[layer], diff_lambda_k1[layer],
                              diff_lambda_q2[layer], diff_lambda_k2[layer]]).astype(jnp.float32)
        proj = _in_proj(x2, pre_norm_w[layer], w_in, layer, cos, sa, sb)
        ya = _diff_attn(proj, lam_rows, diff_subln_w[layer], batch, seq, lam_init)
        yb = _swa(proj, sink_logits[layer], batch, seq)
        qf, kf, vc = _mla_up(proj, mla_q_norm_w[layer], mla_kv_norm_w[layer],
                             w_uq, w_ukv, layer, cos, sa, sb)
        yc = _mla_attn(qf, kf, vc, proj, batch, seq)
        x2 = _out_proj(ya, yb, yc, w_out, layer, x2, post_norm_w[layer])
    return x2.reshape(batch, seq, d)
```

```python
import functools
import math

import jax
import jax.numpy as jnp
import numpy as np
from jax import lax
from jax.experimental import pallas as pl
from jax.experimental.pallas import tpu as pltpu

D_MODEL = 2048
ROPE_THETA = 10000.0
NORM_EPS = 1e-6
NEG_INF = -1e30
LOG2E = math.log2(math.e)
DA_SUBLN_EPS = 1e-5
SW_WINDOW = 128
LANES = 128

IN_TILES = (
    ((0, 1, 2, 3, 8, 9, 10, 11), ("rope_q",) * 4 + ("plain",) * 4),
    ((4, 5, 6, 7, 34, 35, 36, 37), ("rope",) * 4 + ("plain",) * 4),
    ((16, 17, 18, 19, 38, 39, 40, 41), ("rope_q",) * 4 + ("plain",) * 4),
    ((20, 21, 22, 23, 12, 13, 14, 15), ("rope_q",) * 4 + ("silu",) * 4),
    ((26, 27, 28, 29, 30, 31, 32, 33), ("silu",) * 8),
    ((42, 43, 44, 45, 46, 24, 25, 25), ("kr",) + ("gate_c",) * 4 + ("rope", "plain", "skip")),
)
IN_TILE = 8 * LANES
N_SLABS = 8 * len(IN_TILES)
SLAB_AQ, SLAB_AV, SLAB_AK, SLAB_CQ = 0, 4, 8, 12
SLAB_BQ_LO, SLAB_CKV, SLAB_BQ_HI, SLAB_AG = 16, 20, 24, 28
SLAB_BG, SLAB_CKR, SLAB_CG, SLAB_BK, SLAB_BV = 32, 40, 41, 45, 46

VMEM_LIMIT = 56 * 1024 * 1024

_NT = (((1,), (1,)), ((), ()))


def _cparams(sem):
    return pltpu.CompilerParams(dimension_semantics=sem, vmem_limit_bytes=VMEM_LIMIT)


def _rope_table_kernel(pos_ref, freq_ref, cos_ref, sa_ref, sb_ref):
    ang = pos_ref[...].astype(jnp.float32) * freq_ref[...]
    c, s = jnp.cos(ang), jnp.sin(ang)
    quarter = ang.shape[0]
    lane = lax.broadcasted_iota(jnp.int32, ang.shape, 1)
    first_half = (lane % 64) < 32
    for r in range(4):
        rows = slice(r * quarter, (r + 1) * quarter)
        cr = jnp.tile(c[:, 32 * r:32 * (r + 1)], (1, 4))
        sr = jnp.tile(s[:, 32 * r:32 * (r + 1)], (1, 4))
        cos_ref[rows, :] = cr
        sa_ref[rows, :] = jnp.where(first_half, -sr, 0.0)
        sb_ref[rows, :] = jnp.where(first_half, 0.0, sr)


def _rope_tables(positions):
    m = positions.size
    tm = 1024
    pos_p = jnp.repeat(positions.reshape(m // tm, 4, tm // 4).transpose(0, 2, 1), 32, axis=2).reshape(m // 4, LANES)
    inv_freq = jnp.power(ROPE_THETA, -jnp.arange(0, 64, 2, dtype=jnp.float32) / 64)
    freq = jnp.tile(inv_freq, 4).reshape(1, LANES)
    spec = pl.BlockSpec((tm, LANES), lambda i: (i, 0))
    return pl.pallas_call(
        _rope_table_kernel,
        grid=(m // tm,),
        in_specs=[pl.BlockSpec((tm // 4, LANES), lambda i: (i, 0)), pl.BlockSpec((1, LANES), lambda i: (0, 0))],
        out_specs=[spec, spec, spec],
        out_shape=[jax.ShapeDtypeStruct((m, LANES), jnp.float32)] * 3,
        compiler_params=_cparams(("parallel",)),
        name="rope_tables",
    )(pos_p, freq)


def _rope(a, cos, sa, sb):
    return a * cos + pltpu.roll(a, 96, 1) * sa + pltpu.roll(a, 32, 1) * sb


def _silu(a):
    return a * (0.5 + 0.5 * jnp.tanh(0.5 * a))


def _in_proj_kernel(x_ref, prew_ref, *rest):
    w_refs, (cos_ref, sa_ref, sb_ref, o_ref, h_ref) = rest[:8], rest[8:]
    j, r = pl.program_id(1), pl.program_id(2)
    tm = x_ref.shape[0]

    @pl.when(j == 0)
    def _():
        x = x_ref[...]
        ms = jnp.mean(x * x, axis=-1, keepdims=True)
        h_ref[r] = (x * lax.rsqrt(ms + NORM_EPS) * prew_ref[...]).astype(jnp.bfloat16)

    def epilogue(kinds):
        w = jnp.concatenate([w_ref[...] for w_ref in w_refs], axis=0).astype(jnp.bfloat16)
        acc = lax.dot_general(h_ref[r], w, _NT, preferred_element_type=jnp.float32)
        slab = lambda s: acc[:, s * LANES:(s + 1) * LANES]
        lane = lax.broadcasted_iota(jnp.int32, (acc.shape[0], LANES), 1)
        rows = pl.ds(pl.multiple_of(r * tm, tm), tm)
        for s, kind in enumerate(kinds):
            a = slab(s)
            if kind == "skip":
                a = jnp.zeros_like(a)
            if kind in ("rope", "rope_q", "kr"):
                a = _rope(a, cos_ref[rows, :], sa_ref[rows, :], sb_ref[rows, :])
                if kind == "rope_q":
                    a = a * (0.125 * LOG2E)
            elif kind == "silu":
                a = _silu(a)
            elif kind == "gate_c":
                a = jnp.where(lane < 64, pltpu.roll(_silu(slab(s - 1)), 64, 1), pltpu.roll(_silu(a), 64, 1))
            o_ref[:, s * LANES:(s + 1) * LANES] = a.astype(o_ref.dtype)

    for t, (_, kinds) in enumerate(IN_TILES):
        pl.when(j == t)(functools.partial(epilogue, kinds))


def _in_proj(x2, pre_w, w_in, layer, cos, sa, sb):
    m = x2.shape[0]
    tm = 1024
    tab = pl.BlockSpec((2 * tm, LANES), lambda p, j, r: (p, 0))

    def slab_spec(position):
        sources = [slabs[position] for slabs, _ in IN_TILES]

        def index_map(p, j, r):
            source = sources[-1]
            for t in range(len(sources) - 2, -1, -1):
                source = jnp.where(j == t, sources[t], source)
            return layer, source, 0
        return pl.BlockSpec((None, LANES, D_MODEL), index_map)

    wt = jnp.swapaxes(w_in, 1, 2)
    return pl.pallas_call(
        _in_proj_kernel,
        grid=(m // (2 * tm), len(IN_TILES), 2),
        in_specs=[
            pl.BlockSpec((tm, D_MODEL), lambda p, j, r: (2 * p + jnp.where(j == 0, r, 1), 0)),
            pl.BlockSpec((1, D_MODEL), lambda p, j, r: (0, 0)),
            *[slab_spec(position) for position in range(8)],
            tab, tab, tab,
        ],
        out_specs=pl.BlockSpec((tm, IN_TILE), lambda p, j, r: (2 * p + r, j)),
        out_shape=jax.ShapeDtypeStruct((m, N_SLABS * LANES), jnp.bfloat16),
        scratch_shapes=[pltpu.VMEM((2, tm, D_MODEL), jnp.bfloat16)],
        compiler_params=_cparams(("parallel", "arbitrary", "arbitrary")),
        name="in_proj",
    )(x2, pre_w.reshape(1, D_MODEL), *([wt] * 8), cos, sa, sb)


ATTN_TQ = 256


ONES_ROWS = 16


def _transpose_v(v_ref, vt_ref):
    for h in range(vt_ref.shape[0]):
        vt_ref[h, 0:LANES, :] = v_ref[:, h * LANES:(h + 1) * LANES].astype(jnp.float32).T.astype(vt_ref.dtype)
        vt_ref[h, LANES:, :] = jnp.ones((ONES_ROWS, vt_ref.shape[2]), vt_ref.dtype)


ATTN_STREAMS = 4
DIFF_HEADS_PER_STEP = 2
MLA_HEADS_PER_STEP = 2


def _tile_streams(n_tiles):
    assert n_tiles == 2 * ATTN_STREAMS
    return [[n_tiles - 1 - r, r] for r in range(ATTN_STREAMS)]


def _causal_attention(streams, tq, emit):
    key = lax.broadcasted_iota(jnp.int32, (tq, tq), 0)
    qry = lax.broadcasted_iota(jnp.int32, (tq, tq), 1)
    on_or_below_diag = key <= qry
    flat = [[(item, c) for item in stream for c in range(item[3] + 1)] for stream in streams]
    q_t = {}

    def scores(p, k):
        if k >= len(flat[p]):
            return None
        (load_q, load_k, _, i, tag), c = flat[p][k]
        if c == 0:
            q_t[tag] = load_q().astype(jnp.float32).T.astype(jnp.bfloat16)
        s = jnp.dot(load_k(c), q_t[tag], preferred_element_type=jnp.float32)
        if c == i:
            s = jnp.where(on_or_below_diag, s, NEG_INF)
        return s.reshape(tq // 8, 8, tq)

    m_run = [None] * len(streams)
    acc = [None] * len(streams)
    s_next = [scores(p, 0) for p in range(len(streams))]
    for k in range(max(len(f) for f in flat)):
        s_cur, s_next = s_next, [scores(p, k + 1) for p in range(len(streams))]
        for p in range(len(streams)):
            if s_cur[p] is None:
                continue
            (_, _, load_vt, i, tag), c = flat[p][k]
            m = jnp.max(jnp.max(s_cur[p], axis=0), axis=0, keepdims=True)
            if c > 0:
                m = jnp.maximum(m_run[p], m)
            pt = jnp.exp2(s_cur[p] - jnp.broadcast_to(m, (8, tq))[None]).reshape(tq, tq).astype(jnp.bfloat16)
            d = jnp.dot(load_vt(c), pt, preferred_element_type=jnp.float32)
            acc[p] = d if c == 0 else acc[p] * jnp.exp2(m_run[p] - m) + d
            m_run[p] = m
            if c == i:
                emit(tag, acc[p])


def _attn_scratch(seq, heads_per_step):
    return [pltpu.VMEM((heads_per_step, LANES + ONES_ROWS, seq), jnp.bfloat16)]


def _diff_attn_kernel(q_ref, k_ref, v_ref, g_ref, lam_ref, subw_ref, o_ref, vt_ref, *, tq, lam_init):
    _transpose_v(v_ref, vt_ref)
    dv = LANES
    t = lam_ref[...]
    lam = (jnp.exp(jnp.sum(t[0:1] * t[1:2], axis=-1, keepdims=True))
           - jnp.exp(jnp.sum(t[2:3] * t[3:4], axis=-1, keepdims=True)) + lam_init)
    lane = lax.broadcasted_iota(jnp.int32, (tq, LANES), 1)

    def item(h, i, second):
        cols = slice(h * LANES, (h + 1) * LANES)

        def load_q():
            q = q_ref[i * tq:(i + 1) * tq, cols]
            keep = (lane >= 64) if second else (lane < 64)
            return jnp.where(keep, q, jnp.zeros_like(q))
        return (load_q, lambda c: k_ref[c * tq:(c + 1) * tq, cols], lambda c: vt_ref[h, :, c * tq:(c + 1) * tq],
                i, (h, i, second))

    streams = [[item(h, i, second) for i in tiles for second in (False, True)]
               for h in range(DIFF_HEADS_PER_STEP) for tiles in _tile_streams(q_ref.shape[0] // tq)]
    first = {}

    def emit(tag, acc):
        h, i, second = tag
        o_n = acc[0:dv] * (1.0 / acc[dv:dv + 1])
        if not second:
            first[h, i] = o_n
            return
        rows, cols = slice(i * tq, (i + 1) * tq), slice(h * LANES, (h + 1) * LANES)
        o = first.pop((h, i)) - lam * o_n
        ms = jnp.mean(o * o, axis=0, keepdims=True)
        o = (o * lax.rsqrt(ms + DA_SUBLN_EPS)).T
        o = o * subw_ref[...] * (1.0 - lam_init)
        o_ref[rows, cols] = (o * g_ref[rows, cols].astype(jnp.float32)).astype(o_ref.dtype)

    _causal_attention(streams, tq, emit)


def _diff_attn(proj, lam_rows, subln_w, batch, seq, lam_init):
    heads, hps = 4, DIFF_HEADS_PER_STEP
    slab = lambda first: pl.BlockSpec((seq, hps * LANES), lambda b, h: (b, first // hps + h))
    return pl.pallas_call(
        functools.partial(_diff_attn_kernel, tq=ATTN_TQ, lam_init=lam_init),
        grid=(batch, heads // hps),
        in_specs=[
            slab(SLAB_AQ), slab(SLAB_AK), slab(SLAB_AV), slab(SLAB_AG),
            pl.BlockSpec((4, 64), lambda b, h: (0, 0)),
            pl.BlockSpec((1, LANES), lambda b, h: (0, 0)),
        ],
        out_specs=pl.BlockSpec((seq, hps * LANES), lambda b, h: (b, h)),
        out_shape=jax.ShapeDtypeStruct((batch * seq, heads * LANES), jnp.bfloat16),
        scratch_shapes=_attn_scratch(seq, hps),
        compiler_params=_cparams(("parallel", "parallel")),
        name="diff_attn",
    )(proj, proj, proj, proj, lam_rows, subln_w.reshape(1, LANES))


def _mla_attn_kernel(q_ref, k_ref, v_ref, *rest, tq):
    g_refs, (o_ref, vt_ref) = rest[:MLA_HEADS_PER_STEP], rest[MLA_HEADS_PER_STEP:]
    _transpose_v(v_ref, vt_ref)
    dv = LANES

    def item(h, i):
        cols = slice(h * 2 * LANES, (h + 1) * 2 * LANES)
        return (lambda: q_ref[i * tq:(i + 1) * tq, cols], lambda c: k_ref[c * tq:(c + 1) * tq, cols],
                lambda c: vt_ref[h, :, c * tq:(c + 1) * tq], i, (h, i))

    streams = [[item(h, i) for i in tiles]
               for h in range(MLA_HEADS_PER_STEP) for tiles in _tile_streams(q_ref.shape[0] // tq)]

    def emit(tag, acc):
        h, i = tag
        rows = slice(i * tq, (i + 1) * tq)
        o = (acc[0:dv] * (1.0 / acc[dv:dv + 1])).T
        o_ref[rows, h * LANES:(h + 1) * LANES] = (o * g_refs[h][rows, :].astype(jnp.float32)).astype(o_ref.dtype)

    _causal_attention(streams, tq, emit)


def _mla_attn(qf, kf, vc, proj, batch, seq):
    heads, hps = 4, MLA_HEADS_PER_STEP
    gate = lambda r: pl.BlockSpec((seq, LANES), lambda b, h: (b, SLAB_CG + hps * h + r))
    return pl.pallas_call(
        functools.partial(_mla_attn_kernel, tq=ATTN_TQ),
        grid=(batch, heads // hps),
        in_specs=[
            pl.BlockSpec((seq, hps * 2 * LANES), lambda b, h: (b, h)),
            pl.BlockSpec((seq, hps * 2 * LANES), lambda b, h: (b, h)),
            pl.BlockSpec((seq, hps * LANES), lambda b, h: (b, h)),
            *[gate(r) for r in range(hps)],
        ],
        out_specs=pl.BlockSpec((seq, hps * LANES), lambda b, h: (b, h)),
        out_shape=jax.ShapeDtypeStruct((batch * seq, heads * LANES), jnp.bfloat16),
        scratch_shapes=_attn_scratch(seq, hps),
        compiler_params=_cparams(("parallel", "parallel")),
        name="mla_attn",
    )(qf, kf, vc, *([proj] * hps))


def _mla_up_kernel(cq_ref, ckv_ref, ckr_ref, qnw_ref, kvnw_ref, wuq_ref, wukv_ref,
                   cos_ref, sa_ref, sb_ref, qf_ref, kf_ref, vc_ref, wq_ref, wkv_ref):
    def norm(ref, w_ref):
        c = ref[...].astype(jnp.float32)
        ms = jnp.mean(c * c, axis=-1, keepdims=True)
        return (c * lax.rsqrt(ms + NORM_EPS) * w_ref[...]).astype(jnp.bfloat16)

    scale = 192 ** -0.5 * LOG2E
    @pl.when(pl.program_id(0) == 0)
    def _():
        wq = wuq_ref[...]
        zero = jnp.zeros((wq.shape[0], 64), wq.dtype)
        wq = jnp.concatenate([piece for h in range(4) for piece in (wq[:, 192 * h:192 * (h + 1)], zero)], axis=1)
        wq_ref[...] = wq.astype(wq_ref.dtype)
        wkv_ref[...] = wukv_ref[...].astype(wkv_ref.dtype)

    q = jnp.dot(norm(cq_ref, qnw_ref), wq_ref[...], preferred_element_type=jnp.float32)
    kv = jnp.dot(norm(ckv_ref, kvnw_ref), wkv_ref[...], preferred_element_type=jnp.float32)
    kr = ckr_ref[...]
    for h in range(4):
        nope = q[:, 2 * h * LANES:(2 * h + 1) * LANES]
        rope = _rope(q[:, (2 * h + 1) * LANES:(2 * h + 2) * LANES], cos_ref[...], sa_ref[...], sb_ref[...])
        qf_ref[:, 2 * h * LANES:(2 * h + 1) * LANES] = (nope * scale).astype(qf_ref.dtype)
        qf_ref[:, (2 * h + 1) * LANES:(2 * h + 2) * LANES] = (rope * scale).astype(qf_ref.dtype)
        kf_ref[:, 2 * h * LANES:(2 * h + 1) * LANES] = kv[:, 2 * h * LANES:(2 * h + 1) * LANES].astype(kf_ref.dtype)
        kf_ref[:, (2 * h + 1) * LANES:(2 * h + 2) * LANES] = kr
        vc_ref[:, h * LANES:(h + 1) * LANES] = kv[:, (2 * h + 1) * LANES:(2 * h + 2) * LANES].astype(vc_ref.dtype)


def _mla_up(proj, qn_w, kvn_w, w_uq, w_ukv, layer, cos, sa, sb):
    m = proj.shape[0]
    tm = 1024
    tab = pl.BlockSpec((tm, LANES), lambda i: (i, 0))
    full = lambda shape: pl.BlockSpec(shape, lambda i: (0, 0))
    weight = lambda w: pl.BlockSpec((None,) + w.shape[1:], lambda i: (layer, 0, 0))
    return pl.pallas_call(
        _mla_up_kernel,
        grid=(m // tm,),
        in_specs=[
            pl.BlockSpec((tm, 4 * LANES), lambda i: (i, SLAB_CQ // 4)),
            pl.BlockSpec((tm, 4 * LANES), lambda i: (i, SLAB_CKV // 4)),
            pl.BlockSpec((tm, LANES), lambda i: (i, SLAB_CKR)),
            full((1, 512)), full((1, 512)), weight(w_uq), weight(w_ukv),
            tab, tab, tab,
        ],
        out_specs=[
            pl.BlockSpec((tm, 1024), lambda i: (i, 0)),
            pl.BlockSpec((tm, 1024), lambda i: (i, 0)),
            pl.BlockSpec((tm, 512), lambda i: (i, 0)),
        ],
        out_shape=[
            jax.ShapeDtypeStruct((m, 1024), jnp.bfloat16),
            jax.ShapeDtypeStruct((m, 1024), jnp.bfloat16),
            jax.ShapeDtypeStruct((m, 512), jnp.bfloat16),
        ],
        scratch_shapes=[pltpu.VMEM((512, 1024), jnp.bfloat16), pltpu.VMEM((512, 1024), jnp.bfloat16)],
        compiler_params=_cparams(("arbitrary",)),
        name="mla_up",
    )(proj, proj, proj, qn_w.reshape(1, 512), kvn_w.reshape(1, 512), w_uq, w_ukv, cos, sa, sb)


def _swa_kernel(sink_ref, q0_ref, q1_ref, kc_ref, kp_ref, vc_ref, vp_ref, gate_ref, o_ref, *, blocks):
    w = SW_WINDOW
    q_refs = (q0_ref, q1_ref)
    first = pl.program_id(1) == 0
    lane = lax.broadcasted_iota(jnp.int32, ((blocks + 1) * w, LANES), 1)
    lo = lane < 64

    kt = jnp.concatenate([kp_ref[...], kc_ref[...]], axis=0).astype(jnp.float32)
    kr = pltpu.roll(kt, 64, 1)
    k_halves = []
    for g in range(2):
        a, b = (kt, kr) if g == 0 else (kr, kt)
        k_halves.append((jnp.where(lo, a, 0.0).astype(jnp.bfloat16), jnp.where(lo, 0.0, b).astype(jnp.bfloat16)))

    vt = jnp.concatenate([vp_ref[...], vc_ref[...]], axis=0).astype(jnp.float32).T
    zeros = jnp.zeros((64, 2 * w), jnp.bfloat16)
    ones = jnp.ones((ONES_ROWS, 2 * w), jnp.bfloat16)
    no_ones = jnp.zeros((ONES_ROWS, 2 * w), jnp.bfloat16)

    key = lax.broadcasted_iota(jnp.int32, (2 * w, 4 * w), 0)
    qry = lax.broadcasted_iota(jnp.int32, (2 * w, 4 * w), 1) & (w - 1)
    rel = qry + w - key
    in_band = (rel >= 0) & (rel < w)
    first_key = jnp.where(first, w, 0)
    bias = jnp.where(in_band, 0.0, NEG_INF)
    bias_first = jnp.where(in_band & (key >= first_key), 0.0, NEG_INF)

    def scores(n, g):
        k_lo, k_hi = k_halves[g]
        rows = slice(n * w, (n + 2) * w)
        kk = jnp.concatenate([k_lo[rows], k_hi[rows]], axis=0)
        qs = jnp.concatenate(
            [q_refs[g][n * w:(n + 1) * w, s * LANES:(s + 1) * LANES] for s in range(4)], axis=0)
        st = lax.dot_general(kk, qs, _NT, preferred_element_type=jnp.float32)
        ps, sink_terms = [], []
        for e in range(2):
            blk = st[e * 2 * w:(e + 1) * 2 * w] + (bias_first if n == 0 else bias)
            sink = jnp.concatenate(
                [jnp.full((1, w), sink_ref[8 * g + 2 * s + e] * LOG2E, jnp.float32) for s in range(4)], axis=1)
            m = jnp.maximum(jnp.max(blk, axis=0, keepdims=True), sink)
            ps.append(jnp.exp2(blk - m).astype(jnp.bfloat16))
            sink_terms.append(jnp.exp2(sink - m))
        return jnp.concatenate(ps, axis=0), sink_terms

    def outputs(n, g, pt, sink_terms):
        vg = vt[64 * g:64 * (g + 1), n * w:(n + 2) * w].astype(jnp.bfloat16)
        lhs = jnp.concatenate([
            jnp.concatenate([vg, zeros], axis=1), jnp.concatenate([zeros, vg], axis=1),
            jnp.concatenate([ones, no_ones], axis=1), jnp.concatenate([no_ones, ones], axis=1)], axis=0)
        acc = jnp.dot(lhs, pt, preferred_element_type=jnp.float32)
        den_even = acc[128:129] + sink_terms[0]
        den_odd = acc[128 + ONES_ROWS:129 + ONES_ROWS] + sink_terms[1]
        ot = jnp.concatenate([acc[0:64] * (1.0 / den_even), acc[64:128] * (1.0 / den_odd)], axis=0)
        for s in range(4):
            slab = 4 * g + s
            gate = gate_ref[n * w:(n + 1) * w, slab * LANES:(slab + 1) * LANES]
            o = ot[:, s * w:(s + 1) * w].T
            o_ref[n * w:(n + 1) * w, slab * LANES:(slab + 1) * LANES] = (
                o * gate.astype(jnp.float32)).astype(o_ref.dtype)

    pending = None
    for item in [(n, g) for n in range(blocks) for g in range(2)] + [None]:
        computed = scores(*item) if item is not None else None
        if pending is not None:
            outputs(*pending[0], *pending[1])
        pending = (item, computed)


def _swa(proj, sinks, batch, seq):
    w = SW_WINDOW
    blocks = 16
    rows = blocks * w
    steps = seq // rows
    nb = seq // w
    cur = lambda slab: pl.BlockSpec((rows, LANES), lambda b, i: (b * steps + i, slab))
    prev = lambda slab: pl.BlockSpec(
        (w, LANES), lambda b, i: (jnp.maximum(b * nb + i * blocks - 1, 0), slab))
    wide = lambda width, slab: pl.BlockSpec((rows, width * LANES), lambda b, i: (b * steps + i, slab // width))
    return pl.pallas_call(
        functools.partial(_swa_kernel, blocks=blocks),
        grid=(batch, steps),
        in_specs=[
            pl.BlockSpec(memory_space=pltpu.SMEM),
            wide(4, SLAB_BQ_LO), wide(4, SLAB_BQ_HI),
            cur(SLAB_BK), prev(SLAB_BK), cur(SLAB_BV), prev(SLAB_BV),
            wide(8, SLAB_BG),
        ],
        out_specs=pl.BlockSpec((rows, 1024), lambda b, i: (b * steps + i, 0)),
        out_shape=jax.ShapeDtypeStruct((batch * seq, 1024), jnp.bfloat16),
        compiler_params=_cparams(("parallel", "arbitrary")),
        name="swa",
    )(sinks.astype(jnp.float32), proj, proj, proj, proj, proj, proj, proj)


OUT_SUB_ROWS = 256


def _out_proj_kernel(ya_ref, yb_ref, yc_ref, wf_ref, x_ref, postw_ref, o_ref, w_ref):
    @pl.when(pl.program_id(0) == 0)
    def _():
        w_ref[...] = wf_ref[...].astype(w_ref.dtype)

    for r in range(0, o_ref.shape[0], OUT_SUB_ROWS):
        rows = slice(r, r + OUT_SUB_ROWS)
        out = jnp.dot(ya_ref[rows, :], w_ref[0:512, :], preferred_element_type=jnp.float32)
        out += jnp.dot(yb_ref[rows, :], w_ref[512:1536, :], preferred_element_type=jnp.float32)
        out += jnp.dot(yc_ref[rows, :], w_ref[1536:2048, :], preferred_element_type=jnp.float32)
        ms = jnp.mean(out * out, axis=-1, keepdims=True)
        o_ref[rows, :] = x_ref[rows, :] + out * lax.rsqrt(ms + NORM_EPS) * postw_ref[...]


def _out_proj(ya, yb, yc, w_out, layer, x2, post_w):
    m = x2.shape[0]
    tm = 512
    row = lambda width: pl.BlockSpec((tm, width), lambda i: (i, 0))
    return pl.pallas_call(
        _out_proj_kernel,
        grid=(m // tm,),
        in_specs=[
            row(512), row(1024), row(512),
            pl.BlockSpec((None, D_MODEL, D_MODEL), lambda i: (layer, 0, 0), pipeline_mode=pl.Buffered(1)),
            row(D_MODEL),
            pl.BlockSpec((1, D_MODEL), lambda i: (0, 0)),
        ],
        out_specs=row(D_MODEL),
        out_shape=jax.ShapeDtypeStruct((m, D_MODEL), jnp.float32),
        scratch_shapes=[pltpu.VMEM((D_MODEL, D_MODEL), jnp.bfloat16)],
        compiler_params=_cparams(("arbitrary",)),
        name="out_proj",
    )(ya, yb, yc, w_out, x2, post_w.reshape(1, D_MODEL))


def kernel(x, positions, pre_norm_w, post_norm_w, w_in, diff_lambda_q1, diff_lambda_k1, diff_lambda_q2,
           diff_lambda_k2, diff_subln_w, sink_logits, mla_q_norm_w, mla_kv_norm_w, w_uq, w_ukv, w_out):
    batch, seq, d = x.shape
    depth = w_in.shape[0]
    cos, sa, sb = _rope_tables(positions)
    x2 = x.reshape(batch * seq, d)
    for layer in range(depth):
        lam_init = 0.8 - 0.6 * math.exp(-0.3 * layer)
        lam_rows = jnp.stack([diff_lambda_q1[layer], diff_lambda_k1[layer],
                              diff_lambda_q2[layer], diff_lambda_k2[layer]]).astype(jnp.float32)
        proj = _in_proj(x2, pre_norm_w[layer], w_in, layer, cos, sa, sb)
        ya = _diff_attn(proj, lam_rows, diff_subln_w[layer], batch, seq, lam_init)
        yb = _swa(proj, sink_logits[layer], batch, seq)
        qf, kf, vc = _mla_up(proj, mla_q_norm_w[layer], mla_kv_norm_w[layer],
                             w_uq, w_ukv, layer, cos, sa, sb)
        yc = _mla_attn(qf, kf, vc, proj, batch, seq)
        x2 = _out_proj(ya, yb, yc, w_out, layer, x2, post_norm_w[layer])
    return x2.reshape(batch, seq, d)
```

```python
import functools
import math

import jax
import jax.numpy as jnp
from jax import lax
from jax.experimental import pallas as pl
from jax.experimental.pallas import tpu as pltpu

D_MODEL = 2048
ROPE_THETA = 10000.0
NORM_EPS = 1e-6
NEG_INF = -1e30
LOG2E = math.log2(math.e)
DA_SUBLN_EPS = 1e-5
SW_WINDOW = 128
LANES = 128

IN_TILES = (
    ((0, 1, 2, 3, 8, 9, 10, 11), ("rope_q",) * 4 + ("plain",) * 4),
    ((4, 5, 6, 7, 34, 35, 36, 37), ("rope",) * 4 + ("plain",) * 4),
    ((16, 17, 18, 19, 38, 39, 40, 41), ("rope_q",) * 4 + ("plain",) * 4),
    ((20, 21, 22, 23, 12, 13, 14, 15), ("rope_q",) * 4 + ("silu",) * 4),
    ((26, 27, 28, 29, 30, 31, 32, 33), ("silu",) * 8),
    ((42, 43, 44, 45, 46, 24, 25, 25), ("kr",) + ("gate_c",) * 4 + ("rope", "plain", "skip")),
)
IN_TILE = 8 * LANES
N_SLABS = 8 * len(IN_TILES)
SLAB_AQ, SLAB_AV, SLAB_AK, SLAB_CQ = 0, 4, 8, 12
SLAB_BQ_LO, SLAB_CKV, SLAB_BQ_HI, SLAB_AG = 16, 20, 24, 28
SLAB_BG, SLAB_CKR, SLAB_CG, SLAB_BK, SLAB_BV = 32, 40, 41, 45, 46

VMEM_LIMIT = 56 * 1024 * 1024

_NT = (((1,), (1,)), ((), ()))


def _cparams(sem):
    return pltpu.CompilerParams(dimension_semantics=sem, vmem_limit_bytes=VMEM_LIMIT)


def _rope_table_kernel(pos_ref, freq_ref, cos_ref, sa_ref, sb_ref):
    ang = pos_ref[...].astype(jnp.float32) * freq_ref[...]
    c, s = jnp.cos(ang), jnp.sin(ang)
    quarter = ang.shape[0]
    lane = lax.broadcasted_iota(jnp.int32, ang.shape, 1)
    first_half = (lane % 64) < 32
    for r in range(4):
        rows = slice(r * quarter, (r + 1) * quarter)
        cr = jnp.tile(c[:, 32 * r:32 * (r + 1)], (1, 4))
        sr = jnp.tile(s[:, 32 * r:32 * (r + 1)], (1, 4))
        cos_ref[rows, :] = cr
        sa_ref[rows, :] = jnp.where(first_half, -sr, 0.0)
        sb_ref[rows, :] = jnp.where(first_half, 0.0, sr)


def _rope_tables(positions):
    m = positions.size
    tm = 1024
    pos_p = jnp.repeat(positions.reshape(m // tm, 4, tm // 4).transpose(0, 2, 1), 32, axis=2).reshape(m // 4, LANES)
    inv_freq = jnp.power(ROPE_THETA, -jnp.arange(0, 64, 2, dtype=jnp.float32) / 64)
    freq = jnp.tile(inv_freq, 4).reshape(1, LANES)
    spec = pl.BlockSpec((tm, LANES), lambda i: (i, 0))
    return pl.pallas_call(
        _rope_table_kernel,
        grid=(m // tm,),
        in_specs=[pl.BlockSpec((tm // 4, LANES), lambda i: (i, 0)), pl.BlockSpec((1, LANES), lambda i: (0, 0))],
        out_specs=[spec, spec, spec],
        out_shape=[jax.ShapeDtypeStruct((m, LANES), jnp.float32)] * 3,
        compiler_params=_cparams(("parallel",)),
        name="rope_tables",
    )(pos_p, freq)


def _rope(a, cos, sa, sb):
    return a * cos + pltpu.roll(a, 96, 1) * sa + pltpu.roll(a, 32, 1) * sb


def _silu(a):
    return a * (0.5 + 0.5 * jnp.tanh(0.5 * a))


def _in_proj_kernel(x_ref, prew_ref, *rest):
    w_refs, (cos_ref, sa_ref, sb_ref, o_ref, h_ref) = rest[:8], rest[8:]
    j, r = pl.program_id(1), pl.program_id(2)
    tm = x_ref.shape[0]

    @pl.when(j == 0)
    def _():
        x = x_ref[...]
        ms = jnp.mean(x * x, axis=-1, keepdims=True)
        h_ref[r] = (x * lax.rsqrt(ms + NORM_EPS) * prew_ref[...]).astype(jnp.bfloat16)

    def epilogue(kinds):
        w = jnp.concatenate([w_ref[...] for w_ref in w_refs], axis=0).astype(jnp.bfloat16)
        acc = lax.dot_general(h_ref[r], w, _NT, preferred_element_type=jnp.float32)
        slab = lambda s: acc[:, s * LANES:(s + 1) * LANES]
        lane = lax.broadcasted_iota(jnp.int32, (acc.shape[0], LANES), 1)
        rows = pl.ds(pl.multiple_of(r * tm, tm), tm)
        for s, kind in enumerate(kinds):
            a = slab(s)
            if kind == "skip":
                a = jnp.zeros_like(a)
            if kind in ("rope", "rope_q", "kr"):
                a = _rope(a, cos_ref[rows, :], sa_ref[rows, :], sb_ref[rows, :])
                if kind == "rope_q":
                    a = a * (0.125 * LOG2E)
            elif kind == "silu":
                a = _silu(a)
            elif kind == "gate_c":
                a = jnp.where(lane < 64, pltpu.roll(_silu(slab(s - 1)), 64, 1), pltpu.roll(_silu(a), 64, 1))
            o_ref[:, s * LANES:(s + 1) * LANES] = a.astype(o_ref.dtype)

    for t, (_, kinds) in enumerate(IN_TILES):
        pl.when(j == t)(functools.partial(epilogue, kinds))


def _in_proj(x2, pre_w, w_in, layer, cos, sa, sb):
    m = x2.shape[0]
    tm = 1024
    tab = pl.BlockSpec((2 * tm, LANES), lambda p, j, r: (p, 0))

    def slab_spec(position):
        sources = [slabs[position] for slabs, _ in IN_TILES]

        def index_map(p, j, r):
            source = sources[-1]
            for t in range(len(sources) - 2, -1, -1):
                source = jnp.where(j == t, sources[t], source)
            return layer, source, 0
        return pl.BlockSpec((None, LANES, D_MODEL), index_map)

    wt = jnp.swapaxes(w_in, 1, 2)
    return pl.pallas_call(
        _in_proj_kernel,
        grid=(m // (2 * tm), len(IN_TILES), 2),
        in_specs=[
            pl.BlockSpec((tm, D_MODEL), lambda p, j, r: (2 * p + jnp.where(j == 0, r, 1), 0)),
            pl.BlockSpec((1, D_MODEL), lambda p, j, r: (0, 0)),
            *[slab_spec(position) for position in range(8)],
            tab, tab, tab,
        ],
        out_specs=pl.BlockSpec((tm, IN_TILE), lambda p, j, r: (2 * p + r, j)),
        out_shape=jax.ShapeDtypeStruct((m, N_SLABS * LANES), jnp.bfloat16),
        scratch_shapes=[pltpu.VMEM((2, tm, D_MODEL), jnp.bfloat16)],
        compiler_params=_cparams(("parallel", "arbitrary", "arbitrary")),
        name="in_proj",
    )(x2, pre_w.reshape(1, D_MODEL), *([wt] * 8), cos, sa, sb)


ATTN_TQ = 256


ONES_ROWS = 16


def _transpose_v(v_ref, vt_ref):
    for h in range(vt_ref.shape[0]):
        vt_ref[h, 0:LANES, :] = v_ref[:, h * LANES:(h + 1) * LANES].astype(jnp.float32).T.astype(vt_ref.dtype)
        vt_ref[h, LANES:, :] = jnp.ones((ONES_ROWS, vt_ref.shape[2]), vt_ref.dtype)


ATTN_STREAMS = 4
DIFF_HEADS_PER_STEP = 2
MLA_HEADS_PER_STEP = 2


def _tile_streams(n_tiles):
    assert n_tiles == 2 * ATTN_STREAMS
    return [[n_tiles - 1 - r, r] for r in range(ATTN_STREAMS)]


def _causal_attention(streams, tq, emit):
    key = lax.broadcasted_iota(jnp.int32, (tq, tq), 0)
    qry = lax.broadcasted_iota(jnp.int32, (tq, tq), 1)
    on_or_below_diag = key <= qry
    flat = [[(item, c) for item in stream for c in range(item[3] + 1)] for stream in streams]
    q_t = {}

    def scores(p, k):
        if k >= len(flat[p]):
            return None
        (load_q, load_k, _, i, tag), c = flat[p][k]
        if c == 0:
            q_t[tag] = load_q().astype(jnp.float32).T.astype(jnp.bfloat16)
        s = jnp.dot(load_k(c), q_t[tag], preferred_element_type=jnp.float32)
        if c == i:
            s = jnp.where(on_or_below_diag, s, NEG_INF)
        return s.reshape(tq // 8, 8, tq)

    m_run = [None] * len(streams)
    acc = [None] * len(streams)
    s_next = [scores(p, 0) for p in range(len(streams))]
    for k in range(max(len(f) for f in flat)):
        s_cur, s_next = s_next, [scores(p, k + 1) for p in range(len(streams))]
        for p in range(len(streams)):
            if s_cur[p] is None:
                continue
            (_, _, load_vt, i, tag), c = flat[p][k]
            m = jnp.max(jnp.max(s_cur[p], axis=0), axis=0, keepdims=True)
            if c > 0:
                m = jnp.maximum(m_run[p], m)
            pt = jnp.exp2(s_cur[p] - jnp.broadcast_to(m, (8, tq))[None]).reshape(tq, tq).astype(jnp.bfloat16)
            d = jnp.dot(load_vt(c), pt, preferred_element_type=jnp.float32)
            acc[p] = d if c == 0 else acc[p] * jnp.exp2(m_run[p] - m) + d
            m_run[p] = m
            if c == i:
                emit(tag, acc[p])


def _attn_scratch(seq, heads_per_step):
    return [pltpu.VMEM((heads_per_step, LANES + ONES_ROWS, seq), jnp.bfloat16)]


def _diff_attn_kernel(q_ref, k_ref, v_ref, g_ref, lam_ref, subw_ref, o_ref, vt_ref, *, tq, lam_init):
    _transpose_v(v_ref, vt_ref)
    dv = LANES
    t = lam_ref[...]
    lam = (jnp.exp(jnp.sum(t[0:1] * t[1:2], axis=-1, keepdims=True))
           - jnp.exp(jnp.sum(t[2:3] * t[3:4], axis=-1, keepdims=True)) + lam_init)
    lane = lax.broadcasted_iota(jnp.int32, (tq, LANES), 1)

    def item(h, i, second):
        cols = slice(h * LANES, (h + 1) * LANES)

        def load_q():
            q = q_ref[i * tq:(i + 1) * tq, cols]
            keep = (lane >= 64) if second else (lane < 64)
            return jnp.where(keep, q, jnp.zeros_like(q))
        return (load_q, lambda c: k_ref[c * tq:(c + 1) * tq, cols], lambda c: vt_ref[h, :, c * tq:(c + 1) * tq],
                i, (h, i, second))

    streams = [[item(h, i, second) for i in tiles for second in (False, True)]
               for h in range(DIFF_HEADS_PER_STEP) for tiles in _tile_streams(q_ref.shape[0] // tq)]
    first = {}

    def emit(tag, acc):
        h, i, second = tag
        o_n = acc[0:dv] * (1.0 / acc[dv:dv + 1])
        if not second:
            first[h, i] = o_n
            return
        rows, cols = slice(i * tq, (i + 1) * tq), slice(h * LANES, (h + 1) * LANES)
        o = first.pop((h, i)) - lam * o_n
        ms = jnp.mean(o * o, axis=0, keepdims=True)
        o = (o * lax.rsqrt(ms + DA_SUBLN_EPS)).T
        o = o * subw_ref[...] * (1.0 - lam_init)
        o_ref[rows, cols] = (o * g_ref[rows, cols].astype(jnp.float32)).astype(o_ref.dtype)

    _causal_attention(streams, tq, emit)


def _diff_attn(proj, lam_rows, subln_w, batch, seq, lam_init):
    heads, hps = 4, DIFF_HEADS_PER_STEP
    slab = lambda first: pl.BlockSpec((seq, hps * LANES), lambda b, h: (b, first // hps + h))
    return pl.pallas_call(
        functools.partial(_diff_attn_kernel, tq=ATTN_TQ, lam_init=lam_init),
        grid=(batch, heads // hps),
        in_specs=[
            slab(SLAB_AQ), slab(SLAB_AK), slab(SLAB_AV), slab(SLAB_AG),
            pl.BlockSpec((4, 64), lambda b, h: (0, 0)),
            pl.BlockSpec((1, LANES), lambda b, h: (0, 0)),
        ],
        out_specs=pl.BlockSpec((seq, hps * LANES), lambda b, h: (b, h)),
        out_shape=jax.ShapeDtypeStruct((batch * seq, heads * LANES), jnp.bfloat16),
        scratch_shapes=_attn_scratch(seq, hps),
        compiler_params=_cparams(("parallel", "parallel")),
        name="diff_attn",
    )(proj, proj, proj, proj, lam_rows, subln_w.reshape(1, LANES))


def _mla_attn_kernel(q_ref, k_ref, v_ref, *rest, tq):
    g_refs, (o_ref, vt_ref) = rest[:MLA_HEADS_PER_STEP], rest[MLA_HEADS_PER_STEP:]
    _transpose_v(v_ref, vt_ref)
    dv = LANES

    def item(h, i):
        cols = slice(h * 2 * LANES, (h + 1) * 2 * LANES)
        return (lambda: q_ref[i * tq:(i + 1) * tq, cols], lambda c: k_ref[c * tq:(c + 1) * tq, cols],
                lambda c: vt_ref[h, :, c * tq:(c + 1) * tq], i, (h, i))

    streams = [[item(h, i) for i in tiles]
               for h in range(MLA_HEADS_PER_STEP) for tiles in _tile_streams(q_ref.shape[0] // tq)]

    def emit(tag, acc):
        h, i = tag
        rows = slice(i * tq, (i + 1) * tq)
        o = (acc[0:dv] * (1.0 / acc[dv:dv + 1])).T
        o_ref[rows, h * LANES:(h + 1) * LANES] = (o * g_refs[h][rows, :].astype(jnp.float32)).astype(o_ref.dtype)

    _causal_attention(streams, tq, emit)


def _mla_attn(qf, kf, vc, proj, batch, seq):
    heads, hps = 4, MLA_HEADS_PER_STEP
    gate = lambda r: pl.BlockSpec((seq, LANES), lambda b, h: (b, SLAB_CG + hps * h + r))
    return pl.pallas_call(
        functools.partial(_mla_attn_kernel, tq=ATTN_TQ),
        grid=(batch, heads // hps),
        in_specs=[
            pl.BlockSpec((seq, hps * 2 * LANES), lambda b, h: (b, h)),
            pl.BlockSpec((seq, hps * 2 * LANES), lambda b, h: (b, h)),
            pl.BlockSpec((seq, hps * LANES), lambda b, h: (b, h)),
            *[gate(r) for r in range(hps)],
        ],
        out_specs=pl.BlockSpec((seq, hps * LANES), lambda b, h: (b, h)),
        out_shape=jax.ShapeDtypeStruct((batch * seq, heads * LANES), jnp.bfloat16),
        scratch_shapes=_attn_scratch(seq, hps),
        compiler_params=_cparams(("parallel", "parallel")),
        name="mla_attn",
    )(qf, kf, vc, *([proj] * hps))


def _mla_up_kernel(cq_ref, ckv_ref, ckr_ref, qnw_ref, kvnw_ref, wuq_ref, wukv_ref,
                   cos_ref, sa_ref, sb_ref, qf_ref, kf_ref, vc_ref, wq_ref, wkv_ref):
    def norm(ref, w_ref):
        c = ref[...].astype(jnp.float32)
        ms = jnp.mean(c * c, axis=-1, keepdims=True)
        return (c * lax.rsqrt(ms + NORM_EPS) * w_ref[...]).astype(jnp.bfloat16)

    scale = 192 ** -0.5 * LOG2E
    @pl.when(pl.program_id(0) == 0)
    def _():
        wq = wuq_ref[...]
        zero = jnp.zeros((wq.shape[0], 64), wq.dtype)
        wq = jnp.concatenate([piece for h in range(4) for piece in (wq[:, 192 * h:192 * (h + 1)], zero)], axis=1)
        wq_ref[...] = wq.astype(wq_ref.dtype)
        wkv_ref[...] = wukv_ref[...].astype(wkv_ref.dtype)

    q = jnp.dot(norm(cq_ref, qnw_ref), wq_ref[...], preferred_element_type=jnp.float32)
    kv = jnp.dot(norm(ckv_ref, kvnw_ref), wkv_ref[...], preferred_element_type=jnp.float32)
    kr = ckr_ref[...]
    for h in range(4):
        nope = q[:, 2 * h * LANES:(2 * h + 1) * LANES]
        rope = _rope(q[:, (2 * h + 1) * LANES:(2 * h + 2) * LANES], cos_ref[...], sa_ref[...], sb_ref[...])
        qf_ref[:, 2 * h * LANES:(2 * h + 1) * LANES] = (nope * scale).astype(qf_ref.dtype)
        qf_ref[:, (2 * h + 1) * LANES:(2 * h + 2) * LANES] = (rope * scale).astype(qf_ref.dtype)
        kf_ref[:, 2 * h * LANES:(2 * h + 1) * LANES] = kv[:, 2 * h * LANES:(2 * h + 1) * LANES].astype(kf_ref.dtype)
        kf_ref[:, (2 * h + 1) * LANES:(2 * h + 2) * LANES] = kr
        vc_ref[:, h * LANES:(h + 1) * LANES] = kv[:, (2 * h + 1) * LANES:(2 * h + 2) * LANES].astype(vc_ref.dtype)


def _mla_up(proj, qn_w, kvn_w, w_uq, w_ukv, layer, cos, sa, sb):
    m = proj.shape[0]
    tm = 1024
    tab = pl.BlockSpec((tm, LANES), lambda i: (i, 0))
    full = lambda shape: pl.BlockSpec(shape, lambda i: (0, 0))
    weight = lambda w: pl.BlockSpec((None,) + w.shape[1:], lambda i: (layer, 0, 0))
    return pl.pallas_call(
        _mla_up_kernel,
        grid=(m // tm,),
        in_specs=[
            pl.BlockSpec((tm, 4 * LANES), lambda i: (i, SLAB_CQ // 4)),
            pl.BlockSpec((tm, 4 * LANES), lambda i: (i, SLAB_CKV // 4)),
            pl.BlockSpec((tm, LANES), lambda i: (i, SLAB_CKR)),
            full((1, 512)), full((1, 512)), weight(w_uq), weight(w_ukv),
            tab, tab, tab,
        ],
        out_specs=[
            pl.BlockSpec((tm, 1024), lambda i: (i, 0)),
            pl.BlockSpec((tm, 1024), lambda i: (i, 0)),
            pl.BlockSpec((tm, 512), lambda i: (i, 0)),
        ],
        out_shape=[
            jax.ShapeDtypeStruct((m, 1024), jnp.bfloat16),
            jax.ShapeDtypeStruct((m, 1024), jnp.bfloat16),
            jax.ShapeDtypeStruct((m, 512), jnp.bfloat16),
        ],
        scratch_shapes=[pltpu.VMEM((512, 1024), jnp.bfloat16), pltpu.VMEM((512, 1024), jnp.bfloat16)],
        compiler_params=_cparams(("arbitrary",)),
        name="mla_up",
    )(proj, proj, proj, qn_w.reshape(1, 512), kvn_w.reshape(1, 512), w_uq, w_ukv, cos, sa, sb)


def _swa_kernel(sink_ref, q0_ref, q1_ref, kc_ref, kp_ref, vc_ref, vp_ref, gate_ref, o_ref, *, blocks):
    w = SW_WINDOW
    q_refs = (q0_ref, q1_ref)
    first = pl.program_id(1) == 0
    lane = lax.broadcasted_iota(jnp.int32, ((blocks + 1) * w, LANES), 1)
    lo = lane < 64

    kt = jnp.concatenate([kp_ref[...], kc_ref[...]], axis=0).astype(jnp.float32)
    kr = pltpu.roll(kt, 64, 1)
    k_halves = []
    for g in range(2):
        a, b = (kt, kr) if g == 0 else (kr, kt)
        k_halves.append((jnp.where(lo, a, 0.0).astype(jnp.bfloat16), jnp.where(lo, 0.0, b).astype(jnp.bfloat16)))

    vt = jnp.concatenate([vp_ref[...], vc_ref[...]], axis=0).astype(jnp.float32).T
    zeros = jnp.zeros((64, 2 * w), jnp.bfloat16)
    ones = jnp.ones((ONES_ROWS, 2 * w), jnp.bfloat16)
    no_ones = jnp.zeros((ONES_ROWS, 2 * w), jnp.bfloat16)

    key = lax.broadcasted_iota(jnp.int32, (2 * w, 4 * w), 0)
    qry = lax.broadcasted_iota(jnp.int32, (2 * w, 4 * w), 1) & (w - 1)
    rel = qry + w - key
    in_band = (rel >= 0) & (rel < w)
    first_key = jnp.where(first, w, 0)
    bias = jnp.where(in_band, 0.0, NEG_INF)
    bias_first = jnp.where(in_band & (key >= first_key), 0.0, NEG_INF)

    def scores(n, g):
        k_lo, k_hi = k_halves[g]
        rows = slice(n * w, (n + 2) * w)
        kk = jnp.concatenate([k_lo[rows], k_hi[rows]], axis=0)
        qs = jnp.concatenate(
            [q_refs[g][n * w:(n + 1) * w, s * LANES:(s + 1) * LANES] for s in range(4)], axis=0)
        st = lax.dot_general(kk, qs, _NT, preferred_element_type=jnp.float32)
        ps, sink_terms = [], []
        for e in range(2):
            blk = st[e * 2 * w:(e + 1) * 2 * w] + (bias_first if n == 0 else bias)
            sink = jnp.concatenate(
                [jnp.full((1, w), sink_ref[8 * g + 2 * s + e] * LOG2E, jnp.float32) for s in range(4)], axis=1)
            m = jnp.maximum(jnp.max(blk, axis=0, keepdims=True), sink)
            ps.append(jnp.exp2(blk - m).astype(jnp.bfloat16))
            sink_terms.append(jnp.exp2(sink - m))
        return jnp.concatenate(ps, axis=0), sink_terms

    def outputs(n, g, pt, sink_terms):
        vg = vt[64 * g:64 * (g + 1), n * w:(n + 2) * w].astype(jnp.bfloat16)
        lhs = jnp.concatenate([
            jnp.concatenate([vg, zeros], axis=1), jnp.concatenate([zeros, vg], axis=1),
            jnp.concatenate([ones, no_ones], axis=1), jnp.concatenate([no_ones, ones], axis=1)], axis=0)
        acc = jnp.dot(lhs, pt, preferred_element_type=jnp.float32)
        den_even = acc[128:129] + sink_terms[0]
        den_odd = acc[128 + ONES_ROWS:129 + ONES_ROWS] + sink_terms[1]
        ot = jnp.concatenate([acc[0:64] * (1.0 / den_even), acc[64:128] * (1.0 / den_odd)], axis=0)
        for s in range(4):
            slab = 4 * g + s
            gate = gate_ref[n * w:(n + 1) * w, slab * LANES:(slab + 1) * LANES]
            o = ot[:, s * w:(s + 1) * w].T
            o_ref[n * w:(n + 1) * w, slab * LANES:(slab + 1) * LANES] = (
                o * gate.astype(jnp.float32)).astype(o_ref.dtype)

    pending = None
    for item in [(n, g) for n in range(blocks) for g in range(2)] + [None]:
        computed = scores(*item) if item is not None else None
        if pending is not None:
            outputs(*pending[0], *pending[1])
        pending = (item, computed)


def _swa(proj, sinks, batch, seq):
    w = SW_WINDOW
    blocks = 16
    rows = blocks * w
    steps = seq // rows
    nb = seq // w
    cur = lambda slab: pl.BlockSpec((rows, LANES), lambda b, i: (b * steps + i, slab))
    prev = lambda slab: pl.BlockSpec(
        (w, LANES), lambda b, i: (jnp.maximum(b * nb + i * blocks - 1, 0), slab))
    wide = lambda width, slab: pl.BlockSpec((rows, width * LANES), lambda b, i: (b * steps + i, slab // width))
    return pl.pallas_call(
        functools.partial(_swa_kernel, blocks=blocks),
        grid=(batch, steps),
        in_specs=[
            pl.BlockSpec(memory_space=pltpu.SMEM),
            wide(4, SLAB_BQ_LO), wide(4, SLAB_BQ_HI),
            cur(SLAB_BK), prev(SLAB_BK), cur(SLAB_BV), prev(SLAB_BV),
            wide(8, SLAB_BG),
        ],
        out_specs=pl.BlockSpec((rows, 1024), lambda b, i: (b * steps + i, 0)),
        out_shape=jax.ShapeDtypeStruct((batch * seq, 1024), jnp.bfloat16),
        compiler_params=_cparams(("parallel", "arbitrary")),
        name="swa",
    )(sinks.astype(jnp.float32), proj, proj, proj, proj, proj, proj, proj)


OUT_SUB_ROWS = 256


def _out_proj_kernel(ya_ref, yb_ref, yc_ref, wf_ref, x_ref, postw_ref, o_ref, w_ref):
    @pl.when(pl.program_id(0) == 0)
    def _():
        w_ref[...] = wf_ref[...].astype(w_ref.dtype)

    for r in range(0, o_ref.shape[0], OUT_SUB_ROWS):
        rows = slice(r, r + OUT_SUB_ROWS)
        out = jnp.dot(ya_ref[rows, :], w_ref[0:512, :], preferred_element_type=jnp.float32)
        out += jnp.dot(yb_ref[rows, :], w_ref[512:1536, :], preferred_element_type=jnp.float32)
        out += jnp.dot(yc_ref[rows, :], w_ref[1536:2048, :], preferred_element_type=jnp.float32)
        ms = jnp.mean(out * out, axis=-1, keepdims=True)
        o_ref[rows, :] = x_ref[rows, :] + out * lax.rsqrt(ms + NORM_EPS) * postw_ref[...]


def _out_proj(ya, yb, yc, w_out, layer, x2, post_w):
    m = x2.shape[0]
    tm = 512
    row = lambda width: pl.BlockSpec((tm, width), lambda i: (i, 0))
    return pl.pallas_call(
        _out_proj_kernel,
        grid=(m // tm,),
        in_specs=[
            row(512), row(1024), row(512),
            pl.BlockSpec((None, D_MODEL, D_MODEL), lambda i: (layer, 0, 0), pipeline_mode=pl.Buffered(1)),
            row(D_MODEL),
            pl.BlockSpec((1, D_MODEL), lambda i: (0, 0)),
        ],
        out_specs=row(D_MODEL),
        out_shape=jax.ShapeDtypeStruct((m, D_MODEL), jnp.float32),
        scratch_shapes=[pltpu.VMEM((D_MODEL, D_MODEL), jnp.bfloat16)],
        compiler_params=_cparams(("arbitrary",)),
        name="out_proj",
    )(ya, yb, yc, w_out, x2, post_w.reshape(1, D_MODEL))


def kernel(x, positions, pre_norm_w, post_norm_w, w_in, diff_lambda_q1, diff_lambda_k1, diff_lambda_q2,
           diff_lambda_k2, diff_subln_w, sink_logits, mla_q_norm_w, mla_kv_norm_w, w_uq, w_ukv, w_out):
    batch, seq, d = x.shape
    depth = w_in.shape[0]
    cos, sa, sb = _rope_tables(positions)
    x2 = x.reshape(batch * seq, d)
    for layer in range(depth):
        lam_init = 0.8 - 0.6 * math.exp(-0.3 * layer)
        lam_rows = jnp.stack([diff_lambda_q1[layer], diff_lambda_k1[layer],
                              diff_lambda_q2[layer], diff_lambda_k2[layer]]).astype(jnp.float32)
        proj = _in_proj(x2, pre_norm_w[layer], w_in, layer, cos, sa, sb)
        ya = _diff_attn(proj, lam_rows, diff_subln_w[layer], batch, seq, lam_init)
        yb = _swa(proj, sink_logits[layer], batch, seq)
        qf, kf, vc = _mla_up(proj, mla_q_norm_w[layer], mla_kv_norm_w[layer],
                             w_uq, w_ukv, layer, cos, sa, sb)
        yc = _mla_attn(qf, kf, vc, proj, batch, seq)
        x2 = _out_proj(ya, yb, yc, w_out, layer, x2, post_norm_w[layer])
    return x2.reshape(batch, seq, d)
```

```python
import functools
import math

import jax
import jax.numpy as jnp
from jax import lax
from jax.experimental import pallas as pl
from jax.experimental.pallas import tpu as pltpu

D_MODEL = 2048
ROPE_THETA = 10000.0
NORM_EPS = 1e-6
NEG_INF = -1e30
LOG2E = math.log2(math.e)
DA_SUBLN_EPS = 1e-5
SW_WINDOW = 128
LANES = 128

IN_TILES = (
    ((0, 1, 2, 3, 8, 9, 10, 11), ("rope_q",) * 4 + ("plain",) * 4),
    ((4, 5, 6, 7, 34, 35, 36, 37), ("rope",) * 4 + ("plain",) * 4),
    ((16, 17, 18, 19, 38, 39, 40, 41), ("rope_q",) * 4 + ("plain",) * 4),
    ((20, 21, 22, 23, 12, 13, 14, 15), ("rope_q",) * 4 + ("silu",) * 4),
    ((26, 27, 28, 29, 30, 31, 32, 33), ("silu",) * 8),
    ((42, 43, 44, 45, 46, 24, 25, 25), ("kr",) + ("gate_c",) * 4 + ("rope", "plain", "skip")),
)
IN_TILE = 8 * LANES
N_SLABS = 8 * len(IN_TILES)
SLAB_AQ, SLAB_AV, SLAB_AK, SLAB_CQ = 0, 4, 8, 12
SLAB_BQ_LO, SLAB_CKV, SLAB_BQ_HI, SLAB_AG = 16, 20, 24, 28
SLAB_BG, SLAB_CKR, SLAB_CG, SLAB_BK, SLAB_BV = 32, 40, 41, 45, 46

VMEM_LIMIT = 56 * 1024 * 1024

_NT = (((1,), (1,)), ((), ()))


def _cparams(sem):
    return pltpu.CompilerParams(dimension_semantics=sem, vmem_limit_bytes=VMEM_LIMIT)


def _rope_table_kernel(pos_ref, freq_ref, cos_ref, sa_ref, sb_ref):
    ang = pos_ref[...].astype(jnp.float32) * freq_ref[...]
    c, s = jnp.cos(ang), jnp.sin(ang)
    quarter = ang.shape[0]
    lane = lax.broadcasted_iota(jnp.int32, ang.shape, 1)
    first_half = (lane % 64) < 32
    for r in range(4):
        rows = slice(r * quarter, (r + 1) * quarter)
        cr = jnp.tile(c[:, 32 * r:32 * (r + 1)], (1, 4))
        sr = jnp.tile(s[:, 32 * r:32 * (r + 1)], (1, 4))
        cos_ref[rows, :] = cr
        sa_ref[rows, :] = jnp.where(first_half, -sr, 0.0)
        sb_ref[rows, :] = jnp.where(first_half, 0.0, sr)


def _rope_tables(positions):
    m = positions.size
    tm = 1024
    pos_p = jnp.repeat(positions.reshape(m // tm, 4, tm // 4).transpose(0, 2, 1), 32, axis=2).reshape(m // 4, LANES)
    inv_freq = jnp.power(ROPE_THETA, -jnp.arange(0, 64, 2, dtype=jnp.float32) / 64)
    freq = jnp.tile(inv_freq, 4).reshape(1, LANES)
    spec = pl.BlockSpec((tm, LANES), lambda i: (i, 0))
    return pl.pallas_call(
        _rope_table_kernel,
        grid=(m // tm,),
        in_specs=[pl.BlockSpec((tm // 4, LANES), lambda i: (i, 0)), pl.BlockSpec((1, LANES), lambda i: (0, 0))],
        out_specs=[spec, spec, spec],
        out_shape=[jax.ShapeDtypeStruct((m, LANES), jnp.float32)] * 3,
        compiler_params=_cparams(("parallel",)),
        name="rope_tables",
    )(pos_p, freq)


def _rope(a, cos, sa, sb):
    return a * cos + pltpu.roll(a, 96, 1) * sa + pltpu.roll(a, 32, 1) * sb


def _silu(a):
    return a * (0.5 + 0.5 * jnp.tanh(0.5 * a))


def _in_proj_kernel(x_ref, prew_ref, *rest):
    w_refs, (cos_ref, sa_ref, sb_ref, o_ref, h_ref) = rest[:8], rest[8:]
    j, r = pl.program_id(1), pl.program_id(2)
    tm = x_ref.shape[0]

    @pl.when(j == 0)
    def _():
        x = x_ref[...]
        ms = jnp.mean(x * x, axis=-1, keepdims=True)
        h_ref[r] = (x * lax.rsqrt(ms + NORM_EPS) * prew_ref[...]).astype(jnp.bfloat16)

    def epilogue(kinds):
        w = jnp.concatenate([w_ref[...] for w_ref in w_refs], axis=0).astype(jnp.bfloat16)
        acc = lax.dot_general(h_ref[r], w, _NT, preferred_element_type=jnp.float32)
        slab = lambda s: acc[:, s * LANES:(s + 1) * LANES]
        lane = lax.broadcasted_iota(jnp.int32, (acc.shape[0], LANES), 1)
        rows = pl.ds(pl.multiple_of(r * tm, tm), tm)
        for s, kind in enumerate(kinds):
            a = slab(s)
            if kind == "skip":
                a = jnp.zeros_like(a)
            if kind in ("rope", "rope_q", "kr"):
                a = _rope(a, cos_ref[rows, :], sa_ref[rows, :], sb_ref[rows, :])
                if kind == "rope_q":
                    a = a * (0.125 * LOG2E)
            elif kind == "silu":
                a = _silu(a)
            elif kind == "gate_c":
                a = jnp.where(lane < 64, pltpu.roll(_silu(slab(s - 1)), 64, 1), pltpu.roll(_silu(a), 64, 1))
            o_ref[:, s * LANES:(s + 1) * LANES] = a.astype(o_ref.dtype)

    for t, (_, kinds) in enumerate(IN_TILES):
        pl.when(j == t)(functools.partial(epilogue, kinds))


def _in_proj(x2, pre_w, w_in, layer, cos, sa, sb):
    m = x2.shape[0]
    tm = 1024
    tab = pl.BlockSpec((2 * tm, LANES), lambda p, j, r: (p, 0))

    def slab_spec(position):
        sources = [slabs[position] for slabs, _ in IN_TILES]

        def index_map(p, j, r):
            source = sources[-1]
            for t in range(len(sources) - 2, -1, -1):
                source = jnp.where(j == t, sources[t], source)
            return layer, source, 0
        return pl.BlockSpec((None, LANES, D_MODEL), index_map)

    wt = jnp.swapaxes(w_in, 1, 2)
    return pl.pallas_call(
        _in_proj_kernel,
        grid=(m // (2 * tm), len(IN_TILES), 2),
        in_specs=[
            pl.BlockSpec((tm, D_MODEL), lambda p, j, r: (2 * p + jnp.where(j == 0, r, 1), 0)),
            pl.BlockSpec((1, D_MODEL), lambda p, j, r: (0, 0)),
            *[slab_spec(position) for position in range(8)],
            tab, tab, tab,
        ],
        out_specs=pl.BlockSpec((tm, IN_TILE), lambda p, j, r: (2 * p + r, j)),
        out_shape=jax.ShapeDtypeStruct((m, N_SLABS * LANES), jnp.bfloat16),
        scratch_shapes=[pltpu.VMEM((2, tm, D_MODEL), jnp.bfloat16)],
        compiler_params=_cparams(("parallel", "arbitrary", "arbitrary")),
        name="in_proj",
    )(x2, pre_w.reshape(1, D_MODEL), *([wt] * 8), cos, sa, sb)


ATTN_TQ = 256


ONES_ROWS = 16


def _transpose_v(v_ref, vt_ref):
    for h in range(vt_ref.shape[0]):
        vt_ref[h, 0:LANES, :] = v_ref[:, h * LANES:(h + 1) * LANES].astype(jnp.float32).T.astype(vt_ref.dtype)
        vt_ref[h, LANES:, :] = jnp.ones((ONES_ROWS, vt_ref.shape[2]), vt_ref.dtype)


ATTN_STREAMS = 4
DIFF_HEADS_PER_STEP = 2
MLA_HEADS_PER_STEP = 2


def _tile_streams(n_tiles):
    assert n_tiles == 2 * ATTN_STREAMS
    return [[n_tiles - 1 - r, r] for r in range(ATTN_STREAMS)]


def _causal_attention(streams, tq, emit):
    key = lax.broadcasted_iota(jnp.int32, (tq, tq), 0)
    qry = lax.broadcasted_iota(jnp.int32, (tq, tq), 1)
    on_or_below_diag = key <= qry
    flat = [[(item, c) for item in stream for c in range(item[3] + 1)] for stream in streams]
    q_t = {}

    def scores(p, k):
        if k >= len(flat[p]):
            return None
        (load_q, load_k, _, i, tag), c = flat[p][k]
        if c == 0:
            q_t[tag] = load_q().astype(jnp.float32).T.astype(jnp.bfloat16)
        s = jnp.dot(load_k(c), q_t[tag], preferred_element_type=jnp.float32)
        if c == i:
            s = jnp.where(on_or_below_diag, s, NEG_INF)
        return s.reshape(tq // 8, 8, tq)

    m_run = [None] * len(streams)
    acc = [None] * len(streams)
    s_next = [scores(p, 0) for p in range(len(streams))]
    for k in range(max(len(f) for f in flat)):
        s_cur, s_next = s_next, [scores(p, k + 1) for p in range(len(streams))]
        for p in range(len(streams)):
            if s_cur[p] is None:
                continue
            (_, _, load_vt, i, tag), c = flat[p][k]
            m = jnp.max(jnp.max(s_cur[p], axis=0), axis=0, keepdims=True)
            if c > 0:
                m = jnp.maximum(m_run[p], m)
            pt = jnp.exp2(s_cur[p] - jnp.broadcast_to(m, (8, tq))[None]).reshape(tq, tq).astype(jnp.bfloat16)
            d = jnp.dot(load_vt(c), pt, preferred_element_type=jnp.float32)
            acc[p] = d if c == 0 else acc[p] * jnp.exp2(m_run[p] - m) + d
            m_run[p] = m
            if c == i:
                emit(tag, acc[p])


def _attn_scratch(seq, heads_per_step):
    return [pltpu.VMEM((heads_per_step, LANES + ONES_ROWS, seq), jnp.bfloat16)]


def _diff_attn_kernel(q_ref, k_ref, v_ref, g_ref, lam_ref, subw_ref, o_ref, vt_ref, *, tq, lam_init):
    _transpose_v(v_ref, vt_ref)
    dv = LANES
    t = lam_ref[...]
    lam = (jnp.exp(jnp.sum(t[0:1] * t[1:2], axis=-1, keepdims=True))
           - jnp.exp(jnp.sum(t[2:3] * t[3:4], axis=-1, keepdims=True)) + lam_init)
    lane = lax.broadcasted_iota(jnp.int32, (tq, LANES), 1)

    def item(h, i, second):
        cols = slice(h * LANES, (h + 1) * LANES)

        def load_q():
            q = q_ref[i * tq:(i + 1) * tq, cols]
            keep = (lane >= 64) if second else (lane < 64)
            return jnp.where(keep, q, jnp.zeros_like(q))
        return (load_q, lambda c: k_ref[c * tq:(c + 1) * tq, cols], lambda c: vt_ref[h, :, c * tq:(c + 1) * tq],
                i, (h, i, second))

    streams = [[item(h, i, second) for i in tiles for second in (False, True)]
               for h in range(DIFF_HEADS_PER_STEP) for tiles in _tile_streams(q_ref.shape[0] // tq)]
    first = {}

    def emit(tag, acc):
        h, i, second = tag
        o_n = acc[0:dv] * (1.0 / acc[dv:dv + 1])
        if not second:
            first[h, i] = o_n
            return
        rows, cols = slice(i * tq, (i + 1) * tq), slice(h * LANES, (h + 1) * LANES)
        o = first.pop((h, i)) - lam * o_n
        ms = jnp.mean(o * o, axis=0, keepdims=True)
        o = (o * lax.rsqrt(ms + DA_SUBLN_EPS)).T
        o = o * subw_ref[...] * (1.0 - lam_init)
        o_ref[rows, cols] = (o * g_ref[rows, cols].astype(jnp.float32)).astype(o_ref.dtype)

    _causal_attention(streams, tq, emit)


def _diff_attn(proj, lam_rows, subln_w, batch, seq, lam_init):
    heads, hps = 4, DIFF_HEADS_PER_STEP
    slab = lambda first: pl.BlockSpec((seq, hps * LANES), lambda b, h: (b, first // hps + h))
    return pl.pallas_call(
        functools.partial(_diff_attn_kernel, tq=ATTN_TQ, lam_init=lam_init),
        grid=(batch, heads // hps),
        in_specs=[
            slab(SLAB_AQ), slab(SLAB_AK), slab(SLAB_AV), slab(SLAB_AG),
            pl.BlockSpec((4, 64), lambda b, h: (0, 0)),
            pl.BlockSpec((1, LANES), lambda b, h: (0, 0)),
        ],
        out_specs=pl.BlockSpec((seq, hps * LANES), lambda b, h: (b, h)),
        out_shape=jax.ShapeDtypeStruct((batch * seq, heads * LANES), jnp.bfloat16),
        scratch_shapes=_attn_scratch(seq, hps),
        compiler_params=_cparams(("parallel", "parallel")),
        name="diff_attn",
    )(proj, proj, proj, proj, lam_rows, subln_w.reshape(1, LANES))


def _mla_attn_kernel(q_ref, kn_ref, kr_ref, v_ref, *rest, tq):
    g_refs, (o_ref, vt_ref) = rest[:MLA_HEADS_PER_STEP], rest[MLA_HEADS_PER_STEP:]
    _transpose_v(v_ref, vt_ref)
    dv = LANES

    def item(h, i):
        cols = slice(h * 2 * LANES, (h + 1) * 2 * LANES)

        def load_k(c):
            rows = slice(c * tq, (c + 1) * tq)
            return jnp.concatenate([kn_ref[rows, h * LANES:(h + 1) * LANES], kr_ref[rows, :]], axis=1)
        return (lambda: q_ref[i * tq:(i + 1) * tq, cols], load_k,
                lambda c: vt_ref[h, :, c * tq:(c + 1) * tq], i, (h, i))

    streams = [[item(h, i) for i in tiles]
               for h in range(MLA_HEADS_PER_STEP) for tiles in _tile_streams(q_ref.shape[0] // tq)]

    def emit(tag, acc):
        h, i = tag
        rows = slice(i * tq, (i + 1) * tq)
        o = (acc[0:dv] * (1.0 / acc[dv:dv + 1])).T
        o_ref[rows, h * LANES:(h + 1) * LANES] = (o * g_refs[h][rows, :].astype(jnp.float32)).astype(o_ref.dtype)

    _causal_attention(streams, tq, emit)


def _mla_attn(qf, kn, vc, proj, batch, seq):
    heads, hps = 4, MLA_HEADS_PER_STEP
    gate = lambda r: pl.BlockSpec((seq, LANES), lambda b, h: (b, SLAB_CG + hps * h + r))
    return pl.pallas_call(
        functools.partial(_mla_attn_kernel, tq=ATTN_TQ),
        grid=(batch, heads // hps),
        in_specs=[
            pl.BlockSpec((seq, hps * 2 * LANES), lambda b, h: (b, h)),
            pl.BlockSpec((seq, hps * LANES), lambda b, h: (b, h)),
            pl.BlockSpec((seq, LANES), lambda b, h: (b, SLAB_CKR)),
            pl.BlockSpec((seq, hps * LANES), lambda b, h: (b, h)),
            *[gate(r) for r in range(hps)],
        ],
        out_specs=pl.BlockSpec((seq, hps * LANES), lambda b, h: (b, h)),
        out_shape=jax.ShapeDtypeStruct((batch * seq, heads * LANES), jnp.bfloat16),
        scratch_shapes=_attn_scratch(seq, hps),
        compiler_params=_cparams(("parallel", "parallel")),
        name="mla_attn",
    )(qf, kn, proj, vc, *([proj] * hps))


def _mla_up_kernel(cq_ref, ckv_ref, qnw_ref, kvnw_ref, wuq_ref, wukv_ref,
                   cos_ref, sa_ref, sb_ref, qf_ref, kn_ref, vc_ref, wq_ref, wkv_ref):
    def norm(ref, w_ref):
        c = ref[...].astype(jnp.float32)
        ms = jnp.mean(c * c, axis=-1, keepdims=True)
        return (c * lax.rsqrt(ms + NORM_EPS) * w_ref[...]).astype(jnp.bfloat16)

    scale = 192 ** -0.5 * LOG2E
    @pl.when(pl.program_id(0) == 0)
    def _():
        wq = wuq_ref[...]
        zero = jnp.zeros((wq.shape[0], 64), wq.dtype)
        wq = jnp.concatenate([piece for h in range(4) for piece in (wq[:, 192 * h:192 * (h + 1)], zero)], axis=1)
        wq_ref[...] = wq.astype(wq_ref.dtype)
        wkv_ref[...] = wukv_ref[...].astype(wkv_ref.dtype)

    q = jnp.dot(norm(cq_ref, qnw_ref), wq_ref[...], preferred_element_type=jnp.float32)
    kv = jnp.dot(norm(ckv_ref, kvnw_ref), wkv_ref[...], preferred_element_type=jnp.float32)
    for h in range(4):
        nope = q[:, 2 * h * LANES:(2 * h + 1) * LANES]
        rope = _rope(q[:, (2 * h + 1) * LANES:(2 * h + 2) * LANES], cos_ref[...], sa_ref[...], sb_ref[...])
        qf_ref[:, 2 * h * LANES:(2 * h + 1) * LANES] = (nope * scale).astype(qf_ref.dtype)
        qf_ref[:, (2 * h + 1) * LANES:(2 * h + 2) * LANES] = (rope * scale).astype(qf_ref.dtype)
        kn_ref[:, h * LANES:(h + 1) * LANES] = kv[:, 2 * h * LANES:(2 * h + 1) * LANES].astype(kn_ref.dtype)
        vc_ref[:, h * LANES:(h + 1) * LANES] = kv[:, (2 * h + 1) * LANES:(2 * h + 2) * LANES].astype(vc_ref.dtype)


def _mla_up(proj, qn_w, kvn_w, w_uq, w_ukv, layer, cos, sa, sb):
    m = proj.shape[0]
    tm = 1024
    tab = pl.BlockSpec((tm, LANES), lambda i: (i, 0))
    full = lambda shape: pl.BlockSpec(shape, lambda i: (0, 0))
    weight = lambda w: pl.BlockSpec((None,) + w.shape[1:], lambda i: (layer, 0, 0))
    return pl.pallas_call(
        _mla_up_kernel,
        grid=(m // tm,),
        in_specs=[
            pl.BlockSpec((tm, 4 * LANES), lambda i: (i, SLAB_CQ // 4)),
            pl.BlockSpec((tm, 4 * LANES), lambda i: (i, SLAB_CKV // 4)),
            full((1, 512)), full((1, 512)), weight(w_uq), weight(w_ukv),
            tab, tab, tab,
        ],
        out_specs=[
            pl.BlockSpec((tm, 1024), lambda i: (i, 0)),
            pl.BlockSpec((tm, 512), lambda i: (i, 0)),
            pl.BlockSpec((tm, 512), lambda i: (i, 0)),
        ],
        out_shape=[
            jax.ShapeDtypeStruct((m, 1024), jnp.bfloat16),
            jax.ShapeDtypeStruct((m, 512), jnp.bfloat16),
            jax.ShapeDtypeStruct((m, 512), jnp.bfloat16),
        ],
        scratch_shapes=[pltpu.VMEM((512, 1024), jnp.bfloat16), pltpu.VMEM((512, 1024), jnp.bfloat16)],
        compiler_params=_cparams(("arbitrary",)),
        name="mla_up",
    )(proj, proj, qn_w.reshape(1, 512), kvn_w.reshape(1, 512), w_uq, w_ukv, cos, sa, sb)


def _swa_kernel(sink_ref, q0_ref, q1_ref, kc_ref, kp_ref, vc_ref, vp_ref, gate_ref, o_ref, *, blocks):
    w = SW_WINDOW
    q_refs = (q0_ref, q1_ref)
    first = pl.program_id(1) == 0
    lane = lax.broadcasted_iota(jnp.int32, ((blocks + 1) * w, LANES), 1)
    lo = lane < 64

    kt = jnp.concatenate([kp_ref[...], kc_ref[...]], axis=0).astype(jnp.float32)
    kr = pltpu.roll(kt, 64, 1)
    k_halves = []
    for g in range(2):
        a, b = (kt, kr) if g == 0 else (kr, kt)
        k_halves.append((jnp.where(lo, a, 0.0).astype(jnp.bfloat16), jnp.where(lo, 0.0, b).astype(jnp.bfloat16)))

    vt = jnp.concatenate([vp_ref[...], vc_ref[...]], axis=0).astype(jnp.float32).T
    zeros = jnp.zeros((64, 2 * w), jnp.bfloat16)
    ones = jnp.ones((ONES_ROWS, 2 * w), jnp.bfloat16)
    no_ones = jnp.zeros((ONES_ROWS, 2 * w), jnp.bfloat16)

    key = lax.broadcasted_iota(jnp.int32, (2 * w, 4 * w), 0)
    qry = lax.broadcasted_iota(jnp.int32, (2 * w, 4 * w), 1) & (w - 1)
    rel = qry + w - key
    in_band = (rel >= 0) & (rel < w)
    first_key = jnp.where(first, w, 0)
    bias = jnp.where(in_band, 0.0, NEG_INF)
    bias_first = jnp.where(in_band & (key >= first_key), 0.0, NEG_INF)

    def scores(n, g):
        k_lo, k_hi = k_halves[g]
        rows = slice(n * w, (n + 2) * w)
        kk = jnp.concatenate([k_lo[rows], k_hi[rows]], axis=0)
        qs = jnp.concatenate(
            [q_refs[g][n * w:(n + 1) * w, s * LANES:(s + 1) * LANES] for s in range(4)], axis=0)
        st = lax.dot_general(kk, qs, _NT, preferred_element_type=jnp.float32)
        ps, sink_terms = [], []
        for e in range(2):
            blk = st[e * 2 * w:(e + 1) * 2 * w] + (bias_first if n == 0 else bias)
            sink = jnp.concatenate(
                [jnp.full((1, w), sink_ref[8 * g + 2 * s + e] * LOG2E, jnp.float32) for s in range(4)], axis=1)
            m = jnp.maximum(jnp.max(blk, axis=0, keepdims=True), sink)
            ps.append(jnp.exp2(blk - m).astype(jnp.bfloat16))
            sink_terms.append(jnp.exp2(sink - m))
        return jnp.concatenate(ps, axis=0), sink_terms

    def outputs(n, g, pt, sink_terms):
        vg = vt[64 * g:64 * (g + 1), n * w:(n + 2) * w].astype(jnp.bfloat16)
        lhs = jnp.concatenate([
            jnp.concatenate([vg, zeros], axis=1), jnp.concatenate([zeros, vg], axis=1),
            jnp.concatenate([ones, no_ones], axis=1), jnp.concatenate([no_ones, ones], axis=1)], axis=0)
        acc = jnp.dot(lhs, pt, preferred_element_type=jnp.float32)
        den_even = acc[128:129] + sink_terms[0]
        den_odd = acc[128 + ONES_ROWS:129 + ONES_ROWS] + sink_terms[1]
        ot = jnp.concatenate([acc[0:64] * (1.0 / den_even), acc[64:128] * (1.0 / den_odd)], axis=0)
        for s in range(4):
            slab = 4 * g + s
            gate = gate_ref[n * w:(n + 1) * w, slab * LANES:(slab + 1) * LANES]
            o = ot[:, s * w:(s + 1) * w].T
            o_ref[n * w:(n + 1) * w, slab * LANES:(slab + 1) * LANES] = (
                o * gate.astype(jnp.float32)).astype(o_ref.dtype)

    pending = None
    for item in [(n, g) for n in range(blocks) for g in range(2)] + [None]:
        computed = scores(*item) if item is not None else None
        if pending is not None:
            outputs(*pending[0], *pending[1])
        pending = (item, computed)


def _swa(proj, sinks, batch, seq):
    w = SW_WINDOW
    blocks = 16
    rows = blocks * w
    steps = seq // rows
    nb = seq // w
    cur = lambda slab: pl.BlockSpec((rows, LANES), lambda b, i: (b * steps + i, slab))
    prev = lambda slab: pl.BlockSpec(
        (w, LANES), lambda b, i: (jnp.maximum(b * nb + i * blocks - 1, 0), slab))
    wide = lambda width, slab: pl.BlockSpec((rows, width * LANES), lambda b, i: (b * steps + i, slab // width))
    return pl.pallas_call(
        functools.partial(_swa_kernel, blocks=blocks),
        grid=(batch, steps),
        in_specs=[
            pl.BlockSpec(memory_space=pltpu.SMEM),
            wide(4, SLAB_BQ_LO), wide(4, SLAB_BQ_HI),
            cur(SLAB_BK), prev(SLAB_BK), cur(SLAB_BV), prev(SLAB_BV),
            wide(8, SLAB_BG),
        ],
        out_specs=pl.BlockSpec((rows, 1024), lambda b, i: (b * steps + i, 0)),
        out_shape=jax.ShapeDtypeStruct((batch * seq, 1024), jnp.bfloat16),
        compiler_params=_cparams(("parallel", "arbitrary")),
        name="swa",
    )(sinks.astype(jnp.float32), proj, proj, proj, proj, proj, proj, proj)


OUT_SUB_ROWS = 256


def _out_proj_kernel(ya_ref, yb_ref, yc_ref, wf_ref, x_ref, postw_ref, o_ref, w_ref):
    @pl.when(pl.program_id(0) == 0)
    def _():
        w_ref[...] = wf_ref[...].astype(w_ref.dtype)

    for r in range(0, o_ref.shape[0], OUT_SUB_ROWS):
        rows = slice(r, r + OUT_SUB_ROWS)
        out = jnp.dot(ya_ref[rows, :], w_ref[0:512, :], preferred_element_type=jnp.float32)
        out += jnp.dot(yb_ref[rows, :], w_ref[512:1536, :], preferred_element_type=jnp.float32)
        out += jnp.dot(yc_ref[rows, :], w_ref[1536:2048, :], preferred_element_type=jnp.float32)
        ms = jnp.mean(out * out, axis=-1, keepdims=True)
        o_ref[rows, :] = x_ref[rows, :] + out * lax.rsqrt(ms + NORM_EPS) * postw_ref[...]


def _out_proj(ya, yb, yc, w_out, layer, x2, post_w):
    m = x2.shape[0]
    tm = 512
    row = lambda width: pl.BlockSpec((tm, width), lambda i: (i, 0))
    return pl.pallas_call(
        _out_proj_kernel,
        grid=(m // tm,),
        in_specs=[
            row(512), row(1024), row(512),
            pl.BlockSpec((None, D_MODEL, D_MODEL), lambda i: (layer, 0, 0), pipeline_mode=pl.Buffered(1)),
            row(D_MODEL),
            pl.BlockSpec((1, D_MODEL), lambda i: (0, 0)),
        ],
        out_specs=row(D_MODEL),
        out_shape=jax.ShapeDtypeStruct((m, D_MODEL), jnp.float32),
        scratch_shapes=[pltpu.VMEM((D_MODEL, D_MODEL), jnp.bfloat16)],
        compiler_params=_cparams(("arbitrary",)),
        name="out_proj",
    )(ya, yb, yc, w_out, x2, post_w.reshape(1, D_MODEL))


def kernel(x, positions, pre_norm_w, post_norm_w, w_in, diff_lambda_q1, diff_lambda_k1, diff_lambda_q2,
           diff_lambda_k2, diff_subln_w, sink_logits, mla_q_norm_w, mla_kv_norm_w, w_uq, w_ukv, w_out):
    batch, seq, d = x.shape
    depth = w_in.shape[0]
    cos, sa, sb = _rope_tables(positions)
    x2 = x.reshape(batch * seq, d)
    for layer in range(depth):
        lam_init = 0.8 - 0.6 * math.exp(-0.3 * layer)
        lam_rows = jnp.stack([diff_lambda_q1[layer], diff_lambda_k1[layer],
                              diff_lambda_q2[layer], diff_lambda_k2[layer]]).astype(jnp.float32)
        proj = _in_proj(x2, pre_norm_w[layer], w_in, layer, cos, sa, sb)
        ya = _diff_attn(proj, lam_rows, diff_subln_w[layer], batch, seq, lam_init)
        yb = _swa(proj, sink_logits[layer], batch, seq)
        qf, kn, vc = _mla_up(proj, mla_q_norm_w[layer], mla_kv_norm_w[layer],
                             w_uq, w_ukv, layer, cos, sa, sb)
        yc = _mla_attn(qf, kn, vc, proj, batch, seq)
        x2 = _out_proj(ya, yb, yc, w_out, layer, x2, post_norm_w[layer])
    return x2.reshape(batch, seq, d)
```

```python
import functools
import math

import jax
import jax.numpy as jnp
from jax import lax
from jax.experimental import pallas as pl
from jax.experimental.pallas import tpu as pltpu

D_MODEL = 2048
ROPE_THETA = 10000.0
NORM_EPS = 1e-6
NEG_INF = -1e30
LOG2E = math.log2(math.e)
DA_SUBLN_EPS = 1e-5
SW_WINDOW = 128
LANES = 128

IN_TILES = (
    ((0, 1, 2, 3, 8, 9, 10, 11), ("rope_q",) * 4 + ("plain",) * 4),
    ((4, 5, 6, 7, 34, 35, 36, 37), ("rope",) * 4 + ("plain",) * 4),
    ((16, 17, 18, 19, 38, 39, 40, 41), ("rope_q",) * 4 + ("plain",) * 4),
    ((20, 21, 22, 23, 12, 13, 14, 15), ("rope_q",) * 4 + ("silu",) * 4),
    ((26, 27, 28, 29, 30, 31, 32, 33), ("silu",) * 8),
    ((42, 43, 44, 45, 46, 24, 25, 25), ("kr",) + ("gate_c",) * 4 + ("rope", "plain", "skip")),
)
IN_TILE = 8 * LANES
N_SLABS = 8 * len(IN_TILES)
SLAB_AQ, SLAB_AV, SLAB_AK, SLAB_CQ = 0, 4, 8, 12
SLAB_BQ_LO, SLAB_CKV, SLAB_BQ_HI, SLAB_AG = 16, 20, 24, 28
SLAB_BG, SLAB_CKR, SLAB_CG, SLAB_BK, SLAB_BV = 32, 40, 41, 45, 46

VMEM_LIMIT = 56 * 1024 * 1024

_NT = (((1,), (1,)), ((), ()))


def _cparams(sem):
    return pltpu.CompilerParams(dimension_semantics=sem, vmem_limit_bytes=VMEM_LIMIT)


def _rope_table_kernel(pos_ref, freq_ref, cos_ref, sin_ref):
    ang = pos_ref[...].astype(jnp.float32) * freq_ref[...]
    c, s = jnp.cos(ang), jnp.sin(ang)
    quarter = ang.shape[0]
    lane = lax.broadcasted_iota(jnp.int32, ang.shape, 1)
    first_half = (lane % 64) < 32
    for r in range(4):
        rows = slice(r * quarter, (r + 1) * quarter)
        cr = jnp.tile(c[:, 32 * r:32 * (r + 1)], (1, 4))
        sr = jnp.tile(s[:, 32 * r:32 * (r + 1)], (1, 4))
        cos_ref[rows, :] = cr
        sin_ref[rows, :] = jnp.where(first_half, -sr, sr)


def _rope_tables(positions):
    m = positions.size
    tm = 1024
    pos_p = jnp.repeat(positions.reshape(m // tm, 4, tm // 4).transpose(0, 2, 1), 32, axis=2).reshape(m // 4, LANES)
    inv_freq = jnp.power(ROPE_THETA, -jnp.arange(0, 64, 2, dtype=jnp.float32) / 64)
    freq = jnp.tile(inv_freq, 4).reshape(1, LANES)
    spec = pl.BlockSpec((tm, LANES), lambda i: (i, 0))
    return pl.pallas_call(
        _rope_table_kernel,
        grid=(m // tm,),
        in_specs=[pl.BlockSpec((tm // 4, LANES), lambda i: (i, 0)), pl.BlockSpec((1, LANES), lambda i: (0, 0))],
        out_specs=[spec, spec],
        out_shape=[jax.ShapeDtypeStruct((m, LANES), jnp.float32)] * 2,
        compiler_params=_cparams(("parallel",)),
        name="rope_tables",
    )(pos_p, freq)


def _rope(a, cos, sin_signed):
    lane = lax.broadcasted_iota(jnp.int32, a.shape, 1)
    partner = jnp.where((lane % 64) < 32, pltpu.roll(a, 96, 1), pltpu.roll(a, 32, 1))
    return a * cos + partner * sin_signed


def _silu(a):
    return a * (0.5 + 0.5 * jnp.tanh(0.5 * a))


def _in_proj_kernel(x_ref, prew_ref, *rest):
    w_refs, (cos_ref, sin_ref, o_ref, h_ref) = rest[:8], rest[8:]
    j, r = pl.program_id(1), pl.program_id(2)
    tm = x_ref.shape[0]

    @pl.when(j == 0)
    def _():
        x = x_ref[...]
        ms = jnp.mean(x * x, axis=-1, keepdims=True)
        h_ref[r] = (x * lax.rsqrt(ms + NORM_EPS) * prew_ref[...]).astype(jnp.bfloat16)

    def epilogue(kinds):
        w = jnp.concatenate([w_ref[...] for w_ref in w_refs], axis=0).astype(jnp.bfloat16)
        acc = lax.dot_general(h_ref[r], w, _NT, preferred_element_type=jnp.float32)
        slab = lambda s: acc[:, s * LANES:(s + 1) * LANES]
        lane = lax.broadcasted_iota(jnp.int32, (acc.shape[0], LANES), 1)
        rows = pl.ds(pl.multiple_of(r * tm, tm), tm)
        for s, kind in enumerate(kinds):
            a = slab(s)
            if kind == "skip":
                a = jnp.zeros_like(a)
            if kind in ("rope", "rope_q", "kr"):
                a = _rope(a, cos_ref[rows, :], sin_ref[rows, :])
                if kind == "rope_q":
                    a = a * (0.125 * LOG2E)
            elif kind == "silu":
                a = _silu(a)
            elif kind == "gate_c":
                a = jnp.where(lane < 64, pltpu.roll(_silu(slab(s - 1)), 64, 1), pltpu.roll(_silu(a), 64, 1))
            o_ref[:, s * LANES:(s + 1) * LANES] = a.astype(o_ref.dtype)

    for t, (_, kinds) in enumerate(IN_TILES):
        pl.when(j == t)(functools.partial(epilogue, kinds))


def _in_proj(x2, pre_w, w_in, layer, cos, sin):
    m = x2.shape[0]
    tm = 1024
    tab = pl.BlockSpec((2 * tm, LANES), lambda p, j, r: (p, 0))

    def slab_spec(position):
        sources = [slabs[position] for slabs, _ in IN_TILES]

        def index_map(p, j, r):
            source = sources[-1]
            for t in range(len(sources) - 2, -1, -1):
                source = jnp.where(j == t, sources[t], source)
            return layer, source, 0
        return pl.BlockSpec((None, LANES, D_MODEL), index_map)

    wt = jnp.swapaxes(w_in, 1, 2)
    return pl.pallas_call(
        _in_proj_kernel,
        grid=(m // (2 * tm), len(IN_TILES), 2),
        in_specs=[
            pl.BlockSpec((tm, D_MODEL), lambda p, j, r: (2 * p + jnp.where(j == 0, r, 1), 0)),
            pl.BlockSpec((1, D_MODEL), lambda p, j, r: (0, 0)),
            *[slab_spec(position) for position in range(8)],
            tab, tab,
        ],
        out_specs=pl.BlockSpec((tm, IN_TILE), lambda p, j, r: (2 * p + r, j)),
        out_shape=jax.ShapeDtypeStruct((m, N_SLABS * LANES), jnp.bfloat16),
        scratch_shapes=[pltpu.VMEM((2, tm, D_MODEL), jnp.bfloat16)],
        compiler_params=_cparams(("parallel", "arbitrary", "arbitrary")),
        name="in_proj",
    )(x2, pre_w.reshape(1, D_MODEL), *([wt] * 8), cos, sin)


ATTN_TQ = 256


ONES_ROWS = 16


def _transpose_v(v_ref, vt_ref):
    for h in range(vt_ref.shape[0]):
        vt_ref[h, 0:LANES, :] = v_ref[:, h * LANES:(h + 1) * LANES].astype(jnp.float32).T.astype(vt_ref.dtype)
        vt_ref[h, LANES:, :] = jnp.ones((ONES_ROWS, vt_ref.shape[2]), vt_ref.dtype)


ATTN_STREAMS = 4
DIFF_HEADS_PER_STEP = 2
MLA_HEADS_PER_STEP = 2


def _tile_streams(n_tiles):
    assert n_tiles == 2 * ATTN_STREAMS
    return [[n_tiles - 1 - r, r] for r in range(ATTN_STREAMS)]


def _causal_attention(streams, tq, emit):
    key = lax.broadcasted_iota(jnp.int32, (tq, tq), 0)
    qry = lax.broadcasted_iota(jnp.int32, (tq, tq), 1)
    on_or_below_diag = key <= qry
    flat = [[(item, c) for item in stream for c in range(item[3] + 1)] for stream in streams]
    q_t = {}

    def scores(p, k):
        if k >= len(flat[p]):
            return None
        (load_q, load_k, _, i, tag), c = flat[p][k]
        if c == 0:
            q_t[tag] = load_q().astype(jnp.float32).T.astype(jnp.bfloat16)
        s = jnp.dot(load_k(c), q_t[tag], preferred_element_type=jnp.float32)
        if c == i:
            s = jnp.where(on_or_below_diag, s, NEG_INF)
        return s.reshape(tq // 8, 8, tq)

    m_run = [None] * len(streams)
    acc = [None] * len(streams)
    s_next = [scores(p, 0) for p in range(len(streams))]
    for k in range(max(len(f) for f in flat)):
        s_cur, s_next = s_next, [scores(p, k + 1) for p in range(len(streams))]
        for p in range(len(streams)):
            if s_cur[p] is None:
                continue
            (_, _, load_vt, i, tag), c = flat[p][k]
            m = jnp.max(jnp.max(s_cur[p], axis=0), axis=0, keepdims=True)
            if c > 0:
                m = jnp.maximum(m_run[p], m)
            pt = jnp.exp2(s_cur[p] - jnp.broadcast_to(m, (8, tq))[None]).reshape(tq, tq).astype(jnp.bfloat16)
            d = jnp.dot(load_vt(c), pt, preferred_element_type=jnp.float32)
            acc[p] = d if c == 0 else acc[p] * jnp.exp2(m_run[p] - m) + d
            m_run[p] = m
            if c == i:
                emit(tag, acc[p])


def _attn_scratch(seq, heads_per_step):
    return [pltpu.VMEM((heads_per_step, LANES + ONES_ROWS, seq), jnp.bfloat16)]


def _diff_attn_kernel(q_ref, k_ref, v_ref, g_ref, lam_ref, subw_ref, o_ref, vt_ref, *, tq, lam_init):
    _transpose_v(v_ref, vt_ref)
    dv = LANES
    t = lam_ref[...]
    lam = (jnp.exp(jnp.sum(t[0:1] * t[1:2], axis=-1, keepdims=True))
           - jnp.exp(jnp.sum(t[2:3] * t[3:4], axis=-1, keepdims=True)) + lam_init)
    lane = lax.broadcasted_iota(jnp.int32, (tq, LANES), 1)

    def item(h, i, second):
        cols = slice(h * LANES, (h + 1) * LANES)

        def load_q():
            q = q_ref[i * tq:(i + 1) * tq, cols]
            keep = (lane >= 64) if second else (lane < 64)
            return jnp.where(keep, q, jnp.zeros_like(q))
        return (load_q, lambda c: k_ref[c * tq:(c + 1) * tq, cols], lambda c: vt_ref[h, :, c * tq:(c + 1) * tq],
                i, (h, i, second))

    streams = [[item(h, i, second) for i in tiles for second in (False, True)]
               for h in range(DIFF_HEADS_PER_STEP) for tiles in _tile_streams(q_ref.shape[0] // tq)]
    first = {}

    def emit(tag, acc):
        h, i, second = tag
        o_n = acc[0:dv] * (1.0 / acc[dv:dv + 1])
        if not second:
            first[h, i] = o_n
            return
        rows, cols = slice(i * tq, (i + 1) * tq), slice(h * LANES, (h + 1) * LANES)
        o = first.pop((h, i)) - lam * o_n
        ms = jnp.mean(o * o, axis=0, keepdims=True)
        o = (o * lax.rsqrt(ms + DA_SUBLN_EPS)).T
        o = o * subw_ref[...] * (1.0 - lam_init)
        o_ref[rows, cols] = (o * g_ref[rows, cols].astype(jnp.float32)).astype(o_ref.dtype)

    _causal_attention(streams, tq, emit)


def _diff_attn(proj, lam_rows, subln_w, batch, seq, lam_init):
    heads, hps = 4, DIFF_HEADS_PER_STEP
    slab = lambda first: pl.BlockSpec((seq, hps * LANES), lambda b, h: (b, first // hps + h))
    return pl.pallas_call(
        functools.partial(_diff_attn_kernel, tq=ATTN_TQ, lam_init=lam_init),
        grid=(batch, heads // hps),
        in_specs=[
            slab(SLAB_AQ), slab(SLAB_AK), slab(SLAB_AV), slab(SLAB_AG),
            pl.BlockSpec((4, 64), lambda b, h: (0, 0)),
            pl.BlockSpec((1, LANES), lambda b, h: (0, 0)),
        ],
        out_specs=pl.BlockSpec((seq, hps * LANES), lambda b, h: (b, h)),
        out_shape=jax.ShapeDtypeStruct((batch * seq, heads * LANES), jnp.bfloat16),
        scratch_shapes=_attn_scratch(seq, hps),
        compiler_params=_cparams(("parallel", "parallel")),
        name="diff_attn",
    )(proj, proj, proj, proj, lam_rows, subln_w.reshape(1, LANES))


def _mla_attn_kernel(q_ref, kn_ref, kr_ref, v_ref, *rest, tq):
    g_refs, (o_ref, vt_ref) = rest[:MLA_HEADS_PER_STEP], rest[MLA_HEADS_PER_STEP:]
    _transpose_v(v_ref, vt_ref)
    dv = LANES

    def item(h, i):
        cols = slice(h * 2 * LANES, (h + 1) * 2 * LANES)

        def load_k(c):
            rows = slice(c * tq, (c + 1) * tq)
            return jnp.concatenate([kn_ref[rows, h * LANES:(h + 1) * LANES], kr_ref[rows, :]], axis=1)
        return (lambda: q_ref[i * tq:(i + 1) * tq, cols], load_k,
                lambda c: vt_ref[h, :, c * tq:(c + 1) * tq], i, (h, i))

    streams = [[item(h, i) for i in tiles]
               for h in range(MLA_HEADS_PER_STEP) for tiles in _tile_streams(q_ref.shape[0] // tq)]

    def emit(tag, acc):
        h, i = tag
        rows = slice(i * tq, (i + 1) * tq)
        o = (acc[0:dv] * (1.0 / acc[dv:dv + 1])).T
        o_ref[rows, h * LANES:(h + 1) * LANES] = (o * g_refs[h][rows, :].astype(jnp.float32)).astype(o_ref.dtype)

    _causal_attention(streams, tq, emit)


def _mla_attn(qf, kn, vc, proj, batch, seq):
    heads, hps = 4, MLA_HEADS_PER_STEP
    gate = lambda r: pl.BlockSpec((seq, LANES), lambda b, h: (b, SLAB_CG + hps * h + r))
    return pl.pallas_call(
        functools.partial(_mla_attn_kernel, tq=ATTN_TQ),
        grid=(batch, heads // hps),
        in_specs=[
            pl.BlockSpec((seq, hps * 2 * LANES), lambda b, h: (b, h)),
            pl.BlockSpec((seq, hps * LANES), lambda b, h: (b, h)),
            pl.BlockSpec((seq, LANES), lambda b, h: (b, SLAB_CKR)),
            pl.BlockSpec((seq, hps * LANES), lambda b, h: (b, h)),
            *[gate(r) for r in range(hps)],
        ],
        out_specs=pl.BlockSpec((seq, hps * LANES), lambda b, h: (b, h)),
        out_shape=jax.ShapeDtypeStruct((batch * seq, heads * LANES), jnp.bfloat16),
        scratch_shapes=_attn_scratch(seq, hps),
        compiler_params=_cparams(("parallel", "parallel")),
        name="mla_attn",
    )(qf, kn, proj, vc, *([proj] * hps))


def _mla_up_kernel(cq_ref, ckv_ref, qnw_ref, kvnw_ref, wuq_ref, wukv_ref,
                   cos_ref, sin_ref, qf_ref, kn_ref, vc_ref, wq_ref, wkv_ref):
    def norm(ref, w_ref):
        c = ref[...].astype(jnp.float32)
        ms = jnp.mean(c * c, axis=-1, keepdims=True)
        return (c * lax.rsqrt(ms + NORM_EPS) * w_ref[...]).astype(jnp.bfloat16)

    scale = 192 ** -0.5 * LOG2E
    @pl.when(pl.program_id(0) == 0)
    def _():
        wq = wuq_ref[...]
        zero = jnp.zeros((wq.shape[0], 64), wq.dtype)
        wq = jnp.concatenate([piece for h in range(4) for piece in (wq[:, 192 * h:192 * (h + 1)], zero)], axis=1)
        wq_ref[...] = wq.astype(wq_ref.dtype)
        wkv_ref[...] = wukv_ref[...].astype(wkv_ref.dtype)

    q = jnp.dot(norm(cq_ref, qnw_ref), wq_ref[...], preferred_element_type=jnp.float32)
    kv = jnp.dot(norm(ckv_ref, kvnw_ref), wkv_ref[...], preferred_element_type=jnp.float32)
    for h in range(4):
        nope = q[:, 2 * h * LANES:(2 * h + 1) * LANES]
        rope = _rope(q[:, (2 * h + 1) * LANES:(2 * h + 2) * LANES], cos_ref[...], sin_ref[...])
        qf_ref[:, 2 * h * LANES:(2 * h + 1) * LANES] = (nope * scale).astype(qf_ref.dtype)
        qf_ref[:, (2 * h + 1) * LANES:(2 * h + 2) * LANES] = (rope * scale).astype(qf_ref.dtype)
        kn_ref[:, h * LANES:(h + 1) * LANES] = kv[:, 2 * h * LANES:(2 * h + 1) * LANES].astype(kn_ref.dtype)
        vc_ref[:, h * LANES:(h + 1) * LANES] = kv[:, (2 * h + 1) * LANES:(2 * h + 2) * LANES].astype(vc_ref.dtype)


def _mla_up(proj, qn_w, kvn_w, w_uq, w_ukv, layer, cos, sin):
    m = proj.shape[0]
    tm = 1024
    tab = pl.BlockSpec((tm, LANES), lambda i: (i, 0))
    full = lambda shape: pl.BlockSpec(shape, lambda i: (0, 0))
    weight = lambda w: pl.BlockSpec((None,) + w.shape[1:], lambda i: (layer, 0, 0))
    return pl.pallas_call(
        _mla_up_kernel,
        grid=(m // tm,),
        in_specs=[
            pl.BlockSpec((tm, 4 * LANES), lambda i: (i, SLAB_CQ // 4)),
            pl.BlockSpec((tm, 4 * LANES), lambda i: (i, SLAB_CKV // 4)),
            full((1, 512)), full((1, 512)), weight(w_uq), weight(w_ukv),
            tab, tab,
        ],
        out_specs=[
            pl.BlockSpec((tm, 1024), lambda i: (i, 0)),
            pl.BlockSpec((tm, 512), lambda i: (i, 0)),
            pl.BlockSpec((tm, 512), lambda i: (i, 0)),
        ],
        out_shape=[
            jax.ShapeDtypeStruct((m, 1024), jnp.bfloat16),
            jax.ShapeDtypeStruct((m, 512), jnp.bfloat16),
            jax.ShapeDtypeStruct((m, 512), jnp.bfloat16),
        ],
        scratch_shapes=[pltpu.VMEM((512, 1024), jnp.bfloat16), pltpu.VMEM((512, 1024), jnp.bfloat16)],
        compiler_params=_cparams(("arbitrary",)),
        name="mla_up",
    )(proj, proj, qn_w.reshape(1, 512), kvn_w.reshape(1, 512), w_uq, w_ukv, cos, sin)


def _swa_kernel(sink_ref, q0_ref, q1_ref, kc_ref, kp_ref, vc_ref, vp_ref, gate_ref, o_ref, *, blocks):
    w = SW_WINDOW
    q_refs = (q0_ref, q1_ref)
    first = pl.program_id(1) == 0
    lane = lax.broadcasted_iota(jnp.int32, ((blocks + 1) * w, LANES), 1)
    lo = lane < 64

    kt = jnp.concatenate([kp_ref[...], kc_ref[...]], axis=0).astype(jnp.float32)
    kr = pltpu.roll(kt, 64, 1)
    k_halves = []
    for g in range(2):
        a, b = (kt, kr) if g == 0 else (kr, kt)
        k_halves.append((jnp.where(lo, a, 0.0).astype(jnp.bfloat16), jnp.where(lo, 0.0, b).astype(jnp.bfloat16)))

    vt = jnp.concatenate([vp_ref[...], vc_ref[...]], axis=0).astype(jnp.float32).T
    zeros = jnp.zeros((64, 2 * w), jnp.bfloat16)
    ones = jnp.ones((ONES_ROWS, 2 * w), jnp.bfloat16)
    no_ones = jnp.zeros((ONES_ROWS, 2 * w), jnp.bfloat16)

    key = lax.broadcasted_iota(jnp.int32, (2 * w, 4 * w), 0)
    qry = lax.broadcasted_iota(jnp.int32, (2 * w, 4 * w), 1) & (w - 1)
    rel = qry + w - key
    in_band = (rel >= 0) & (rel < w)
    first_key = jnp.where(first, w, 0)
    bias = jnp.where(in_band, 0.0, NEG_INF)
    bias_first = jnp.where(in_band & (key >= first_key), 0.0, NEG_INF)

    def scores(n, g):
        k_lo, k_hi = k_halves[g]
        rows = slice(n * w, (n + 2) * w)
        kk = jnp.concatenate([k_lo[rows], k_hi[rows]], axis=0)
        qs = jnp.concatenate(
            [q_refs[g][n * w:(n + 1) * w, s * LANES:(s + 1) * LANES] for s in range(4)], axis=0)
        st = lax.dot_general(kk, qs, _NT, preferred_element_type=jnp.float32)
        ps, sink_terms = [], []
        for e in range(2):
            blk = st[e * 2 * w:(e + 1) * 2 * w] + (bias_first if n == 0 else bias)
            sink = jnp.concatenate(
                [jnp.full((1, w), sink_ref[8 * g + 2 * s + e] * LOG2E, jnp.float32) for s in range(4)], axis=1)
            m = jnp.maximum(jnp.max(blk, axis=0, keepdims=True), sink)
            ps.append(jnp.exp2(blk - m).astype(jnp.bfloat16))
            sink_terms.append(jnp.exp2(sink - m))
        return jnp.concatenate(ps, axis=0), sink_terms

    def outputs(n, g, pt, sink_terms):
        vg = vt[64 * g:64 * (g + 1), n * w:(n + 2) * w].astype(jnp.bfloat16)
        lhs = jnp.concatenate([
            jnp.concatenate([vg, zeros], axis=1), jnp.concatenate([zeros, vg], axis=1),
            jnp.concatenate([ones, no_ones], axis=1), jnp.concatenate([no_ones, ones], axis=1)], axis=0)
        acc = jnp.dot(lhs, pt, preferred_element_type=jnp.float32)
        den_even = acc[128:129] + sink_terms[0]
        den_odd = acc[128 + ONES_ROWS:129 + ONES_ROWS] + sink_terms[1]
        ot = jnp.concatenate([acc[0:64] * (1.0 / den_even), acc[64:128] * (1.0 / den_odd)], axis=0)
        for s in range(4):
            slab = 4 * g + s
            gate = gate_ref[n * w:(n + 1) * w, slab * LANES:(slab + 1) * LANES]
            o = ot[:, s * w:(s + 1) * w].T
            o_ref[n * w:(n + 1) * w, slab * LANES:(slab + 1) * LANES] = (
                o * gate.astype(jnp.float32)).astype(o_ref.dtype)

    pending = None
    for item in [(n, g) for n in range(blocks) for g in range(2)] + [None]:
        computed = scores(*item) if item is not None else None
        if pending is not None:
            outputs(*pending[0], *pending[1])
        pending = (item, computed)


def _swa(proj, sinks, batch, seq):
    w = SW_WINDOW
    blocks = 16
    rows = blocks * w
    steps = seq // rows
    nb = seq // w
    cur = lambda slab: pl.BlockSpec((rows, LANES), lambda b, i: (b * steps + i, slab))
    prev = lambda slab: pl.BlockSpec(
        (w, LANES), lambda b, i: (jnp.maximum(b * nb + i * blocks - 1, 0), slab))
    wide = lambda width, slab: pl.BlockSpec((rows, width * LANES), lambda b, i: (b * steps + i, slab // width))
    return pl.pallas_call(
        functools.partial(_swa_kernel, blocks=blocks),
        grid=(batch, steps),
        in_specs=[
            pl.BlockSpec(memory_space=pltpu.SMEM),
            wide(4, SLAB_BQ_LO), wide(4, SLAB_BQ_HI),
            cur(SLAB_BK), prev(SLAB_BK), cur(SLAB_BV), prev(SLAB_BV),
            wide(8, SLAB_BG),
        ],
        out_specs=pl.BlockSpec((rows, 1024), lambda b, i: (b * steps + i, 0)),
        out_shape=jax.ShapeDtypeStruct((batch * seq, 1024), jnp.bfloat16),
        compiler_params=_cparams(("parallel", "arbitrary")),
        name="swa",
    )(sinks.astype(jnp.float32), proj, proj, proj, proj, proj, proj, proj)


OUT_SUB_ROWS = 256


def _out_proj_kernel(ya_ref, yb_ref, yc_ref, wf_ref, x_ref, postw_ref, o_ref, w_ref):
    @pl.when(pl.program_id(0) == 0)
    def _():
        w_ref[...] = wf_ref[...].astype(w_ref.dtype)

    for r in range(0, o_ref.shape[0], OUT_SUB_ROWS):
        rows = slice(r, r + OUT_SUB_ROWS)
        out = jnp.dot(ya_ref[rows, :], w_ref[0:512, :], preferred_element_type=jnp.float32)
        out += jnp.dot(yb_ref[rows, :], w_ref[512:1536, :], preferred_element_type=jnp.float32)
        out += jnp.dot(yc_ref[rows, :], w_ref[1536:2048, :], preferred_element_type=jnp.float32)
        ms = jnp.mean(out * out, axis=-1, keepdims=True)
        o_ref[rows, :] = x_ref[rows, :] + out * lax.rsqrt(ms + NORM_EPS) * postw_ref[...]


def _out_proj(ya, yb, yc, w_out, layer, x2, post_w):
    m = x2.shape[0]
    tm = 512
    row = lambda width: pl.BlockSpec((tm, width), lambda i: (i, 0))
    return pl.pallas_call(
        _out_proj_kernel,
        grid=(m // tm,),
        in_specs=[
            row(512), row(1024), row(512),
            pl.BlockSpec((None, D_MODEL, D_MODEL), lambda i: (layer, 0, 0), pipeline_mode=pl.Buffered(1)),
            row(D_MODEL),
            pl.BlockSpec((1, D_MODEL), lambda i: (0, 0)),
        ],
        out_specs=row(D_MODEL),
        out_shape=jax.ShapeDtypeStruct((m, D_MODEL), jnp.float32),
        scratch_shapes=[pltpu.VMEM((D_MODEL, D_MODEL), jnp.bfloat16)],
        compiler_params=_cparams(("arbitrary",)),
        name="out_proj",
    )(ya, yb, yc, w_out, x2, post_w.reshape(1, D_MODEL))


def kernel(x, positions, pre_norm_w, post_norm_w, w_in, diff_lambda_q1, diff_lambda_k1, diff_lambda_q2,
           diff_lambda_k2, diff_subln_w, sink_logits, mla_q_norm_w, mla_kv_norm_w, w_uq, w_ukv, w_out):
    batch, seq, d = x.shape
    depth = w_in.shape[0]
    cos, sin = _rope_tables(positions)
    x2 = x.reshape(batch * seq, d)
    for layer in range(depth):
        lam_init = 0.8 - 0.6 * math.exp(-0.3 * layer)
        lam_rows = jnp.stack([diff_lambda_q1[layer], diff_lambda_k1[layer],
                              diff_lambda_q2[layer], diff_lambda_k2[layer]]).astype(jnp.float32)
        proj = _in_proj(x2, pre_norm_w[layer], w_in, layer, cos, sin)
        ya = _diff_attn(proj, lam_rows, diff_subln_w[layer], batch, seq, lam_init)
        yb = _swa(proj, sink_logits[layer], batch, seq)
        qf, kn, vc = _mla_up(proj, mla_q_norm_w[layer], mla_kv_norm_w[layer],
                             w_uq, w_ukv, layer, cos, sin)
        yc = _mla_attn(qf, kn, vc, proj, batch, seq)
        x2 = _out_proj(ya, yb, yc, w_out, layer, x2, post_norm_w[layer])
    return x2.reshape(batch, seq, d)
```

```python
import functools
import math

import jax
import jax.numpy as jnp
from jax import lax
from jax.experimental import pallas as pl
from jax.experimental.pallas import tpu as pltpu

D_MODEL = 2048
ROPE_THETA = 10000.0
NORM_EPS = 1e-6
NEG_INF = -1e30
LOG2E = math.log2(math.e)
DA_SUBLN_EPS = 1e-5
SW_WINDOW = 128
LANES = 128

IN_TILES = (
    ((0, 1, 2, 3, 8, 9, 10, 11), ("rope_q",) * 4 + ("plain",) * 4),
    ((4, 5, 6, 7, 34, 35, 36, 37), ("rope",) * 4 + ("plain",) * 4),
    ((16, 17, 18, 19, 38, 39, 40, 41), ("rope_q",) * 4 + ("plain",) * 4),
    ((20, 21, 22, 23, 12, 13, 14, 15), ("rope_q",) * 4 + ("silu",) * 4),
    ((26, 27, 28, 29, 30, 31, 32, 33), ("silu",) * 8),
    ((42, 43, 44, 45, 46, 24, 25, 25), ("kr",) + ("gate_c",) * 4 + ("rope", "plain", "skip")),
)
IN_TILE = 8 * LANES
N_SLABS = 8 * len(IN_TILES)
SLAB_AQ, SLAB_AV, SLAB_AK, SLAB_CQ = 0, 4, 8, 12
SLAB_BQ_LO, SLAB_CKV, SLAB_BQ_HI, SLAB_AG = 16, 20, 24, 28
SLAB_BG, SLAB_CKR, SLAB_CG, SLAB_BK, SLAB_BV = 32, 40, 41, 45, 46

VMEM_LIMIT = 56 * 1024 * 1024

_NT = (((1,), (1,)), ((), ()))


def _cparams(sem):
    return pltpu.CompilerParams(dimension_semantics=sem, vmem_limit_bytes=VMEM_LIMIT)


def _rope_table_kernel(pos_ref, freq_ref, cos_ref, sin_ref):
    ang = pos_ref[...].astype(jnp.float32) * freq_ref[...]
    c, s = jnp.cos(ang), jnp.sin(ang)
    quarter = ang.shape[0]
    lane = lax.broadcasted_iota(jnp.int32, ang.shape, 1)
    first_half = (lane % 64) < 32
    for r in range(4):
        rows = slice(r * quarter, (r + 1) * quarter)
        cr = jnp.tile(c[:, 32 * r:32 * (r + 1)], (1, 4))
        sr = jnp.tile(s[:, 32 * r:32 * (r + 1)], (1, 4))
        cos_ref[rows, :] = cr
        sin_ref[rows, :] = jnp.where(first_half, -sr, sr)


def _rope_tables(positions):
    m = positions.size
    tm = 1024
    pos_p = jnp.repeat(positions.reshape(m // tm, 4, tm // 4).transpose(0, 2, 1), 32, axis=2).reshape(m // 4, LANES)
    inv_freq = jnp.power(ROPE_THETA, -jnp.arange(0, 64, 2, dtype=jnp.float32) / 64)
    freq = jnp.tile(inv_freq, 4).reshape(1, LANES)
    spec = pl.BlockSpec((tm, LANES), lambda i: (i, 0))
    return pl.pallas_call(
        _rope_table_kernel,
        grid=(m // tm,),
        in_specs=[pl.BlockSpec((tm // 4, LANES), lambda i: (i, 0)), pl.BlockSpec((1, LANES), lambda i: (0, 0))],
        out_specs=[spec, spec],
        out_shape=[jax.ShapeDtypeStruct((m, LANES), jnp.float32)] * 2,
        compiler_params=_cparams(("parallel",)),
        name="rope_tables",
    )(pos_p, freq)


def _rope(a, cos, sin_signed):
    lane = lax.broadcasted_iota(jnp.int32, a.shape, 1)
    partner = jnp.where((lane % 64) < 32, pltpu.roll(a, 96, 1), pltpu.roll(a, 32, 1))
    return a * cos + partner * sin_signed


def _silu(a):
    return a * (0.5 + 0.5 * jnp.tanh(0.5 * a))


def _in_proj_kernel(x_ref, prew_ref, *rest):
    w_refs, (cos_ref, sin_ref, o_ref, h_ref) = rest[:8], rest[8:]
    j, r = pl.program_id(1), pl.program_id(2)
    tm = x_ref.shape[0]

    @pl.when(j == 0)
    def _():
        x = x_ref[...]
        ms = jnp.mean(x * x, axis=-1, keepdims=True)
        h_ref[r] = (x * lax.rsqrt(ms + NORM_EPS) * prew_ref[...]).astype(jnp.bfloat16)

    def epilogue(kinds):
        w = jnp.concatenate([w_ref[...] for w_ref in w_refs], axis=0).astype(jnp.bfloat16)
        acc = lax.dot_general(h_ref[r], w, _NT, preferred_element_type=jnp.float32)
        slab = lambda s: acc[:, s * LANES:(s + 1) * LANES]
        lane = lax.broadcasted_iota(jnp.int32, (acc.shape[0], LANES), 1)
        rows = pl.ds(pl.multiple_of(r * tm, tm), tm)
        for s, kind in enumerate(kinds):
            a = slab(s)
            if kind == "skip":
                a = jnp.zeros_like(a)
            if kind in ("rope", "rope_q", "kr"):
                a = _rope(a, cos_ref[rows, :], sin_ref[rows, :])
                if kind == "rope_q":
                    a = a * (0.125 * LOG2E)
            elif kind == "silu":
                a = _silu(a)
            elif kind == "gate_c":
                a = jnp.where(lane < 64, pltpu.roll(_silu(slab(s - 1)), 64, 1), pltpu.roll(_silu(a), 64, 1))
            o_ref[:, s * LANES:(s + 1) * LANES] = a.astype(o_ref.dtype)

    for t, (_, kinds) in enumerate(IN_TILES):
        pl.when(j == t)(functools.partial(epilogue, kinds))


def _in_proj(x2, pre_w, w_in, layer, cos, sin):
    m = x2.shape[0]
    tm = 1024
    tab = pl.BlockSpec((2 * tm, LANES), lambda p, j, r: (p, 0))

    def slab_spec(position):
        sources = [slabs[position] for slabs, _ in IN_TILES]

        def index_map(p, j, r):
            source = sources[-1]
            for t in range(len(sources) - 2, -1, -1):
                source = jnp.where(j == t, sources[t], source)
            return layer, source, 0
        return pl.BlockSpec((None, LANES, D_MODEL), index_map)

    wt = jnp.swapaxes(w_in, 1, 2)
    return pl.pallas_call(
        _in_proj_kernel,
        grid=(m // (2 * tm), len(IN_TILES), 2),
        in_specs=[
            pl.BlockSpec((tm, D_MODEL), lambda p, j, r: (2 * p + jnp.where(j == 0, r, 1), 0)),
            pl.BlockSpec((1, D_MODEL), lambda p, j, r: (0, 0)),
            *[slab_spec(position) for position in range(8)],
            tab, tab,
        ],
        out_specs=pl.BlockSpec((tm, IN_TILE), lambda p, j, r: (2 * p + r, j)),
        out_shape=jax.ShapeDtypeStruct((m, N_SLABS * LANES), jnp.bfloat16),
        scratch_shapes=[pltpu.VMEM((2, tm, D_MODEL), jnp.bfloat16)],
        compiler_params=_cparams(("parallel", "arbitrary", "arbitrary")),
        name="in_proj",
    )(x2, pre_w.reshape(1, D_MODEL), *([wt] * 8), cos, sin)


ATTN_TQ = 256


ONES_ROWS = 16


def _transpose_v(v_ref, vt_ref):
    for h in range(vt_ref.shape[0]):
        vt_ref[h, 0:LANES, :] = v_ref[:, h * LANES:(h + 1) * LANES].astype(jnp.float32).T.astype(vt_ref.dtype)
        vt_ref[h, LANES:, :] = jnp.ones((ONES_ROWS, vt_ref.shape[2]), vt_ref.dtype)


ATTN_STREAMS = 4
DIFF_HEADS_PER_STEP = 2
MLA_HEADS_PER_STEP = 2


def _tile_streams(n_tiles):
    assert n_tiles == 2 * ATTN_STREAMS
    return [[n_tiles - 1 - r, r] for r in range(ATTN_STREAMS)]


def _causal_attention(streams, tq, emit):
    key = lax.broadcasted_iota(jnp.int32, (tq, tq), 0)
    qry = lax.broadcasted_iota(jnp.int32, (tq, tq), 1)
    on_or_below_diag = key <= qry
    flat = [[(item, c) for item in stream for c in range(item[3] + 1)] for stream in streams]
    q_t = {}

    def scores(p, k):
        if k >= len(flat[p]):
            return None
        (load_q, load_k, _, i, tag), c = flat[p][k]
        if c == 0:
            q_t[tag] = load_q().astype(jnp.float32).T.astype(jnp.bfloat16)
        s = jnp.dot(load_k(c), q_t[tag], preferred_element_type=jnp.float32)
        if c == i:
            s = jnp.where(on_or_below_diag, s, NEG_INF)
        return s.reshape(tq // 8, 8, tq)

    m_run = [None] * len(streams)
    acc = [None] * len(streams)
    s_next = [scores(p, 0) for p in range(len(streams))]
    for k in range(max(len(f) for f in flat)):
        s_cur, s_next = s_next, [scores(p, k + 1) for p in range(len(streams))]
        for p in range(len(streams)):
            if s_cur[p] is None:
                continue
            (_, _, load_vt, i, tag), c = flat[p][k]
            m = jnp.max(jnp.max(s_cur[p], axis=0), axis=0, keepdims=True)
            if c > 0:
                m = jnp.maximum(m_run[p], m)
            pt = jnp.exp2(s_cur[p] - jnp.broadcast_to(m, (8, tq))[None]).reshape(tq, tq).astype(jnp.bfloat16)
            d = jnp.dot(load_vt(c), pt, preferred_element_type=jnp.float32)
            acc[p] = d if c == 0 else acc[p] * jnp.exp2(m_run[p] - m) + d
            m_run[p] = m
            if c == i:
                emit(tag, acc[p])


def _attn_scratch(seq, heads_per_step):
    return [pltpu.VMEM((heads_per_step, LANES + ONES_ROWS, seq), jnp.bfloat16)]


def _diff_attn_kernel(q_ref, k_ref, v_ref, g_ref, lam_ref, subw_ref, o_ref, vt_ref, *, tq, lam_init):
    _transpose_v(v_ref, vt_ref)
    dv = LANES
    t = lam_ref[...]
    lam = (jnp.exp(jnp.sum(t[0:1] * t[1:2], axis=-1, keepdims=True))
           - jnp.exp(jnp.sum(t[2:3] * t[3:4], axis=-1, keepdims=True)) + lam_init)
    lane = lax.broadcasted_iota(jnp.int32, (tq, LANES), 1)

    def item(h, i, second):
        cols = slice(h * LANES, (h + 1) * LANES)

        def load_q():
            q = q_ref[i * tq:(i + 1) * tq, cols]
            keep = (lane >= 64) if second else (lane < 64)
            return jnp.where(keep, q, jnp.zeros_like(q))
        return (load_q, lambda c: k_ref[c * tq:(c + 1) * tq, cols], lambda c: vt_ref[h, :, c * tq:(c + 1) * tq],
                i, (h, i, second))

    streams = [[item(h, i, second) for i in tiles for second in (False, True)]
               for h in range(DIFF_HEADS_PER_STEP) for tiles in _tile_streams(q_ref.shape[0] // tq)]
    first = {}

    def emit(tag, acc):
        h, i, second = tag
        o_n = acc[0:dv] * (1.0 / acc[dv:dv + 1])
        if not second:
            first[h, i] = o_n
            return
        rows, cols = slice(i * tq, (i + 1) * tq), slice(h * LANES, (h + 1) * LANES)
        o = first.pop((h, i)) - lam * o_n
        ms = jnp.mean(o * o, axis=0, keepdims=True)
        o = (o * lax.rsqrt(ms + DA_SUBLN_EPS)).T
        o = o * subw_ref[...] * (1.0 - lam_init)
        o_ref[rows, cols] = (o * g_ref[rows, cols].astype(jnp.float32)).astype(o_ref.dtype)

    _causal_attention(streams, tq, emit)


def _diff_attn(proj, lam_rows, subln_w, batch, seq, lam_init):
    heads, hps = 4, DIFF_HEADS_PER_STEP
    slab = lambda first: pl.BlockSpec((seq, hps * LANES), lambda b, h: (b, first // hps + h))
    return pl.pallas_call(
        functools.partial(_diff_attn_kernel, tq=ATTN_TQ, lam_init=lam_init),
        grid=(batch, heads // hps),
        in_specs=[
            slab(SLAB_AQ), slab(SLAB_AK), slab(SLAB_AV), slab(SLAB_AG),
            pl.BlockSpec((4, 64), lambda b, h: (0, 0)),
            pl.BlockSpec((1, LANES), lambda b, h: (0, 0)),
        ],
        out_specs=pl.BlockSpec((seq, hps * LANES), lambda b, h: (b, h)),
        out_shape=jax.ShapeDtypeStruct((batch * seq, heads * LANES), jnp.bfloat16),
        scratch_shapes=_attn_scratch(seq, hps),
        compiler_params=_cparams(("parallel", "parallel")),
        name="diff_attn",
    )(proj, proj, proj, proj, lam_rows, subln_w.reshape(1, LANES))


def _mla_attn_kernel(q_ref, kn_ref, kr_ref, v_ref, *rest, tq):
    g_refs, (o_ref, vt_ref) = rest[:MLA_HEADS_PER_STEP], rest[MLA_HEADS_PER_STEP:]
    _transpose_v(v_ref, vt_ref)
    dv = LANES

    def item(h, i):
        cols = slice(h * 2 * LANES, (h + 1) * 2 * LANES)

        def load_k(c):
            rows = slice(c * tq, (c + 1) * tq)
            return jnp.concatenate([kn_ref[rows, h * LANES:(h + 1) * LANES], kr_ref[rows, :]], axis=1)
        return (lambda: q_ref[i * tq:(i + 1) * tq, cols], load_k,
                lambda c: vt_ref[h, :, c * tq:(c + 1) * tq], i, (h, i))

    streams = [[item(h, i) for i in tiles]
               for h in range(MLA_HEADS_PER_STEP) for tiles in _tile_streams(q_ref.shape[0] // tq)]

    def emit(tag, acc):
        h, i = tag
        rows = slice(i * tq, (i + 1) * tq)
        o = (acc[0:dv] * (1.0 / acc[dv:dv + 1])).T
        o_ref[rows, h * LANES:(h + 1) * LANES] = (o * g_refs[h][rows, :].astype(jnp.float32)).astype(o_ref.dtype)

    _causal_attention(streams, tq, emit)


def _mla_attn(qf, kn, vc, proj, batch, seq):
    heads, hps = 4, MLA_HEADS_PER_STEP
    gate = lambda r: pl.BlockSpec((seq, LANES), lambda b, h: (b, SLAB_CG + hps * h + r))
    return pl.pallas_call(
        functools.partial(_mla_attn_kernel, tq=ATTN_TQ),
        grid=(batch, heads // hps),
        in_specs=[
            pl.BlockSpec((seq, hps * 2 * LANES), lambda b, h: (b, h)),
            pl.BlockSpec((seq, hps * LANES), lambda b, h: (b, h)),
            pl.BlockSpec((seq, LANES), lambda b, h: (b, SLAB_CKR)),
            pl.BlockSpec((seq, hps * LANES), lambda b, h: (b, h)),
            *[gate(r) for r in range(hps)],
        ],
        out_specs=pl.BlockSpec((seq, hps * LANES), lambda b, h: (b, h)),
        out_shape=jax.ShapeDtypeStruct((batch * seq, heads * LANES), jnp.bfloat16),
        scratch_shapes=_attn_scratch(seq, hps),
        compiler_params=_cparams(("parallel", "parallel")),
        name="mla_attn",
    )(qf, kn, proj, vc, *([proj] * hps))


def _mla_up_kernel(cq_ref, ckv_ref, qnw_ref, kvnw_ref, wuq_ref, wukv_ref,
                   cos_ref, sin_ref, qf_ref, kn_ref, vc_ref, wq_ref, wkv_ref):
    def norm(ref, w_ref):
        c = ref[...].astype(jnp.float32)
        ms = jnp.mean(c * c, axis=-1, keepdims=True)
        return (c * lax.rsqrt(ms + NORM_EPS) * w_ref[...]).astype(jnp.bfloat16)

    scale = 192 ** -0.5 * LOG2E
    @pl.when(pl.program_id(0) == 0)
    def _():
        wq = wuq_ref[...]
        zero = jnp.zeros((wq.shape[0], 64), wq.dtype)
        wq = jnp.concatenate([piece for h in range(4) for piece in (wq[:, 192 * h:192 * (h + 1)], zero)], axis=1)
        wq_ref[...] = wq.astype(wq_ref.dtype)
        wkv_ref[...] = wukv_ref[...].astype(wkv_ref.dtype)

    q = jnp.dot(norm(cq_ref, qnw_ref), wq_ref[...], preferred_element_type=jnp.float32)
    kv = jnp.dot(norm(ckv_ref, kvnw_ref), wkv_ref[...], preferred_element_type=jnp.float32)
    for h in range(4):
        nope = q[:, 2 * h * LANES:(2 * h + 1) * LANES]
        rope = _rope(q[:, (2 * h + 1) * LANES:(2 * h + 2) * LANES], cos_ref[...], sin_ref[...])
        qf_ref[:, 2 * h * LANES:(2 * h + 1) * LANES] = (nope * scale).astype(qf_ref.dtype)
        qf_ref[:, (2 * h + 1) * LANES:(2 * h + 2) * LANES] = (rope * scale).astype(qf_ref.dtype)
        kn_ref[:, h * LANES:(h + 1) * LANES] = kv[:, 2 * h * LANES:(2 * h + 1) * LANES].astype(kn_ref.dtype)
        vc_ref[:, h * LANES:(h + 1) * LANES] = kv[:, (2 * h + 1) * LANES:(2 * h + 2) * LANES].astype(vc_ref.dtype)


def _mla_up(proj, qn_w, kvn_w, w_uq, w_ukv, layer, cos, sin):
    m = proj.shape[0]
    tm = 1024
    tab = pl.BlockSpec((tm, LANES), lambda i: (i, 0))
    full = lambda shape: pl.BlockSpec(shape, lambda i: (0, 0))
    weight = lambda w: pl.BlockSpec((None,) + w.shape[1:], lambda i: (layer, 0, 0))
    return pl.pallas_call(
        _mla_up_kernel,
        grid=(m // tm,),
        in_specs=[
            pl.BlockSpec((tm, 4 * LANES), lambda i: (i, SLAB_CQ // 4)),
            pl.BlockSpec((tm, 4 * LANES), lambda i: (i, SLAB_CKV // 4)),
            full((1, 512)), full((1, 512)), weight(w_uq), weight(w_ukv),
            tab, tab,
        ],
        out_specs=[
            pl.BlockSpec((tm, 1024), lambda i: (i, 0)),
            pl.BlockSpec((tm, 512), lambda i: (i, 0)),
            pl.BlockSpec((tm, 512), lambda i: (i, 0)),
        ],
        out_shape=[
            jax.ShapeDtypeStruct((m, 1024), jnp.bfloat16),
            jax.ShapeDtypeStruct((m, 512), jnp.bfloat16),
            jax.ShapeDtypeStruct((m, 512), jnp.bfloat16),
        ],
        scratch_shapes=[pltpu.VMEM((512, 1024), jnp.bfloat16), pltpu.VMEM((512, 1024), jnp.bfloat16)],
        compiler_params=_cparams(("arbitrary",)),
        name="mla_up",
    )(proj, proj, qn_w.reshape(1, 512), kvn_w.reshape(1, 512), w_uq, w_ukv, cos, sin)


def _swa_kernel(sink_ref, q0_ref, q1_ref, kc_ref, kp_ref, vc_ref, vp_ref, gate_ref, o_ref, *, blocks):
    w = SW_WINDOW
    q_refs = (q0_ref, q1_ref)
    first = pl.program_id(1) == 0
    lane = lax.broadcasted_iota(jnp.int32, ((blocks + 1) * w, LANES), 1)
    lo = lane < 64

    kt = jnp.concatenate([kp_ref[...], kc_ref[...]], axis=0).astype(jnp.float32)
    kr = pltpu.roll(kt, 64, 1)
    k_halves = []
    for g in range(2):
        a, b = (kt, kr) if g == 0 else (kr, kt)
        k_halves.append((jnp.where(lo, a, 0.0).astype(jnp.bfloat16), jnp.where(lo, 0.0, b).astype(jnp.bfloat16)))

    vt = jnp.concatenate([vp_ref[...], vc_ref[...]], axis=0).astype(jnp.float32).T
    zeros = jnp.zeros((64, 2 * w), jnp.bfloat16)
    ones = jnp.ones((ONES_ROWS, 2 * w), jnp.bfloat16)
    no_ones = jnp.zeros((ONES_ROWS, 2 * w), jnp.bfloat16)

    key = lax.broadcasted_iota(jnp.int32, (2 * w, 4 * w), 0)
    qry = lax.broadcasted_iota(jnp.int32, (2 * w, 4 * w), 1) & (w - 1)
    rel = qry + w - key
    in_band = (rel >= 0) & (rel < w)
    first_key = jnp.where(first, w, 0)
    bias = jnp.where(in_band, 0.0, NEG_INF)
    bias_first = jnp.where(in_band & (key >= first_key), 0.0, NEG_INF)

    def scores(n, g):
        k_lo, k_hi = k_halves[g]
        rows = slice(n * w, (n + 2) * w)
        kk = jnp.concatenate([k_lo[rows], k_hi[rows]], axis=0)
        qs = jnp.concatenate(
            [q_refs[g][n * w:(n + 1) * w, s * LANES:(s + 1) * LANES] for s in range(4)], axis=0)
        st = lax.dot_general(kk, qs, _NT, preferred_element_type=jnp.float32)
        ps, sink_terms = [], []
        for e in range(2):
            blk = st[e * 2 * w:(e + 1) * 2 * w] + (bias_first if n == 0 else bias)
            sink = jnp.concatenate(
                [jnp.full((1, w), sink_ref[8 * g + 2 * s + e] * LOG2E, jnp.float32) for s in range(4)], axis=1)
            m = jnp.maximum(jnp.max(blk, axis=0, keepdims=True), sink)
            ps.append(jnp.exp2(blk - m).astype(jnp.bfloat16))
            sink_terms.append(jnp.exp2(sink - m))
        return jnp.concatenate(ps, axis=0), sink_terms

    def outputs(n, g, pt, sink_terms):
        vg = vt[64 * g:64 * (g + 1), n * w:(n + 2) * w].astype(jnp.bfloat16)
        lhs = jnp.concatenate([
            jnp.concatenate([vg, zeros], axis=1), jnp.concatenate([zeros, vg], axis=1),
            jnp.concatenate([ones, no_ones], axis=1), jnp.concatenate([no_ones, ones], axis=1)], axis=0)
        acc = jnp.dot(lhs, pt, preferred_element_type=jnp.float32)
        den_even = acc[128:129] + sink_terms[0]
        den_odd = acc[128 + ONES_ROWS:129 + ONES_ROWS] + sink_terms[1]
        ot = jnp.concatenate([acc[0:64] * (1.0 / den_even), acc[64:128] * (1.0 / den_odd)], axis=0)
        for s in range(4):
            slab = 4 * g + s
            gate = gate_ref[n * w:(n + 1) * w, slab * LANES:(slab + 1) * LANES]
            o = ot[:, s * w:(s + 1) * w].T
            o_ref[n * w:(n + 1) * w, slab * LANES:(slab + 1) * LANES] = (
                o * gate.astype(jnp.float32)).astype(o_ref.dtype)

    pending = None
    for item in [(n, g) for n in range(blocks) for g in range(2)] + [None]:
        computed = scores(*item) if item is not None else None
        if pending is not None:
            outputs(*pending[0], *pending[1])
        pending = (item, computed)


def _swa(proj, sinks, batch, seq):
    w = SW_WINDOW
    blocks = 16
    rows = blocks * w
    steps = seq // rows
    nb = seq // w
    cur = lambda slab: pl.BlockSpec((rows, LANES), lambda b, i: (b * steps + i, slab))
    prev = lambda slab: pl.BlockSpec(
        (w, LANES), lambda b, i: (jnp.maximum(b * nb + i * blocks - 1, 0), slab))
    wide = lambda width, slab: pl.BlockSpec((rows, width * LANES), lambda b, i: (b * steps + i, slab // width))
    return pl.pallas_call(
        functools.partial(_swa_kernel, blocks=blocks),
        grid=(batch, steps),
        in_specs=[
            pl.BlockSpec(memory_space=pltpu.SMEM),
            wide(4, SLAB_BQ_LO), wide(4, SLAB_BQ_HI),
            cur(SLAB_BK), prev(SLAB_BK), cur(SLAB_BV), prev(SLAB_BV),
            wide(8, SLAB_BG),
        ],
        out_specs=pl.BlockSpec((rows, 1024), lambda b, i: (b * steps + i, 0)),
        out_shape=jax.ShapeDtypeStruct((batch * seq, 1024), jnp.bfloat16),
        compiler_params=_cparams(("parallel", "arbitrary")),
        name="swa",
    )(sinks.astype(jnp.float32), proj, proj, proj, proj, proj, proj, proj)


OUT_SUB_ROWS = 128


def _out_proj_kernel(ya_ref, yb_ref, yc_ref, wf_ref, x_ref, postw_ref, o_ref, w_ref):
    @pl.when(pl.program_id(0) == 0)
    def _():
        w_ref[...] = wf_ref[...].astype(w_ref.dtype)

    for r in range(0, o_ref.shape[0], OUT_SUB_ROWS):
        rows = slice(r, r + OUT_SUB_ROWS)
        out = jnp.dot(ya_ref[rows, :], w_ref[0:512, :], preferred_element_type=jnp.float32)
        out += jnp.dot(yb_ref[rows, :], w_ref[512:1536, :], preferred_element_type=jnp.float32)
        out += jnp.dot(yc_ref[rows, :], w_ref[1536:2048, :], preferred_element_type=jnp.float32)
        ms = jnp.mean(out * out, axis=-1, keepdims=True)
        o_ref[rows, :] = x_ref[rows, :] + out * lax.rsqrt(ms + NORM_EPS) * postw_ref[...]


def _out_proj(ya, yb, yc, w_out, layer, x2, post_w):
    m = x2.shape[0]
    tm = 512
    row = lambda width: pl.BlockSpec((tm, width), lambda i: (i, 0))
    return pl.pallas_call(
        _out_proj_kernel,
        grid=(m // tm,),
        in_specs=[
            row(512), row(1024), row(512),
            pl.BlockSpec((None, D_MODEL, D_MODEL), lambda i: (layer, 0, 0), pipeline_mode=pl.Buffered(1)),
            row(D_MODEL),
            pl.BlockSpec((1, D_MODEL), lambda i: (0, 0)),
        ],
        out_specs=row(D_MODEL),
        out_shape=jax.ShapeDtypeStruct((m, D_MODEL), jnp.float32),
        scratch_shapes=[pltpu.VMEM((D_MODEL, D_MODEL), jnp.bfloat16)],
        compiler_params=_cparams(("arbitrary",)),
        name="out_proj",
    )(ya, yb, yc, w_out, x2, post_w.reshape(1, D_MODEL))


def kernel(x, positions, pre_norm_w, post_norm_w, w_in, diff_lambda_q1, diff_lambda_k1, diff_lambda_q2,
           diff_lambda_k2, diff_subln_w, sink_logits, mla_q_norm_w, mla_kv_norm_w, w_uq, w_ukv, w_out):
    batch, seq, d = x.shape
    depth = w_in.shape[0]
    cos, sin = _rope_tables(positions)
    x2 = x.reshape(batch * seq, d)
    for layer in range(depth):
        lam_init = 0.8 - 0.6 * math.exp(-0.3 * layer)
        lam_rows = jnp.stack([diff_lambda_q1[layer], diff_lambda_k1[layer],
                              diff_lambda_q2[layer], diff_lambda_k2[layer]]).astype(jnp.float32)
        proj = _in_proj(x2, pre_norm_w[layer], w_in, layer, cos, sin)
        ya = _diff_attn(proj, lam_rows, diff_subln_w[layer], batch, seq, lam_init)
        yb = _swa(proj, sink_logits[layer], batch, seq)
        qf, kn, vc = _mla_up(proj, mla_q_norm_w[layer], mla_kv_norm_w[layer],
                             w_uq, w_ukv, layer, cos, sin)
        yc = _mla_attn(qf, kn, vc, proj, batch, seq)
        x2 = _out_proj(ya, yb, yc, w_out, layer, x2, post_norm_w[layer])
    return x2.reshape(batch, seq, d)
```
